```python
import math
import jax, jax.numpy as jnp
from jax import lax
import numpy as np

D_MODEL = 2048
BATCH = 32
SEQ = 256
DEPTH = 2
DEC_BATCH = 4
DEC_SEQ = 4096
PAST_LEN = 512

GRID_W = 64
EPS = 1e-6
CONV_W = 5
CHUNK = 64
Q_BLOCK = 128

GDN_HEADS = 4
GDN_DK = 128
GDN_DV = 128
GDN_W = GDN_HEADS * GDN_DV
GDN_QKV = 2 * GDN_HEADS * GDN_DK + GDN_HEADS * GDN_DV

SSM_HEADS = 16
SSM_HEADDIM = 64
SSM_GROUPS = 2
SSM_DSTATE = 64
SSM_INNER = SSM_HEADS * SSM_HEADDIM
SSM_XBC = SSM_INNER + 2 * SSM_GROUPS * SSM_DSTATE

MLA_HEADS = 8
MLA_NOPE = 64
MLA_ROPE = 32
MLA_VDIM = 64
Q_RANK = 512
KV_RANK = 256
MLA_W = MLA_HEADS * MLA_VDIM
ROPE_THETA = 10000.0

MIX_W = GDN_W + SSM_INNER + MLA_W
IN_SIZES = (GDN_QKV, GDN_W, 2 * GDN_HEADS, 2 * GDN_HEADS, SSM_INNER, SSM_XBC, 2 * SSM_HEADS, Q_RANK, KV_RANK + MLA_ROPE)
IN_W = sum(IN_SIZES)

N_EGROUPS = 4
E_PER_GROUP = 4
N_EXPERTS = N_EGROUPS * E_PER_GROUP
TOPK_IN_GROUP = 2
EXPERT_FF = 512

kernel_name = 'hybrid_gdn_ssd_mla_hmoe_diffusion_step'

f32 = jnp.float32


def rmsnorm(x, g):
    xf = x.astype(f32)
    y = xf * lax.rsqrt(jnp.mean(xf * xf, axis=-1, keepdims=True) + EPS)
    return (y * g.astype(f32)).astype(x.dtype)


def l2norm(x):
    xf = x.astype(f32)
    return xf * lax.rsqrt(jnp.sum(xf * xf, axis=-1, keepdims=True) + EPS)


def split_cols(t, sizes):
    outs, o = [], 0
    for s in sizes:
        outs.append(t[..., o:o + s])
        o += s
    return outs


def dwconv(x, w):
    pad = CONV_W // 2
    T = x.shape[1]
    xp = jnp.pad(x, ((0, 0), (pad, pad), (0, 0)))
    return sum(xp[:, i:i + T] * w[i] for i in range(CONV_W))


def modulation(cvec, w_ada, b_ada):
    m = jax.nn.silu(cvec) @ w_ada + b_ada
    return [t[:, None, :] for t in split_cols(m, (D_MODEL,) * 6)]


def axial_rope_tables(T):
    rows = T // GRID_W
    row = jnp.repeat(jnp.arange(rows, dtype=f32), GRID_W)
    col = jnp.tile(jnp.arange(GRID_W, dtype=f32), rows)
    nf = MLA_ROPE // 4
    inv = jnp.power(ROPE_THETA, -jnp.arange(nf, dtype=f32) / nf)
    ang = jnp.concatenate([row[:, None] * inv, col[:, None] * inv], axis=-1)
    return jnp.cos(ang), jnp.sin(ang)


def apply_rope(x, cos, sin):
    half = MLA_ROPE // 2
    xf = x.astype(f32)
    x1, x2 = xf[..., :half], xf[..., half:]
    return jnp.concatenate([x1 * cos - x2 * sin, x1 * sin + x2 * cos], axis=-1).astype(x.dtype)


def block_attention(q, k, v):
    B, Tq, H, Dqk = q.shape
    nb = Tq // Q_BLOCK
    scale = Dqk ** -0.5
    qb = jnp.moveaxis(q.reshape(B, nb, Q_BLOCK, H, Dqk), 1, 0)

    def one(qblk):
        s = jnp.einsum('bqhd,bshd->bhqs', qblk, k).astype(f32) * scale
        p = jax.nn.softmax(s, axis=-1).astype(v.dtype)
        return jnp.einsum('bhqs,bshe->bqhe', p, v)

    o = lax.map(one, qb)
    return jnp.moveaxis(o, 0, 1).reshape(B, Tq, H, v.shape[-1])


def gdn_chunked(q, k, v, log_a, beta, s0):
    B, T, H, DK = q.shape
    DV = v.shape[-1]
    N, C = T // CHUNK, CHUNK

    def chunks(t):
        t = jnp.swapaxes(t.astype(f32), 1, 2)
        return t.reshape((B, H, N, C) + t.shape[3:])

    qc = chunks(q) * (DK ** -0.5)
    kc, vc = chunks(k), chunks(v)
    g = jnp.cumsum(chunks(log_a), axis=-1)
    bc = chunks(beta)
    idx = jnp.arange(C)
    tril = idx[:, None] >= idx[None, :]
    strict = idx[:, None] > idx[None, :]
    decay = jnp.exp(jnp.where(tril, g[..., :, None] - g[..., None, :], -jnp.inf))
    kk = jnp.einsum('bhnid,bhnjd->bhnij', kc, kc)
    lmat = jnp.where(strict, kk * bc[..., :, None] * decay, 0.0)
    amat = lmat + jnp.eye(C, dtype=f32)
    kb = kc * bc[..., None]
    rhs = jnp.concatenate([vc * bc[..., None], kb * jnp.exp(g)[..., None]], axis=-1)
    sol = lax.linalg.triangular_solve(amat, rhs, left_side=True, lower=True, unit_diagonal=True)
    u, w = sol[..., :DV], sol[..., DV:]
    qk = jnp.where(tril, jnp.einsum('bhnid,bhnjd->bhnij', qc, kc) * decay, 0.0)
    g_last = g[..., -1]
    k_dec = kc * jnp.exp(g_last[..., None] - g)[..., None]
    q_dec = qc * jnp.exp(g)[..., None]

    def step(S, inp):
        u_n, w_n, qk_n, q_n, k_n, gl_n = inp
        v_new = u_n - jnp.einsum('bhcd,bhde->bhce', w_n, S)
        o = jnp.einsum('bhcd,bhde->bhce', q_n, S) + jnp.einsum('bhij,bhje->bhie', qk_n, v_new)
        S = S * jnp.exp(gl_n)[..., None, None] + jnp.einsum('bhcd,bhce->bhde', k_n, v_new)
        return S, o

    xs = tuple(jnp.moveaxis(t, 2, 0) for t in (u, w, qk, q_dec, k_dec, g_last))
    s_fin, o = lax.scan(step, s0.astype(f32), xs)
    o = jnp.moveaxis(o, 0, 2).reshape(B, H, T, DV)
    return jnp.swapaxes(o, 1, 2), s_fin


def ssd_chunked(x, dt, a, bm, cm, h0):
    Bsz, T, H, P = x.shape
    G, NS = bm.shape[2], bm.shape[3]
    R = H // G
    NC, C = T // CHUNK, CHUNK
    xc = x.astype(f32).reshape(Bsz, NC, C, G, R, P)
    dtc = dt.astype(f32).reshape(Bsz, NC, C, G, R)
    acum = jnp.cumsum(dtc * a.reshape(G, R), axis=2)
    bc = bm.astype(f32).reshape(Bsz, NC, C, G, NS)
    cc = cm.astype(f32).reshape(Bsz, NC, C, G, NS)
    idx = jnp.arange(C)
    tril = (idx[:, None] >= idx[None, :])[:, :, None, None]
    seg = acum[:, :, :, None] - acum[:, :, None, :]
    lmat = jnp.exp(jnp.where(tril, seg, -jnp.inf))
    cb = jnp.einsum('bcign,bcjgn->bcijg', cc, bc)
    xdt = xc * dtc[..., None]
    y_diag = jnp.einsum('bcijg,bcijgr,bcjgrp->bcigrp', cb, lmat, xdt)
    decay_states = jnp.exp(acum[:, :, -1:] - acum)
    states = jnp.einsum('bcjgn,bcjgr,bcjgrp->bcgrpn', bc, decay_states, xdt)
    chunk_decay = jnp.exp(acum[:, :, -1])

    def step(h, inp):
        st, dec = inp
        return h * dec[..., None, None] + st, h

    h_init = h0.astype(f32).reshape(Bsz, G, R, P, NS)
    h_fin, h_prev = lax.scan(step, h_init, (jnp.moveaxis(states, 1, 0), jnp.moveaxis(chunk_decay, 1, 0)))
    h_prev = jnp.moveaxis(h_prev, 0, 1)
    y_off = jnp.einsum('bcign,bcgrpn,bcigr->bcigrp', cc, h_prev, jnp.exp(acum))
    y = (y_diag + y_off).reshape(Bsz, T, H, P)
    return y, h_fin.reshape(Bsz, H, P, NS)


def token_mix(h, ctx, lp):
    B, T, _ = h.shape
    act = h.dtype
    proj = h @ lp['w_in']
    g_qkv, g_z, g_a, g_b, s_z, s_xbc, s_dt, m_q, m_kv = split_cols(proj, IN_SIZES)
    rev = lambda t: jnp.flip(t, axis=1)

    qkv = jax.nn.silu(dwconv(g_qkv, lp['gdn_conv']))
    q, k, v = split_cols(qkv, (GDN_HEADS * GDN_DK, GDN_HEADS * GDN_DK, GDN_W))
    q = l2norm(q.reshape(B, T, GDN_HEADS, GDN_DK))
    k = l2norm(k.reshape(B, T, GDN_HEADS, GDN_DK))
    v = v.reshape(B, T, GDN_HEADS, GDN_DV).astype(f32)
    log_alpha = -jnp.exp(lp['gdn_A_log'].astype(f32)) * jax.nn.softplus(
        g_a.reshape(B, T, 2, GDN_HEADS).astype(f32) + lp['gdn_dt_bias'].astype(f32))
    beta = jax.nn.sigmoid(g_b.reshape(B, T, 2, GDN_HEADS).astype(f32))
    gdn_s0 = jnp.zeros((B, 2, GDN_HEADS, GDN_DK, GDN_DV), f32) if ctx is None else ctx[2].astype(f32)
    o_f, s_f = gdn_chunked(q, k, v, log_alpha[:, :, 0], beta[:, :, 0], gdn_s0[:, 0])
    o_b, s_b = gdn_chunked(rev(q), rev(k), rev(v), rev(log_alpha[:, :, 1]), rev(beta[:, :, 1]), gdn_s0[:, 1])
    o = rmsnorm(o_f + rev(o_b), lp['gdn_norm']) * jax.nn.silu(g_z.reshape(B, T, GDN_HEADS, GDN_DV).astype(f32))
    o_gdn = o.reshape(B, T, GDN_W).astype(act)

    xbc = jax.nn.silu(dwconv(s_xbc, lp['ssm_conv']) + lp['ssm_conv_bias'])
    xs, bm, cm = split_cols(xbc, (SSM_INNER, SSM_GROUPS * SSM_DSTATE, SSM_GROUPS * SSM_DSTATE))
    xs = xs.reshape(B, T, SSM_HEADS, SSM_HEADDIM)
    bm = bm.reshape(B, T, SSM_GROUPS, SSM_DSTATE)
    cm = cm.reshape(B, T, SSM_GROUPS, SSM_DSTATE)
    dt = jax.nn.softplus(s_dt.reshape(B, T, 2, SSM_HEADS).astype(f32) + lp['ssm_dt_bias'].astype(f32))
    a = -jnp.exp(lp['ssm_A_log'].astype(f32))
    ssm_h0 = jnp.zeros((B, 2, SSM_HEADS, SSM_HEADDIM, SSM_DSTATE), f32) if ctx is None else ctx[3].astype(f32)
    y_f, h_f = ssd_chunked(xs, dt[:, :, 0], a[0], bm, cm, ssm_h0[:, 0])
    y_b, h_b = ssd_chunked(rev(xs), rev(dt[:, :, 1]), a[1], rev(bm), rev(cm), ssm_h0[:, 1])
    y = y_f + rev(y_b) + xs.astype(f32) * lp['ssm_D'].astype(f32)[:, None]
    y = y.reshape(B, T, SSM_INNER) * jax.nn.silu(s_z.astype(f32))
    o_ssm = rmsnorm(y, lp['ssm_norm']).astype(act)

    cq = rmsnorm(m_q, lp['mla_q_norm'])
    qf = (cq @ lp['mla_w_uq']).reshape(B, T, MLA_HEADS, MLA_NOPE + MLA_ROPE)
    q_nope, q_pe = qf[..., :MLA_NOPE], qf[..., MLA_NOPE:]
    ckv = rmsnorm(m_kv[..., :KV_RANK], lp['mla_kv_norm'])
    kpe = m_kv[..., KV_RANK:]
    if ctx is None:
        ckv_all, kpe_all = ckv, kpe
    else:
        cos, sin = axial_rope_tables(T)
        q_pe = apply_rope(q_pe, cos[None, :, None, :], sin[None, :, None, :])
        kpe_lat = apply_rope(kpe, cos[None], sin[None])
        ckv_all = jnp.concatenate([ctx[0], ckv], axis=1)
        kpe_all = jnp.concatenate([ctx[1], kpe_lat], axis=1)
    S = ckv_all.shape[1]
    kv = (ckv_all @ lp['mla_w_ukv']).reshape(B, S, MLA_HEADS, MLA_NOPE + MLA_VDIM)
    k_nope, v_mla = kv[..., :MLA_NOPE], kv[..., MLA_NOPE:]
    k_full = jnp.concatenate([k_nope, jnp.broadcast_to(kpe_all[:, :, None, :], (B, S, MLA_HEADS, MLA_ROPE))], axis=-1)
    q_full = jnp.concatenate([q_nope, q_pe], axis=-1)
    o_mla = block_attention(q_full, k_full, v_mla).reshape(B, T, MLA_W).astype(act)

    out = jnp.concatenate([o_gdn, o_ssm, o_mla], axis=-1) @ lp['w_out']
    ctx_out = (ckv, kpe, jnp.stack([s_f, s_b], axis=1), jnp.stack([h_f, h_b], axis=1))
    return out, ctx_out


def hier_moe(x, lp):
    B, T, D = x.shape
    xt = x.reshape(B * T, D)
    pg = jax.nn.softmax((xt @ lp['router_group']).astype(f32) + lp['router_group_bias'].astype(f32), axis=-1)
    gsel = jnp.argmax(pg, axis=-1)
    pg_sel = jnp.take_along_axis(pg, gsel[:, None], axis=-1)
    le = ((xt @ lp['router_expert']).astype(f32) + lp['router_expert_bias'].astype(f32)).reshape(-1, N_EGROUPS, E_PER_GROUP)
    le_sel = jnp.take_along_axis(le, gsel[:, None, None], axis=1)[:, 0]
    pe = jax.nn.softmax(le_sel, axis=-1)
    topv, topi = lax.top_k(pe, TOPK_IN_GROUP)
    wts = pg_sel * topv / jnp.sum(topv, axis=-1, keepdims=True)
    eid = gsel[:, None] * E_PER_GROUP + topi
    gate = jnp.sum(jax.nn.one_hot(eid, N_EXPERTS, dtype=f32) * wts[..., None], axis=1).astype(x.dtype)
    out = jnp.zeros_like(xt)
    for e in range(N_EXPERTS):
        hdn = jax.nn.silu(xt @ lp['moe_w_gate'][e]) * (xt @ lp['moe_w_up'][e])
        out = out + (hdn @ lp['moe_w_down'][e]) * gate[:, e:e + 1]
    return out.reshape(B, T, D)


def trunk_layer(x, cvec, ctx, lp):
    sh1, sc1, g1, sh2, sc2, g2 = modulation(cvec, lp['w_ada'], lp['b_ada'])
    h = rmsnorm(x, lp['norm1']) * (1 + sc1) + sh1
    mix_out, ctx_out = token_mix(h, ctx, lp)
    x = x + g1 * mix_out
    h = rmsnorm(x, lp['norm2']) * (1 + sc2) + sh2
    x = x + g2 * hier_moe(h, lp)
    return x, ctx_out


def setup_inputs(seed: int = 0) -> dict:
    key = jax.random.key(seed)
    ks = iter(jax.random.split(key, 48))
    L, D = DEPTH, D_MODEL

    def nrm(shape, scale):
        return jax.random.normal(next(ks), shape, f32) * scale

    def gain(shape):
        return 1.0 + 0.02 * jax.random.normal(next(ks), shape, f32)

    def a_log(shape):
        return jnp.log(jax.random.uniform(next(ks), shape, f32, 1.0, 16.0))

    def dt_bias(shape):
        dt = jnp.exp(jax.random.uniform(next(ks), shape, f32, math.log(1e-3), math.log(1e-1)))
        return dt + jnp.log(-jnp.expm1(-dt))

    return {
        'x_prompt': nrm((BATCH, SEQ, D), 1.0),
        'x_sample': nrm((DEC_BATCH, DEC_SEQ, D), 1.0),
        'cache_ckv': nrm((DEC_BATCH, L, PAST_LEN, KV_RANK), 1.0),
        'cache_kpe': nrm((DEC_BATCH, L, PAST_LEN, MLA_ROPE), 1.0),
        'state_gdn': nrm((DEC_BATCH, L, 2, GDN_HEADS, GDN_DK, GDN_DV), 0.5),
        'state_ssm': nrm((DEC_BATCH, L, 2, SSM_HEADS, SSM_HEADDIM, SSM_DSTATE), 0.5),
        'c': nrm((DEC_BATCH, D), 1.0),
        'c_ctx': nrm((D,), 1.0),
        'w_ada': nrm((L, D, 6 * D), D ** -0.5),
        'b_ada': nrm((L, 6 * D), 0.02),
        'norm1': gain((L, D)),
        'norm2': gain((L, D)),
        'w_in': nrm((L, D, IN_W), D ** -0.5),
        'gdn_conv': nrm((L, CONV_W, GDN_QKV), CONV_W ** -0.5),
        'gdn_A_log': a_log((L, 2, GDN_HEADS)),
        'gdn_dt_bias': dt_bias((L, 2, GDN_HEADS)),
        'gdn_norm': gain((L, GDN_DV)),
        'ssm_conv': nrm((L, CONV_W, SSM_XBC), CONV_W ** -0.5),
        'ssm_conv_bias': nrm((L, SSM_XBC), 0.02),
        'ssm_A_log': a_log((L, 2, SSM_HEADS)),
        'ssm_dt_bias': dt_bias((L, 2, SSM_HEADS)),
        'ssm_D': gain((L, SSM_HEADS)),
        'ssm_norm': gain((L, SSM_INNER)),
        'mla_q_norm': gain((L, Q_RANK)),
        'mla_w_uq': nrm((L, Q_RANK, MLA_HEADS * (MLA_NOPE + MLA_ROPE)), Q_RANK ** -0.5),
        'mla_kv_norm': gain((L, KV_RANK)),
        'mla_w_ukv': nrm((L, KV_RANK, MLA_HEADS * (MLA_NOPE + MLA_VDIM)), KV_RANK ** -0.5),
        'w_out': nrm((L, MIX_W, D), MIX_W ** -0.5),
        'router_group': nrm((L, D, N_EGROUPS), D ** -0.5),
        'router_group_bias': nrm((L, N_EGROUPS), 0.01),
        'router_expert': nrm((L, D, N_EXPERTS), D ** -0.5),
        'router_expert_bias': nrm((L, N_EXPERTS), 0.01),
        'moe_w_gate': nrm((L, N_EXPERTS, D, EXPERT_FF), D ** -0.5),
        'moe_w_up': nrm((L, N_EXPERTS, D, EXPERT_FF), D ** -0.5),
        'moe_w_down': nrm((L, N_EXPERTS, EXPERT_FF, D), EXPERT_FF ** -0.5),
        'final_norm': gain((D,)),
    }


def reference(x_prompt, x_sample, cache_ckv, cache_kpe, state_gdn, state_ssm, c, c_ctx,
              w_ada, b_ada, norm1, norm2, w_in, gdn_conv, gdn_A_log, gdn_dt_bias, gdn_norm,
              ssm_conv, ssm_conv_bias, ssm_A_log, ssm_dt_bias, ssm_D, ssm_norm,
              mla_q_norm, mla_w_uq, mla_kv_norm, mla_w_ukv, w_out,
              router_group, router_group_bias, router_expert, router_expert_bias,
              moe_w_gate, moe_w_up, moe_w_down, final_norm):
    def layer_params(l):
        return {
            'w_ada': w_ada[l], 'b_ada': b_ada[l], 'norm1': norm1[l], 'norm2': norm2[l],
            'w_in': w_in[l], 'gdn_conv': gdn_conv[l], 'gdn_A_log': gdn_A_log[l],
            'gdn_dt_bias': gdn_dt_bias[l], 'gdn_norm': gdn_norm[l],
            'ssm_conv': ssm_conv[l], 'ssm_conv_bias': ssm_conv_bias[l], 'ssm_A_log': ssm_A_log[l],
            'ssm_dt_bias': ssm_dt_bias[l], 'ssm_D': ssm_D[l], 'ssm_norm': ssm_norm[l],
            'mla_q_norm': mla_q_norm[l], 'mla_w_uq': mla_w_uq[l], 'mla_kv_norm': mla_kv_norm[l],
            'mla_w_ukv': mla_w_ukv[l], 'w_out': w_out[l],
            'router_group': router_group[l], 'router_group_bias': router_group_bias[l],
            'router_expert': router_expert[l], 'router_expert_bias': router_expert_bias[l],
            'moe_w_gate': moe_w_gate[l], 'moe_w_up': moe_w_up[l], 'moe_w_down': moe_w_down[l],
        }

    x = x_prompt
    c_ctx_b = c_ctx[None, :]
    ckvs, kpes, gdns, ssms = [], [], [], []
    for l in range(DEPTH):
        x, (ckv_l, kpe_l, gdn_l, ssm_l) = trunk_layer(x, c_ctx_b, None, layer_params(l))
        ckvs.append(ckv_l)
        kpes.append(kpe_l)
        gdns.append(gdn_l.astype(x_prompt.dtype))
        ssms.append(ssm_l.astype(x_prompt.dtype))
    y_prompt = rmsnorm(x, final_norm)
    new_ckv = jnp.stack(ckvs, axis=1)
    new_kpe = jnp.stack(kpes, axis=1)
    new_gdn = jnp.stack(gdns, axis=1)
    new_ssm = jnp.stack(ssms, axis=1)

    x = x_sample
    for l in range(DEPTH):
        ctx = (cache_ckv[:, l], cache_kpe[:, l], state_gdn[:, l], state_ssm[:, l])
        x, _ = trunk_layer(x, c, ctx, layer_params(l))
    y_sample = rmsnorm(x, final_norm)
    return (y_prompt, y_sample, new_ckv, new_kpe, new_gdn, new_ssm)
```

```python
import functools
import math

import numpy as np
import jax
import jax.numpy as jnp
from jax import lax
from jax.experimental import pallas as pl
from jax.experimental.pallas import tpu as pltpu

f32 = jnp.float32
bf16 = jnp.bfloat16

D_MODEL = 2048
DEPTH = 2
GRID_W = 64
EPS = 1e-6
CONV_W = 5
CHUNK = 64

GDN_HEADS = 4
GDN_DK = 128
GDN_DV = 128
GDN_W = GDN_HEADS * GDN_DV
GDN_QKV = 2 * GDN_HEADS * GDN_DK + GDN_HEADS * GDN_DV

SSM_HEADS = 16
SSM_HEADDIM = 64
SSM_GROUPS = 2
SSM_DSTATE = 64
SSM_INNER = SSM_HEADS * SSM_HEADDIM
SSM_XBC = SSM_INNER + 2 * SSM_GROUPS * SSM_DSTATE

MLA_HEADS = 8
MLA_NOPE = 64
MLA_ROPE = 32
MLA_VDIM = 64
Q_RANK = 512
KV_RANK = 256
MLA_W = MLA_HEADS * MLA_VDIM
ROPE_THETA = 10000.0

MIX_W = GDN_W + SSM_INNER + MLA_W
IN_SIZES = (GDN_QKV, GDN_W, 2 * GDN_HEADS, 2 * GDN_HEADS, SSM_INNER, SSM_XBC, 2 * SSM_HEADS, Q_RANK, KV_RANK + MLA_ROPE)

N_EGROUPS = 4
E_PER_GROUP = 4
N_EXPERTS = N_EGROUPS * E_PER_GROUP
EXPERT_FF = 512

LANES = 128
SUBLANES = 8

TB = 256

COL_QKV = 0
COL_GZ = 1536
COL_SZ = 2048
COL_MQ = 3072
COL_CKV = 3584
COL_XBC = 3840
COL_SMALL = 5120
PROJ_W = 5248
SM_GA, SM_GB, SM_DT, SM_KPE = 0, 8, 16, 64


class Layout:
    def __init__(self, n_ctx, t_ctx, n_lat, t_lat):
        self.n_ctx, self.t_ctx, self.n_lat, self.t_lat = n_ctx, t_ctx, n_lat, t_lat
        self.m_ctx = n_ctx * t_ctx
        self.m_lat = n_lat * t_lat
        self.m = self.m_ctx + self.m_lat
        self.n_seq = n_ctx + n_lat
        bc, bl = t_ctx // TB, t_lat // TB
        seq, first, last, mod = [], [], [], []
        for s in range(n_ctx):
            for j in range(bc):
                seq.append(s); first.append(int(j == 0)); last.append(int(j == bc - 1)); mod.append(0)
        for s in range(n_lat):
            for j in range(bl):
                seq.append(n_ctx + s); first.append(int(j == 0)); last.append(int(j == bl - 1)); mod.append(1 + s)
        self.n_blk = len(seq)
        self.seq = np.array(seq, np.int32)
        self.first = np.array(first, np.int32)
        self.last = np.array(last, np.int32)
        self.mod = np.array(mod + mod[-1:], np.int32)
        fwd = np.arange(self.n_blk, dtype=np.int32)
        bwd = []
        i = 0
        while i < self.n_blk:
            j = i
            while self.last[j] == 0:
                j += 1
            bwd.extend(range(j, i - 1, -1))
            i = j + 1
        self.order = {False: fwd, True: np.array(bwd, np.int32)}

    def scan_tables(self, rev):
        order = self.order[rev]
        pad = lambda a: jnp.asarray(np.concatenate([a, a[-1:]]))
        return pad(order), pad(self.seq[order]), pad(self.first[order]), pad(self.last[order])


def _mm(a, b):
    return jnp.dot(a.astype(bf16), b.astype(bf16), preferred_element_type=f32)


def _mm_nt(a, b):
    return lax.dot_general(a.astype(bf16), b.astype(bf16), (((1,), (1,)), ((), ())), preferred_element_type=f32)


def _mm_tn(a, b):
    n = a.shape[1]
    eye = jnp.where(lax.broadcasted_iota(jnp.int32, (n, n), 0) == lax.broadcasted_iota(jnp.int32, (n, n), 1),
                    1.0, 0.0).astype(bf16)
    at = lax.dot_general(eye, a.astype(bf16), (((1,), (1,)), ((), ())), preferred_element_type=f32)
    return jnp.dot(at.astype(bf16), b.astype(bf16), preferred_element_type=f32)


def _split3(x):
    hi = x.astype(bf16)
    r = x - hi.astype(f32)
    mid = r.astype(bf16)
    lo = (r - mid.astype(f32)).astype(bf16)
    return hi, mid, lo


def _mm01(m01, x):
    hi, mid, lo = _split3(x)
    d = lambda t: jnp.dot(m01, t, preferred_element_type=f32)
    return d(hi) + d(mid) + d(lo)


def _mm3(a, b):
    ah = a.astype(bf16)
    al = (a - ah.astype(f32)).astype(bf16)
    bh = b.astype(bf16)
    bl = (b - bh.astype(f32)).astype(bf16)
    d = lambda x, y: jnp.dot(x, y, preferred_element_type=f32)
    return d(ah, bh) + d(ah, bl) + d(al, bh)


def _sigmoid(x):
    return 1.0 / (1.0 + jnp.exp(-x))


def _silu(x):
    return x * _sigmoid(x)


def _softplus(x):
    return jnp.maximum(x, 0.0) + jnp.log(1.0 + jnp.exp(-jnp.abs(x)))


def _chunk_masks(rev):
    i = lax.broadcasted_iota(jnp.int32, (CHUNK, CHUNK), 0)
    j = lax.broadcasted_iota(jnp.int32, (CHUNK, CHUNK), 1)
    if rev:
        incl, strict = j >= i, j > i
    else:
        incl, strict = j <= i, j < i
    ll = jnp.where(incl, 1.0, 0.0).astype(bf16)
    lls = jnp.where(incl, 0.0, 1.0).astype(bf16)
    uu = jnp.where(strict, 1.0, 0.0)
    return incl, strict, ll, lls, uu


def _fill_ext(ext_ref, main_ref, prev_ref, next_ref, sfirst, slast):
    ext_ref[0:SUBLANES, :] = jnp.where(sfirst == 1, 0.0, prev_ref[...])
    ext_ref[SUBLANES:SUBLANES + TB, :] = main_ref[...]
    ext_ref[SUBLANES + TB:2 * SUBLANES + TB, :] = jnp.where(slast == 1, 0.0, next_ref[...])


def _conv_tile(ext_ref, w_ref, r0, col0, width, bias=None):
    acc = None
    for t in range(CONV_W):
        rows = ext_ref[pl.ds(SUBLANES + r0 - CONV_W // 2 + t, CHUNK), col0:col0 + width]
        term = rows * w_ref[t:t + 1, col0:col0 + width]
        acc = term if acc is None else acc + term
    if bias is not None:
        acc = acc + bias
    return _silu(acc)


INV_BLOCK = 16


def _block_masks():
    i = lax.broadcasted_iota(jnp.int32, (CHUNK, CHUNK), 0)
    j = lax.broadcasted_iota(jnp.int32, (CHUNK, CHUNK), 1)
    blk16 = (i // INV_BLOCK) == (j // INV_BLOCK)
    blk32 = (i // (2 * INV_BLOCK)) == (j // (2 * INV_BLOCK))
    return blk16, blk32


def _unit_tri_inverse(lmat, eye, blk16, blk32):
    ld = jnp.where(blk16, lmat, 0.0)
    t = eye - ld
    pw = _mm3(ld, ld)
    for it in range(3):
        t = t + _mm3(t, pw)
        if it < 2:
            pw = _mm3(pw, pw)
    off32 = jnp.where(jnp.logical_and(blk32, jnp.logical_not(blk16)), lmat, 0.0)
    t = t - _mm3(_mm3(t, off32), t)
    off64 = jnp.where(blk32, 0.0, lmat)
    return t - _mm3(_mm3(t, off64), t)


def _gdn_kernel(order_ref, seq_ref, first_ref, last_ref,
                main_ref, prev_ref, next_ref, small_ref, convw_ref, par_ref, s0_ref,
                o_ref, sfin_ref, s_ref, ext_ref, *, rev):
    del order_ref, seq_ref
    step = pl.program_id(0)
    sfirst = first_ref[step]
    slast = last_ref[step]
    pfirst, plast = (slast, sfirst) if rev else (sfirst, slast)
    d = 1 if rev else 0

    @pl.when(pfirst == 1)
    def _():
        s_ref[...] = s0_ref[...]

    _fill_ext(ext_ref, main_ref, prev_ref, next_ref, sfirst, slast)

    incl, strict, ll, lls, uu = _chunk_masks(rev)
    eye = jnp.where(lax.broadcasted_iota(jnp.int32, (CHUNK, CHUNK), 0)
                    == lax.broadcasted_iota(jnp.int32, (CHUNK, CHUNK), 1), 1.0, 0.0)
    blk16, blk32 = _block_masks()
    neg_a = -jnp.exp(par_ref[0:1, :])
    dt_bias = par_ref[1:2, :]
    n_chunks = TB // CHUNK

    for cc in range(n_chunks):
        c = n_chunks - 1 - cc if rev else cc
        r0 = c * CHUNK
        sm = small_ref[r0:r0 + CHUNK, :]
        la_blk = neg_a * _softplus(sm + dt_bias)
        beta_blk = _sigmoid(sm)
        g_blk = _mm01(ll, la_blk)
        gr_blk = _mm01(lls, la_blk)
        gl_blk = jnp.sum(la_blk, axis=0, keepdims=True)
        for h in range(GDN_HEADS):
            ia = SM_GA + d * GDN_HEADS + h
            ib = SM_GB + d * GDN_HEADS + h
            la_c = la_blk[:, ia:ia + 1]
            beta_c = beta_blk[:, ib:ib + 1]
            g_c = g_blk[:, ia:ia + 1]
            gr_c = gr_blk[:, ia:ia + 1]
            gl_s = gl_blk[:, ia:ia + 1]

            q = _conv_tile(ext_ref, convw_ref, r0, h * GDN_DK, GDN_DK)
            k = _conv_tile(ext_ref, convw_ref, r0, GDN_HEADS * GDN_DK + h * GDN_DK, GDN_DK)
            v = _conv_tile(ext_ref, convw_ref, r0, 2 * GDN_HEADS * GDN_DK + h * GDN_DV, GDN_DV)
            q = q * lax.rsqrt(jnp.sum(q * q, axis=-1, keepdims=True) + EPS) * (GDN_DK ** -0.5)
            k = k * lax.rsqrt(jnp.sum(k * k, axis=-1, keepdims=True) + EPS)

            dmat = _mm01(ll, la_c * uu)
            decay = jnp.where(incl, jnp.exp(dmat), 0.0)
            kk = _mm_nt(k, k)
            lmat = jnp.where(strict, kk * beta_c * decay, 0.0)
            tinv = _unit_tri_inverse(lmat, eye, blk16, blk32)
            kb = k * beta_c
            u = _mm3(tinv, v * beta_c)
            w = _mm3(tinv, kb * jnp.exp(g_c))
            qk = jnp.where(incl, _mm_nt(q, k) * decay, 0.0)
            k_dec = k * jnp.exp(gr_c)
            q_dec = q * jnp.exp(g_c)

            s_h = s_ref[h]
            v_new = u - _mm(w, s_h)
            o = _mm(q_dec, s_h) + _mm(qk, v_new)
            s_ref[h] = s_h * jnp.exp(gl_s) + _mm_tn(k_dec, v_new)
            o_ref[r0:r0 + CHUNK, h * GDN_DV:(h + 1) * GDN_DV] = o

    @pl.when(plast == 1)
    def _():
        sfin_ref[...] = s_ref[...]


def _halo_maps(n_rows8):
    prev_map = lambda i, order, *_: (jnp.maximum(order[i] * (TB // SUBLANES) - 1, 0), 0)
    next_map = lambda i, order, *_: (jnp.minimum((order[i] + 1) * (TB // SUBLANES), n_rows8 - 1), 0)
    return prev_map, next_map


def gdn_scan(lay, proj, convw, par, s0, layer, rev):
    prev_map, next_map = _halo_maps(lay.m // SUBLANES)
    blk_map = lambda i, order, *_: (order[i], 0)
    seq_map = lambda i, order, seq, *_: (seq[i], 0, 0, 0)
    grid_spec = pltpu.PrefetchScalarGridSpec(
        num_scalar_prefetch=4,
        grid=(lay.n_blk,),
        in_specs=[
            pl.BlockSpec((TB, GDN_QKV), blk_map),
            pl.BlockSpec((SUBLANES, GDN_QKV), prev_map),
            pl.BlockSpec((SUBLANES, GDN_QKV), next_map),
            pl.BlockSpec((TB, LANES), lambda i, order, *_: (order[i], COL_SMALL // LANES)),
            pl.BlockSpec((None, CONV_W, GDN_QKV), lambda i, *_: (layer, 0, 0)),
            pl.BlockSpec((None, SUBLANES, LANES), lambda i, *_: (layer, 0, 0)),
            pl.BlockSpec((None, GDN_HEADS, GDN_DK, GDN_DV), seq_map),
        ],
        out_specs=[
            pl.BlockSpec((TB, GDN_W), blk_map),
            pl.BlockSpec((None, GDN_HEADS, GDN_DK, GDN_DV), seq_map),
        ],
        scratch_shapes=[
            pltpu.VMEM((GDN_HEADS, GDN_DK, GDN_DV), f32),
            pltpu.VMEM((TB + 2 * SUBLANES, GDN_QKV), f32),
        ],
    )
    return pl.pallas_call(
        functools.partial(_gdn_kernel, rev=rev),
        grid_spec=grid_spec,
        out_shape=[
            jax.ShapeDtypeStruct((lay.m, GDN_W), f32),
            jax.ShapeDtypeStruct((lay.n_seq, GDN_HEADS, GDN_DK, GDN_DV), f32),
        ],
        compiler_params=pltpu.CompilerParams(dimension_semantics=("arbitrary",)),
        name="gdn_bwd" if rev else "gdn_fwd",
    )(*lay.scan_tables(rev), proj, proj, proj, proj, convw, par, s0)


SSM_PAIRS = SSM_HEADS // 2
PAIRS_PER_GROUP = SSM_PAIRS // SSM_GROUPS
COL_B = SSM_INNER
COL_C = SSM_INNER + SSM_GROUPS * SSM_DSTATE


def _ssd_kernel(order_ref, seq_ref, first_ref, last_ref,
                main_ref, prev_ref, next_ref, small_ref, convw_ref, cpar_ref, par_ref, h0_ref,
                y_ref, hfin_ref, h_ref, ext_ref, *, rev):
    del order_ref, seq_ref
    step = pl.program_id(0)
    sfirst = first_ref[step]
    slast = last_ref[step]
    pfirst, plast = (slast, sfirst) if rev else (sfirst, slast)
    d = 1 if rev else 0

    @pl.when(pfirst == 1)
    def _():
        h_ref[...] = h0_ref[...]

    _fill_ext(ext_ref, main_ref, prev_ref, next_ref, sfirst, slast)

    incl, _, ll, lls, uu = _chunk_masks(rev)
    lane_lo = lax.broadcasted_iota(jnp.int32, (CHUNK, LANES), 1) < SSM_DSTATE
    row_lo = lax.broadcasted_iota(jnp.int32, (LANES, LANES), 0) < SSM_HEADDIM
    neg_a = -jnp.exp(par_ref[0:1, :])
    dt_bias = par_ref[1:2, :]
    n_chunks = TB // CHUNK

    for cc in range(n_chunks):
        c = n_chunks - 1 - cc if rev else cc
        r0 = c * CHUNK
        sm = small_ref[r0:r0 + CHUNK, :]
        dt_blk = _softplus(sm + dt_bias)
        dta_blk = dt_blk * neg_a
        acum_blk = _mm01(ll, dta_blk)
        ar_blk = _mm01(lls, dta_blk)
        al_blk = jnp.sum(dta_blk, axis=0, keepdims=True)
        b_pair = _conv_tile(ext_ref, convw_ref, r0, COL_B, LANES, cpar_ref[0:1, COL_B:COL_B + LANES])
        c_pair = _conv_tile(ext_ref, convw_ref, r0, COL_C, LANES, cpar_ref[0:1, COL_C:COL_C + LANES])
        for g in range(SSM_GROUPS):
            gmask = lane_lo if g == 0 else jnp.logical_not(lane_lo)
            c_g = jnp.where(gmask, c_pair, 0.0)
            b_g = jnp.where(gmask, b_pair, 0.0)
            cb = _mm_nt(c_g, b_pair)
            for pp in range(PAIRS_PER_GROUP):
                p = g * PAIRS_PER_GROUP + pp
                col = p * LANES
                xs = _conv_tile(ext_ref, convw_ref, r0, col, LANES, cpar_ref[0:1, col:col + LANES])
                cols = []
                for hh in (2 * p, 2 * p + 1):
                    il = SM_DT + d * SSM_HEADS + hh
                    dta_c = dta_blk[:, il:il + 1]
                    dmat = _mm01(ll, dta_c * uu)
                    m_h = cb * jnp.where(incl, jnp.exp(dmat), 0.0)
                    cols.append((dt_blk[:, il:il + 1], acum_blk[:, il:il + 1], ar_blk[:, il:il + 1],
                                 al_blk[:, il:il + 1], m_h))
                (dt0, ac0, ar0, al0, m0), (dt1, ac1, ar1, al1, m1) = cols
                xdt = xs * jnp.where(lane_lo, dt0, dt1)
                y = jnp.where(lane_lo, _mm(m0, xdt), _mm(m1, xdt))
                st_new = jnp.where(row_lo, _mm_tn(xdt, b_g * jnp.exp(ar0)), _mm_tn(xdt, b_g * jnp.exp(ar1)))
                h_pair = h_ref[col:col + LANES, :]
                y = y + _mm_nt(c_g, h_pair) * jnp.where(lane_lo, jnp.exp(ac0), jnp.exp(ac1))
                if not rev:
                    y = y + xs * cpar_ref[1:2, col:col + LANES]
                h_ref[col:col + LANES, :] = h_pair * jnp.where(row_lo, jnp.exp(al0), jnp.exp(al1)) + st_new
                y_ref[r0:r0 + CHUNK, col:col + LANES] = y

    @pl.when(plast == 1)
    def _():
        hfin_ref[...] = h_ref[...]


def ssd_scan(lay, proj, convw, cpar, par, h0, layer, rev):
    prev_map, next_map = _halo_maps(lay.m // SUBLANES)
    cblk = COL_XBC // SSM_XBC
    blk_map = lambda i, order, *_: (order[i], 0)
    seq_map = lambda i, order, seq, *_: (seq[i], 0, 0)
    grid_spec = pltpu.PrefetchScalarGridSpec(
        num_scalar_prefetch=4,
        grid=(lay.n_blk,),
        in_specs=[
            pl.BlockSpec((TB, SSM_XBC), lambda i, order, *_: (order[i], cblk)),
            pl.BlockSpec((SUBLANES, SSM_XBC), lambda i, *a: (prev_map(i, *a)[0], cblk)),
            pl.BlockSpec((SUBLANES, SSM_XBC), lambda i, *a: (next_map(i, *a)[0], cblk)),
            pl.BlockSpec((TB, LANES), lambda i, order, *_: (order[i], COL_SMALL // LANES)),
            pl.BlockSpec((None, CONV_W, SSM_XBC), lambda i, *_: (layer, 0, 0)),
            pl.BlockSpec((None, SUBLANES, SSM_XBC), lambda i, *_: (layer, 0, 0)),
            pl.BlockSpec((None, SUBLANES, LANES), lambda i, *_: (layer, 0, 0)),
            pl.BlockSpec((None, SSM_INNER, LANES), seq_map),
        ],
        out_specs=[
            pl.BlockSpec((TB, SSM_INNER), blk_map),
            pl.BlockSpec((None, SSM_INNER, LANES), seq_map),
        ],
        scratch_shapes=[
            pltpu.VMEM((SSM_INNER, LANES), f32),
            pltpu.VMEM((TB + 2 * SUBLANES, SSM_XBC), f32),
        ],
    )
    return pl.pallas_call(
        functools.partial(_ssd_kernel, rev=rev),
        grid_spec=grid_spec,
        out_shape=[
            jax.ShapeDtypeStruct((lay.m, SSM_INNER), f32),
            jax.ShapeDtypeStruct((lay.n_seq, SSM_INNER, LANES), f32),
        ],
        compiler_params=pltpu.CompilerParams(dimension_semantics=("arbitrary",)),
        name="ssd_bwd" if rev else "ssd_fwd",
    )(*lay.scan_tables(rev), proj, proj, proj, proj, convw, cpar, par, h0)


def ssm_state_to_lanes(h):
    n = h.shape[0]
    hg = h.reshape(n, SSM_GROUPS, SSM_HEADS // SSM_GROUPS * SSM_HEADDIM, SSM_DSTATE)
    z = jnp.zeros_like(hg[:, 0])
    return jnp.concatenate([jnp.concatenate([hg[:, 0], z], -1), jnp.concatenate([z, hg[:, 1]], -1)], 1)


def ssm_state_from_lanes(hl):
    n = hl.shape[0]
    r = hl.reshape(n, SSM_GROUPS, SSM_INNER // SSM_GROUPS, SSM_GROUPS, SSM_DSTATE)
    h = jnp.stack([r[:, g, :, g, :] for g in range(SSM_GROUPS)], 1)
    return h.reshape(n, SSM_HEADS, SSM_HEADDIM, SSM_DSTATE)


ADA_TN = 1536
VMEM_LIMIT = 56 * 1024 * 1024


def _ada_kernel(c_ref, w_ref, b_ref, o_ref):
    s = _silu(c_ref[...])
    o_ref[...] = jnp.dot(s.astype(bf16), w_ref[...].astype(bf16), preferred_element_type=f32) + b_ref[...]


def ada_mod(cvec, w_ada, b_ada, layer):
    n = 6 * D_MODEL
    out = pl.pallas_call(
        _ada_kernel,
        grid=(n // ADA_TN,),
        in_specs=[
            pl.BlockSpec((SUBLANES, D_MODEL), lambda j: (0, 0)),
            pl.BlockSpec((None, D_MODEL, ADA_TN), lambda j: (layer, 0, j)),
            pl.BlockSpec((None, 1, ADA_TN), lambda j: (layer, 0, j)),
        ],
        out_specs=pl.BlockSpec((SUBLANES, ADA_TN), lambda j: (0, j)),
        out_shape=jax.ShapeDtypeStruct((SUBLANES, n), f32),
        compiler_params=pltpu.CompilerParams(dimension_semantics=("arbitrary",), vmem_limit_bytes=VMEM_LIMIT),
        name="ada_mod",
    )(cvec, w_ada, b_ada)
    return out.reshape(SUBLANES, 6, D_MODEL)


MOD_SH1, MOD_SC1, MOD_G1, MOD_SH2, MOD_SC2, MOD_G2 = range(6)


def _rms(x):
    return x * lax.rsqrt(jnp.mean(x * x, axis=-1, keepdims=True) + EPS)


PROJ_CHUNK = 512


def _in_proj_kernel(mod_ref, *refs, has_res):
    del mod_ref
    if has_res:
        x_ref, y_ref, pmods_ref, mods_ref, n1_ref, w_ref, proj_ref, xo_ref = refs
        x = x_ref[...] + pmods_ref[MOD_G2:MOD_G2 + 1, :] * y_ref[...]
        xo_ref[...] = x
    else:
        x_ref, mods_ref, n1_ref, w_ref, proj_ref = refs
        x = x_ref[...]
    h = _rms(x) * n1_ref[...]
    h = h * (1.0 + mods_ref[MOD_SC1:MOD_SC1 + 1, :]) + mods_ref[MOD_SH1:MOD_SH1 + 1, :]
    hb = h.astype(bf16)
    for c0 in range(0, PROJ_W, PROJ_CHUNK):
        wd = min(PROJ_CHUNK, PROJ_W - c0)
        proj_ref[:, c0:c0 + wd] = jnp.dot(hb, w_ref[:, c0:c0 + wd], preferred_element_type=f32)


def in_proj(lay, x, mods, norm1, w_in_r, layer, res=None):
    row_map = lambda i, mod: (i, 0)
    mod_map = lambda i, mod: (mod[i], 0, 0)
    in_specs = [pl.BlockSpec((TB, D_MODEL), row_map)]
    args = [x]
    if res is not None:
        in_specs += [pl.BlockSpec((TB, D_MODEL), row_map), pl.BlockSpec((None, 6, D_MODEL), mod_map)]
        args += [res[0], res[1]]
    in_specs += [
        pl.BlockSpec((None, 6, D_MODEL), mod_map),
        pl.BlockSpec((None, 1, D_MODEL), lambda i, mod: (layer, 0, 0)),
        pl.BlockSpec((D_MODEL, PROJ_W), lambda i, mod: (0, 0), pipeline_mode=pl.Buffered(1)),
    ]
    args += [mods, norm1, w_in_r]
    out_specs = [pl.BlockSpec((TB, PROJ_W), row_map)]
    out_shape = [jax.ShapeDtypeStruct((lay.m, PROJ_W), f32)]
    if res is not None:
        out_specs.append(pl.BlockSpec((TB, D_MODEL), row_map))
        out_shape.append(jax.ShapeDtypeStruct((lay.m, D_MODEL), f32))
    outs = pl.pallas_call(
        functools.partial(_in_proj_kernel, has_res=res is not None),
        grid_spec=pltpu.PrefetchScalarGridSpec(
            num_scalar_prefetch=1, grid=(lay.n_blk,), in_specs=in_specs, out_specs=out_specs),
        out_shape=out_shape,
        compiler_params=pltpu.CompilerParams(dimension_semantics=("arbitrary",), vmem_limit_bytes=VMEM_LIMIT),
        name="in_proj",
    )(jnp.asarray(lay.mod), *args)
    return (outs[0], outs[1]) if res is not None else (outs[0], x)


MLA_QK = MLA_NOPE + MLA_ROPE
Q_HEADS_W = MLA_HEADS * LANES
ROPE_HALF = MLA_ROPE // 2


def _rope_lanes(x, c, s1, s2):
    return x * c + pltpu.roll(x, ROPE_HALF, 1) * s1 + pltpu.roll(x, LANES - ROPE_HALF, 1) * s2


def _mla_prep_kernel(pos_ref, mq_ref, ckv_ref, small_ref, qn_ref, kvn_ref, wuq_ref, rc_ref, rs1_ref, rs2_ref,
                     q_ref, ckvn_ref, kpe_ref):
    del pos_ref
    c, s1, s2 = rc_ref[...], rs1_ref[...], rs2_ref[...]
    cq = (_rms(mq_ref[...]) * qn_ref[...]).astype(bf16)
    qf = jnp.dot(cq, wuq_ref[...], preferred_element_type=f32)
    scale = MLA_QK ** -0.5
    for h in range(MLA_HEADS):
        xh = qf[:, h * LANES:(h + 1) * LANES]
        q_ref[:, h * LANES:(h + 1) * LANES] = (_rope_lanes(xh, c, s1, s2) * scale).astype(bf16)
    ckvn_ref[...] = _rms(ckv_ref[...]) * kvn_ref[...]
    lane = lax.broadcasted_iota(jnp.int32, (TB, LANES), 1)
    in_rope = jnp.logical_and(lane >= SM_KPE, lane < SM_KPE + MLA_ROPE)
    kpe_ref[...] = jnp.where(in_rope, _rope_lanes(small_ref[...], c, s1, s2), 0.0)


def rope_tables(lay):
    t = lay.t_lat
    rows = t // GRID_W
    row = jnp.repeat(jnp.arange(rows, dtype=f32), GRID_W)
    col = jnp.tile(jnp.arange(GRID_W, dtype=f32), rows)
    nf = MLA_ROPE // 4
    inv = jnp.power(ROPE_THETA, -jnp.arange(nf, dtype=f32) / nf)
    ang = jnp.concatenate([row[:, None] * inv, col[:, None] * inv], axis=-1)
    cos, sin = jnp.cos(ang), jnp.sin(ang)
    z = lambda n: jnp.zeros((t, n), f32)
    c = jnp.concatenate([jnp.ones((t, MLA_NOPE), f32), cos, cos, z(LANES - MLA_QK)], -1)
    s1 = jnp.concatenate([z(MLA_NOPE + ROPE_HALF), sin, z(LANES - MLA_QK)], -1)
    s2 = jnp.concatenate([z(MLA_NOPE), -sin, z(ROPE_HALF + LANES - MLA_QK)], -1)
    ident = jnp.concatenate([jnp.ones((TB, MLA_QK), f32), jnp.zeros((TB, LANES - MLA_QK), f32)], -1)
    zero = jnp.zeros((TB, LANES), f32)
    return (jnp.concatenate([ident, c]), jnp.concatenate([zero, s1]), jnp.concatenate([zero, s2]))


def _pos_blocks(lay):
    pos = [0] * (lay.m_ctx // TB)
    for _ in range(lay.n_lat):
        pos += [1 + j for j in range(lay.t_lat // TB)]
    return np.array(pos + pos[-1:], np.int32)


def mla_prep(lay, proj, q_norm, kv_norm, w_uq_r, tables, layer):
    row_map = lambda i, pos: (i, 0)
    tab_map = lambda i, pos: (pos[i], 0)
    return pl.pallas_call(
        _mla_prep_kernel,
        grid_spec=pltpu.PrefetchScalarGridSpec(
            num_scalar_prefetch=1, grid=(lay.n_blk,),
            in_specs=[
                pl.BlockSpec((TB, Q_RANK), lambda i, pos: (i, COL_MQ // Q_RANK)),
                pl.BlockSpec((TB, KV_RANK), lambda i, pos: (i, COL_CKV // KV_RANK)),
                pl.BlockSpec((TB, LANES), lambda i, pos: (i, COL_SMALL // LANES)),
                pl.BlockSpec((None, 1, Q_RANK), lambda i, pos: (layer, 0, 0)),
                pl.BlockSpec((None, 1, KV_RANK), lambda i, pos: (layer, 0, 0)),
                pl.BlockSpec((Q_RANK, Q_HEADS_W), lambda i, pos: (0, 0)),
                pl.BlockSpec((TB, LANES), tab_map),
                pl.BlockSpec((TB, LANES), tab_map),
                pl.BlockSpec((TB, LANES), tab_map),
            ],
            out_specs=[
                pl.BlockSpec((TB, Q_HEADS_W), row_map),
                pl.BlockSpec((TB, KV_RANK), row_map),
                pl.BlockSpec((TB, LANES), row_map),
            ]),
        out_shape=[
            jax.ShapeDtypeStruct((lay.m, Q_HEADS_W), bf16),
            jax.ShapeDtypeStruct((lay.m, KV_RANK), f32),
            jax.ShapeDtypeStruct((lay.m, LANES), f32),
        ],
        compiler_params=pltpu.CompilerParams(dimension_semantics=("arbitrary",)),
        name="mla_prep",
    )(jnp.asarray(_pos_blocks(lay)), proj, proj, proj, q_norm, kv_norm, w_uq_r, *tables)


V_PAIRS = MLA_HEADS // 2
KV_UP_W = MLA_HEADS * LANES + MLA_HEADS * MLA_VDIM


def _kv_up_kernel(ckv_ref, kpe_ref, w_ref, k_ref, v_ref):
    kv = jnp.dot(ckv_ref[...].astype(bf16), w_ref[...], preferred_element_type=f32)
    kpe = kpe_ref[...]
    for h in range(MLA_HEADS):
        k_ref[h] = (kv[:, h * LANES:(h + 1) * LANES] + kpe).astype(bf16)
    v0 = MLA_HEADS * LANES
    for p in range(V_PAIRS):
        v_ref[p] = kv[:, v0 + p * LANES:v0 + (p + 1) * LANES].astype(bf16)


def kv_up(ckv, kpe, w_ukv_r):
    r = ckv.shape[0]
    return pl.pallas_call(
        _kv_up_kernel,
        grid=(r // TB,),
        in_specs=[
            pl.BlockSpec((TB, KV_RANK), lambda i: (i, 0)),
            pl.BlockSpec((TB, LANES), lambda i: (i, 0)),
            pl.BlockSpec((KV_RANK, KV_UP_W), lambda i: (0, 0)),
        ],
        out_specs=[
            pl.BlockSpec((MLA_HEADS, TB, LANES), lambda i: (0, i, 0)),
            pl.BlockSpec((V_PAIRS, TB, LANES), lambda i: (0, i, 0)),
        ],
        out_shape=[
            jax.ShapeDtypeStruct((MLA_HEADS, r, LANES), bf16),
            jax.ShapeDtypeStruct((V_PAIRS, r, LANES), bf16),
        ],
        compiler_params=pltpu.CompilerParams(dimension_semantics=("arbitrary",)),
        name="kv_up",
    )(ckv, kpe, w_ukv_r)


ATT_KC = 512


def _attn_kernel(q_ref, k_ref, v_ref, o_ref, m_ref, l_ref, acc_ref, *, n_keys):
    lane_lo = lax.broadcasted_iota(jnp.int32, (TB, LANES), 1) < MLA_VDIM
    kc = min(ATT_KC, n_keys)
    for p in range(V_PAIRS):
        for slot, h in enumerate((2 * p, 2 * p + 1)):
            q = q_ref[:, h * LANES:(h + 1) * LANES]
            m_ref[...] = jnp.full((TB, LANES), -jnp.inf, f32)
            l_ref[...] = jnp.zeros((TB, LANES), f32)
            acc_ref[slot] = jnp.zeros((TB, LANES), f32)

            def body(c, carry, q=q, h=h, p=p, slot=slot):
                c0 = pl.multiple_of(c * kc, kc)
                s = lax.dot_general(q, k_ref[h, pl.ds(c0, kc), :], (((1,), (1,)), ((), ())),
                                    preferred_element_type=f32)
                m_old = m_ref[...]
                m_new = jnp.maximum(m_old, jnp.max(s, axis=-1, keepdims=True))
                alpha = jnp.exp(m_old - m_new)
                pe = jnp.exp(s - m_new[:, 0:1])
                l_ref[...] = alpha * l_ref[...] + jnp.sum(pe, axis=-1, keepdims=True)
                acc_ref[slot] = alpha * acc_ref[slot] + jnp.dot(
                    pe.astype(bf16), v_ref[p, pl.ds(c0, kc), :], preferred_element_type=f32)
                m_ref[...] = m_new
                return carry

            lax.fori_loop(0, n_keys // kc, body, 0)
            acc_ref[slot] = acc_ref[slot] / l_ref[...]
        o_ref[:, p * LANES:(p + 1) * LANES] = jnp.where(lane_lo, acc_ref[0], acc_ref[1])


def attention(q, k, v, q_blk0, n_seq, t_q, n_keys):
    qb = t_q // TB
    return pl.pallas_call(
        functools.partial(_attn_kernel, n_keys=n_keys),
        grid=(n_seq, qb),
        in_specs=[
            pl.BlockSpec((TB, Q_HEADS_W), lambda b, j: (q_blk0 + b * qb + j, 0)),
            pl.BlockSpec((MLA_HEADS, n_keys, LANES), lambda b, j: (0, b, 0)),
            pl.BlockSpec((V_PAIRS, n_keys, LANES), lambda b, j: (0, b, 0)),
        ],
        out_specs=pl.BlockSpec((TB, MLA_W), lambda b, j: (b * qb + j, 0)),
        out_shape=jax.ShapeDtypeStruct((n_seq * t_q, MLA_W), f32),
        scratch_shapes=[pltpu.VMEM((TB, LANES), f32), pltpu.VMEM((TB, LANES), f32),
                        pltpu.VMEM((2, TB, LANES), f32)],
        compiler_params=pltpu.CompilerParams(dimension_semantics=("arbitrary", "arbitrary"),
                                             vmem_limit_bytes=VMEM_LIMIT),
        name="mla_attn",
    )(q, k, v)


N_PAIRS = E_PER_GROUP * (E_PER_GROUP - 1) // 2
N_BUCKETS = N_EGROUPS * N_PAIRS
RT_BUCKET, RT_WLO, RT_WHI = 0, 1, 2


def _route(lg):
    lane = lax.broadcasted_iota(jnp.int32, lg.shape, 1)
    big = jnp.int32(LANES)
    rmax = lambda x: jnp.max(x, axis=-1, keepdims=True)
    rmin = lambda x: jnp.min(x, axis=-1, keepdims=True)
    rsum = lambda x: jnp.sum(x, axis=-1, keepdims=True)
    neg = -jnp.inf
    gmask = lane < N_EGROUPS
    gl = jnp.where(gmask, lg, neg)
    gmax = rmax(gl)
    gsum = rsum(jnp.where(gmask, jnp.exp(lg - gmax), 0.0))
    gsel = rmin(jnp.where(gl == gmax, lane, big))
    pg_sel = 1.0 / gsum
    e0 = N_EGROUPS + E_PER_GROUP * gsel
    emask = jnp.logical_and(lane >= e0, lane < e0 + E_PER_GROUP)
    el = jnp.where(emask, lg, neg)
    emax = rmax(el)
    ee = jnp.where(emask, jnp.exp(lg - emax), 0.0)
    pe = ee / rsum(ee)
    pe_m = jnp.where(emask, pe, -1.0)
    v1 = rmax(pe_m)
    i1 = rmin(jnp.where(pe_m == v1, lane, big))
    pe_m2 = jnp.where(lane == i1, -1.0, pe_m)
    v2 = rmax(pe_m2)
    i2 = rmin(jnp.where(pe_m2 == v2, lane, big))
    w1 = pg_sel * v1 / (v1 + v2)
    w2 = pg_sel * v2 / (v1 + v2)
    a1, a2 = i1 - e0, i2 - e0
    lo, hi = jnp.minimum(a1, a2), jnp.maximum(a1, a2)
    pair = (lo * (2 * E_PER_GROUP - 1 - lo)) // 2 + (hi - lo - 1)
    bucket = (gsel * N_PAIRS + pair).astype(f32)
    w_lo = jnp.where(a1 < a2, w1, w2)
    w_hi = jnp.where(a1 < a2, w2, w1)
    return jnp.where(lane == RT_BUCKET, bucket,
                     jnp.where(lane == RT_WLO, w_lo, jnp.where(lane == RT_WHI, w_hi, 0.0)))


def _out_proj_kernel(mod_ref, of_ref, ob_ref, gz_ref, yf_ref, yb_ref, sz_ref, om_ref, x_ref, mods_ref,
                     gn_ref, sn_ref, n2_ref, wout_ref, wr_ref, br_ref, x1_ref, h2_ref, rt_ref):
    del mod_ref
    acc = None
    for h in range(GDN_HEADS):
        sl = slice(h * GDN_DV, (h + 1) * GDN_DV)
        o = _rms(of_ref[:, sl] + ob_ref[:, sl]) * gn_ref[...] * _silu(gz_ref[:, sl])
        t = jnp.dot(o.astype(bf16), wout_ref[sl, :], preferred_element_type=f32)
        acc = t if acc is None else acc + t
    y = (yf_ref[...] + yb_ref[...]) * _silu(sz_ref[...])
    ys = _rms(y) * sn_ref[...]
    acc = acc + jnp.dot(ys.astype(bf16), wout_ref[GDN_W:GDN_W + SSM_INNER, :], preferred_element_type=f32)
    acc = acc + jnp.dot(om_ref[...].astype(bf16), wout_ref[GDN_W + SSM_INNER:MIX_W, :],
                        preferred_element_type=f32)
    x1 = x_ref[...] + mods_ref[MOD_G1:MOD_G1 + 1, :] * acc
    x1_ref[...] = x1
    h2 = _rms(x1) * n2_ref[...]
    h2 = h2 * (1.0 + mods_ref[MOD_SC2:MOD_SC2 + 1, :]) + mods_ref[MOD_SH2:MOD_SH2 + 1, :]
    h2_ref[...] = h2
    rt_ref[...] = _route(_mm3(h2, wr_ref[...]) + br_ref[...])


def out_proj(lay, o_f, o_b, proj, y_f, y_b, o_mla, x, mods, gdn_norm, ssm_norm, norm2, w_out_b, w_rt, b_rt, layer):
    row_map = lambda i, mod: (i, 0)
    lyr = lambda i, mod: (layer, 0, 0)
    return pl.pallas_call(
        _out_proj_kernel,
        grid_spec=pltpu.PrefetchScalarGridSpec(
            num_scalar_prefetch=1, grid=(lay.n_blk,),
            in_specs=[
                pl.BlockSpec((TB, GDN_W), row_map),
                pl.BlockSpec((TB, GDN_W), row_map),
                pl.BlockSpec((TB, GDN_W), lambda i, mod: (i, COL_GZ // GDN_W)),
                pl.BlockSpec((TB, SSM_INNER), row_map),
                pl.BlockSpec((TB, SSM_INNER), row_map),
                pl.BlockSpec((TB, SSM_INNER), lambda i, mod: (i, COL_SZ // SSM_INNER)),
                pl.BlockSpec((TB, MLA_W), row_map),
                pl.BlockSpec((TB, D_MODEL), row_map),
                pl.BlockSpec((None, 6, D_MODEL), lambda i, mod: (mod[i], 0, 0)),
                pl.BlockSpec((None, 1, GDN_DV), lyr),
                pl.BlockSpec((None, 1, SSM_INNER), lyr),
                pl.BlockSpec((None, 1, D_MODEL), lyr),
                pl.BlockSpec((MIX_W, D_MODEL), lambda i, mod: (0, 0), pipeline_mode=pl.Buffered(1)),
                pl.BlockSpec((None, D_MODEL, LANES), lyr),
                pl.BlockSpec((None, 1, LANES), lyr),
            ],
            out_specs=[
                pl.BlockSpec((TB, D_MODEL), row_map),
                pl.BlockSpec((TB, D_MODEL), row_map),
                pl.BlockSpec((TB, LANES), row_map),
            ]),
        out_shape=[
            jax.ShapeDtypeStruct((lay.m, D_MODEL), f32),
            jax.ShapeDtypeStruct((lay.m, D_MODEL), f32),
            jax.ShapeDtypeStruct((lay.m, LANES), f32),
        ],
        compiler_params=pltpu.CompilerParams(dimension_semantics=("arbitrary",), vmem_limit_bytes=VMEM_LIMIT),
        name="out_proj",
    )(jnp.asarray(lay.mod), o_f, o_b, proj, y_f, y_b, proj, o_mla, x, mods, gdn_norm, ssm_norm, norm2,
      w_out_b, w_rt, b_rt)


TME = 256
PAIR_LO = (0, 0, 0, 1, 1, 2)
PAIR_HI = (1, 2, 3, 2, 3, 3)


def _moe_kernel(te_ref, nv_ref, nu_ref, src_ref,
                h2_hbm, ws_ref, wg_ref, wu_ref, wd_ref, y_hbm, xbuf, ybuf, gsem, ssem):
    t = pl.program_id(0)
    j = pl.program_id(1)
    used = t < nu_ref[0]
    base = t * TME
    nvalid = nv_ref[t]

    def row_in(r):
        return pltpu.make_async_copy(h2_hbm.at[pl.ds(src_ref[base + r], 1), :], xbuf.at[pl.ds(r, 1), :], gsem)

    def row_out(r):
        return pltpu.make_async_copy(ybuf.at[pl.ds(r, 1), :], y_hbm.at[pl.ds(src_ref[base + r], 1), :], ssem)

    @pl.when(jnp.logical_and(used, j == 0))
    def _():
        def issue(r, c):
            row_in(r).start()
            return c
        lax.fori_loop(0, TME, issue, 0)

        def drain(r, c):
            row_in(r).wait()
            return c
        lax.fori_loop(0, TME, drain, 0)

    @pl.when(used)
    def _():
        xb = xbuf[...].astype(bf16)
        hg = jnp.dot(xb, wg_ref[...], preferred_element_type=f32)
        hu = jnp.dot(xb, wu_ref[...], preferred_element_type=f32)
        hdn = (_silu(hg) * hu).astype(bf16)
        yo = jnp.dot(hdn, wd_ref[...], preferred_element_type=f32)
        lane = lax.broadcasted_iota(jnp.int32, (TME, LANES), 1)
        wsel = jnp.sum(jnp.where(lane == j, ws_ref[...], 0.0), axis=-1, keepdims=True)

        @pl.when(j == 0)
        def _():
            ybuf[...] = wsel * yo

        @pl.when(j == 1)
        def _():
            ybuf[...] = ybuf[...] + wsel * yo

    @pl.when(jnp.logical_and(used, j == 1))
    def _():
        def issue(r, c):
            @pl.when(r < nvalid)
            def _():
                row_out(r).start()
            return c
        lax.fori_loop(0, TME, issue, 0)

        def drain(r, c):
            @pl.when(r < nvalid)
            def _():
                row_out(r).wait()
            return c
        lax.fori_loop(0, TME, drain, 0)


def moe_dispatch(lay, route):
    m = lay.m
    n_tiles = m // TME + N_BUCKETS
    bucket = route[:, RT_BUCKET].astype(jnp.int32)
    order = jnp.argsort(bucket, stable=True).astype(jnp.int32)
    counts = jnp.zeros((N_BUCKETS,), jnp.int32).at[bucket].add(1)
    ptiles = (counts + TME - 1) // TME
    pend = jnp.cumsum(ptiles)
    pstart = pend - ptiles
    start = jnp.cumsum(counts) - counts
    sb = bucket[order]
    pos = pstart[sb] * TME + (jnp.arange(m, dtype=jnp.int32) - start[sb])
    src = jnp.zeros((n_tiles * TME,), jnp.int32).at[pos].set(order)
    wlo = route[:, RT_WLO][order]
    whi = route[:, RT_WHI][order]
    n_used = pend[-1]
    tid = jnp.arange(n_tiles, dtype=jnp.int32)
    tb = jnp.minimum(jnp.searchsorted(pend, tid, side="right").astype(jnp.int32), N_BUCKETS - 1)
    tb = jnp.where(tid < n_used, tb, tb[jnp.maximum(n_used - 1, 0)])
    nvalid = jnp.clip(counts[tb] - (tid - pstart[tb]) * TME, 0, TME)
    nvalid = jnp.where(tid < n_used, nvalid, 0).astype(jnp.int32)
    grp, pr = tb // N_PAIRS, tb % N_PAIRS
    e_lo = grp * E_PER_GROUP + jnp.asarray(PAIR_LO, jnp.int32)[pr]
    e_hi = grp * E_PER_GROUP + jnp.asarray(PAIR_HI, jnp.int32)[pr]
    swap = (tid % 2) == 1
    te = jnp.stack([jnp.where(swap, e_hi, e_lo), jnp.where(swap, e_lo, e_hi)], -1).reshape(-1)
    swap_row = swap[pos // TME]
    w0 = jnp.where(swap_row, whi, wlo)
    w1 = jnp.where(swap_row, wlo, whi)
    ws = jnp.zeros((n_tiles * TME, LANES), f32).at[pos, 0].set(w0).at[pos, 1].set(w1)
    te = jnp.concatenate([te, te[-2:]])
    return te.astype(jnp.int32), nvalid, n_used.reshape(1).astype(jnp.int32), src, ws, n_tiles


def moe_apply(lay, h2, route, wg_b, wu_b, wd_b):
    te, nvalid, n_used, src, ws, n_tiles = moe_dispatch(lay, route)
    wmap = lambda t, j, te, *_: (te[2 * t + j], 0, 0)
    return pl.pallas_call(
        _moe_kernel,
        grid_spec=pltpu.PrefetchScalarGridSpec(
            num_scalar_prefetch=4, grid=(n_tiles, 2),
            in_specs=[
                pl.BlockSpec(memory_space=pl.ANY),
                pl.BlockSpec((TME, LANES), lambda t, j, *_: (t, 0)),
                pl.BlockSpec((None, D_MODEL, EXPERT_FF), wmap),
                pl.BlockSpec((None, D_MODEL, EXPERT_FF), wmap),
                pl.BlockSpec((None, EXPERT_FF, D_MODEL), wmap),
            ],
            out_specs=pl.BlockSpec(memory_space=pl.ANY),
            scratch_shapes=[
                pltpu.VMEM((TME, D_MODEL), f32),
                pltpu.VMEM((TME, D_MODEL), f32),
                pltpu.SemaphoreType.DMA,
                pltpu.SemaphoreType.DMA,
            ]),
        out_shape=jax.ShapeDtypeStruct((lay.m, D_MODEL), f32),
        compiler_params=pltpu.CompilerParams(dimension_semantics=("arbitrary", "arbitrary"),
                                             vmem_limit_bytes=VMEM_LIMIT),
        name="moe",
    )(te, nvalid, n_used, src, h2, ws, wg_b, wu_b, wd_b)


def _final_kernel(mod_ref, x_ref, y_ref, mods_ref, g_ref, o_ref):
    del mod_ref
    x = x_ref[...] + mods_ref[MOD_G2:MOD_G2 + 1, :] * y_ref[...]
    o_ref[...] = _rms(x) * g_ref[...]


def final_norm_apply(lay, x, ytok, mods, g):
    row_map = lambda i, mod: (i, 0)
    return pl.pallas_call(
        _final_kernel,
        grid_spec=pltpu.PrefetchScalarGridSpec(
            num_scalar_prefetch=1, grid=(lay.n_blk,),
            in_specs=[
                pl.BlockSpec((TB, D_MODEL), row_map),
                pl.BlockSpec((TB, D_MODEL), row_map),
                pl.BlockSpec((None, 6, D_MODEL), lambda i, mod: (mod[i], 0, 0)),
                pl.BlockSpec((1, D_MODEL), lambda i, mod: (0, 0)),
            ],
            out_specs=pl.BlockSpec((TB, D_MODEL), row_map)),
        out_shape=jax.ShapeDtypeStruct((lay.m, D_MODEL), f32),
        compiler_params=pltpu.CompilerParams(dimension_semantics=("arbitrary",)),
        name="final_norm",
    )(jnp.asarray(lay.mod), x, ytok, mods, g)


def _reorder_w_in(w):
    g_qkv, g_z, g_a, g_b, s_z, s_xbc, s_dt, m_q, m_kv = jnp.split(w, np.cumsum(IN_SIZES)[:-1], axis=1)
    ckv, kpe = m_kv[:, :KV_RANK], m_kv[:, KV_RANK:]
    z = lambda n: jnp.zeros((w.shape[0], n), w.dtype)
    small = jnp.concatenate([g_a, g_b, s_dt, z(SM_KPE - SM_DT - 2 * SSM_HEADS), kpe, z(LANES - SM_KPE - MLA_ROPE)], 1)
    return jnp.concatenate([g_qkv, g_z, s_z, m_q, ckv, s_xbc, small], 1).astype(bf16)


def _reorder_w_uq(w):
    r = w.reshape(Q_RANK, MLA_HEADS, MLA_QK)
    return jnp.pad(r, ((0, 0), (0, 0), (0, LANES - MLA_QK))).reshape(Q_RANK, Q_HEADS_W).astype(bf16)


def _reorder_w_ukv(w):
    r = w.reshape(KV_RANK, MLA_HEADS, MLA_NOPE + MLA_VDIM)
    k = jnp.pad(r[:, :, :MLA_NOPE], ((0, 0), (0, 0), (0, LANES - MLA_NOPE))).reshape(KV_RANK, MLA_HEADS * LANES)
    v = r[:, :, MLA_NOPE:].reshape(KV_RANK, MLA_HEADS * MLA_VDIM)
    return jnp.concatenate([k, v], 1).astype(bf16)


def _lane_row(vals, lane0):
    n = vals.shape[-1]
    return jnp.pad(vals, ((0, 0), (lane0, LANES - lane0 - n)))


def _par_rows(row0, row1):
    z = jnp.zeros_like(row0)
    return jnp.stack([row0, row1] + [z] * (SUBLANES - 2), axis=1)


def kernel(x_prompt, x_sample, cache_ckv, cache_kpe, state_gdn, state_ssm, c, c_ctx, w_ada, b_ada, norm1, norm2, w_in, gdn_conv, gdn_A_log, gdn_dt_bias, gdn_norm, ssm_conv, ssm_conv_bias, ssm_A_log, ssm_dt_bias, ssm_D, ssm_norm, mla_q_norm, mla_w_uq, mla_kv_norm, mla_w_ukv, w_out, router_group, router_group_bias, router_expert, router_expert_bias, moe_w_gate, moe_w_up, moe_w_down, final_norm):
    n_ctx, t_ctx, _ = x_prompt.shape
    n_lat, t_lat, _ = x_sample.shape
    past = cache_ckv.shape[2]
    depth = w_in.shape[0]
    assert n_lat < SUBLANES and t_ctx % TB == 0 and t_lat % TB == 0 and past % TB == 0
    lay = Layout(n_ctx, t_ctx, n_lat, t_lat)
    m_ctx = lay.m_ctx

    x = jnp.concatenate([x_prompt.reshape(m_ctx, D_MODEL), x_sample.reshape(lay.m_lat, D_MODEL)], 0)
    cvec = jnp.concatenate([c_ctx[None], c, jnp.zeros((SUBLANES - 1 - n_lat, D_MODEL), f32)], 0)
    tables = rope_tables(lay)

    b_ada3 = b_ada[:, None, :]
    norm1_3, norm2_3 = norm1[:, None, :], norm2[:, None, :]
    gdn_par = _par_rows(_lane_row(gdn_A_log.reshape(depth, -1), SM_GA), _lane_row(gdn_dt_bias.reshape(depth, -1), SM_GA))
    ssm_par = _par_rows(_lane_row(ssm_A_log.reshape(depth, -1), SM_DT), _lane_row(ssm_dt_bias.reshape(depth, -1), SM_DT))
    d_lanes = jnp.pad(jnp.repeat(ssm_D, SSM_HEADDIM, axis=-1), ((0, 0), (0, SSM_XBC - SSM_INNER)))
    ssm_cpar = _par_rows(ssm_conv_bias, d_lanes)
    w_rt = jnp.pad(jnp.concatenate([router_group, router_expert], -1),
                   ((0, 0), (0, 0), (0, LANES - N_EGROUPS - N_EXPERTS)))
    b_rt = jnp.pad(jnp.concatenate([router_group_bias, router_expert_bias], -1),
                   ((0, 0), (0, LANES - N_EGROUPS - N_EXPERTS)))[:, None, :]
    kpe_cache = jnp.pad(cache_kpe, ((0, 0), (0, 0), (0, 0), (SM_KPE, LANES - SM_KPE - MLA_ROPE)))

    ckvs, kpes, gdns, ssms = [], [], [], []
    ytok = mods_prev = None
    for l in range(depth):
        mods = ada_mod(cvec, w_ada, b_ada3, l)
        res = None if l == 0 else (ytok, mods_prev)
        proj, x = in_proj(lay, x, mods, norm1_3, _reorder_w_in(w_in[l]), l, res)

        zg = jnp.zeros((n_ctx, GDN_HEADS, GDN_DK, GDN_DV), f32)
        o_f, sg_f = gdn_scan(lay, proj, gdn_conv, gdn_par, jnp.concatenate([zg, state_gdn[:, l, 0]], 0), l, False)
        o_b, sg_b = gdn_scan(lay, proj, gdn_conv, gdn_par, jnp.concatenate([zg, state_gdn[:, l, 1]], 0), l, True)

        zs = jnp.zeros((n_ctx, SSM_INNER, LANES), f32)
        y_f, hs_f = ssd_scan(lay, proj, ssm_conv, ssm_cpar, ssm_par,
                             jnp.concatenate([zs, ssm_state_to_lanes(state_ssm[:, l, 0])], 0), l, False)
        y_b, hs_b = ssd_scan(lay, proj, ssm_conv, ssm_cpar, ssm_par,
                             jnp.concatenate([zs, ssm_state_to_lanes(state_ssm[:, l, 1])], 0), l, True)

        q, ckvn, kpe = mla_prep(lay, proj, mla_q_norm[:, None, :], mla_kv_norm[:, None, :],
                                _reorder_w_uq(mla_w_uq[l]), tables, l)
        w_ukv_r = _reorder_w_ukv(mla_w_ukv[l])
        k_ctx, v_ctx = kv_up(ckvn[:m_ctx], kpe[:m_ctx], w_ukv_r)
        ckv_lat = jnp.concatenate([cache_ckv[:, l], ckvn[m_ctx:].reshape(n_lat, t_lat, KV_RANK)], 1)
        kpe_lat = jnp.concatenate([kpe_cache[:, l], kpe[m_ctx:].reshape(n_lat, t_lat, LANES)], 1)
        k_lat, v_lat = kv_up(ckv_lat.reshape(-1, KV_RANK), kpe_lat.reshape(-1, LANES), w_ukv_r)
        o_mla = jnp.concatenate([
            attention(q, k_ctx, v_ctx, 0, n_ctx, t_ctx, t_ctx),
            attention(q, k_lat, v_lat, m_ctx // TB, n_lat, t_lat, past + t_lat)], 0)

        x, h2, route = out_proj(lay, o_f, o_b, proj, y_f, y_b, o_mla, x, mods, gdn_norm[:, None, :],
                                ssm_norm[:, None, :], norm2_3, w_out[l].astype(bf16), w_rt, b_rt, l)
        ytok = moe_apply(lay, h2, route, moe_w_gate[l].astype(bf16), moe_w_up[l].astype(bf16),
                         moe_w_down[l].astype(bf16))
        mods_prev = mods

        ckvs.append(ckvn[:m_ctx].reshape(n_ctx, t_ctx, KV_RANK))
        kpes.append(kpe[:m_ctx, SM_KPE:SM_KPE + MLA_ROPE].reshape(n_ctx, t_ctx, MLA_ROPE))
        gdns.append(jnp.stack([sg_f[:n_ctx], sg_b[:n_ctx]], 1))
        ssms.append(jnp.stack([ssm_state_from_lanes(hs_f[:n_ctx]), ssm_state_from_lanes(hs_b[:n_ctx])], 1))

    y = final_norm_apply(lay, x, ytok, mods_prev, final_norm[None, :])
    return (y[:m_ctx].reshape(n_ctx, t_ctx, D_MODEL), y[m_ctx:].reshape(n_lat, t_lat, D_MODEL),
            jnp.stack(ckvs, 1), jnp.stack(kpes, 1), jnp.stack(gdns, 1), jnp.stack(ssms, 1))
```

```python
import functools
import math

import numpy as np
import jax
import jax.numpy as jnp
from jax import lax
from jax.experimental import pallas as pl
from jax.experimental.pallas import tpu as pltpu

f32 = jnp.float32
bf16 = jnp.bfloat16

D_MODEL = 2048
DEPTH = 2
GRID_W = 64
EPS = 1e-6
CONV_W = 5
CHUNK = 64

GDN_HEADS = 4
GDN_DK = 128
GDN_DV = 128
GDN_W = GDN_HEADS * GDN_DV
GDN_QKV = 2 * GDN_HEADS * GDN_DK + GDN_HEADS * GDN_DV

SSM_HEADS = 16
SSM_HEADDIM = 64
SSM_GROUPS = 2
SSM_DSTATE = 64
SSM_INNER = SSM_HEADS * SSM_HEADDIM
SSM_XBC = SSM_INNER + 2 * SSM_GROUPS * SSM_DSTATE

MLA_HEADS = 8
MLA_NOPE = 64
MLA_ROPE = 32
MLA_VDIM = 64
Q_RANK = 512
KV_RANK = 256
MLA_W = MLA_HEADS * MLA_VDIM
ROPE_THETA = 10000.0

MIX_W = GDN_W + SSM_INNER + MLA_W
IN_SIZES = (GDN_QKV, GDN_W, 2 * GDN_HEADS, 2 * GDN_HEADS, SSM_INNER, SSM_XBC, 2 * SSM_HEADS, Q_RANK, KV_RANK + MLA_ROPE)

N_EGROUPS = 4
E_PER_GROUP = 4
N_EXPERTS = N_EGROUPS * E_PER_GROUP
EXPERT_FF = 512

LANES = 128
SUBLANES = 8

TB = 256

COL_QKV = 0
COL_GZ = 1536
COL_SZ = 2048
COL_MQ = 3072
COL_CKV = 3584
COL_XBC = 3840
COL_SMALL = 5120
PROJ_W = 5248
SM_GA, SM_GB, SM_DT, SM_KPE = 0, 8, 16, 64


class Layout:
    def __init__(self, n_ctx, t_ctx, n_lat, t_lat):
        self.n_ctx, self.t_ctx, self.n_lat, self.t_lat = n_ctx, t_ctx, n_lat, t_lat
        self.m_ctx = n_ctx * t_ctx
        self.m_lat = n_lat * t_lat
        self.m = self.m_ctx + self.m_lat
        self.n_seq = n_ctx + n_lat
        bc, bl = t_ctx // TB, t_lat // TB
        seq, first, last, mod = [], [], [], []
        for s in range(n_ctx):
            for j in range(bc):
                seq.append(s); first.append(int(j == 0)); last.append(int(j == bc - 1)); mod.append(0)
        for s in range(n_lat):
            for j in range(bl):
                seq.append(n_ctx + s); first.append(int(j == 0)); last.append(int(j == bl - 1)); mod.append(1 + s)
        self.n_blk = len(seq)
        self.seq = np.array(seq, np.int32)
        self.first = np.array(first, np.int32)
        self.last = np.array(last, np.int32)
        self.mod = np.array(mod + mod[-1:], np.int32)
        fwd = np.arange(self.n_blk, dtype=np.int32)
        bwd = []
        i = 0
        while i < self.n_blk:
            j = i
            while self.last[j] == 0:
                j += 1
            bwd.extend(range(j, i - 1, -1))
            i = j + 1
        self.order = {False: fwd, True: np.array(bwd, np.int32)}

    def scan_tables(self, rev):
        order = self.order[rev]
        pad = lambda a: jnp.asarray(np.concatenate([a, a[-1:]]))
        return pad(order), pad(self.seq[order]), pad(self.first[order]), pad(self.last[order])


def _mm(a, b):
    return jnp.dot(a.astype(bf16), b.astype(bf16), preferred_element_type=f32)


def _mm_nt(a, b):
    return lax.dot_general(a.astype(bf16), b.astype(bf16), (((1,), (1,)), ((), ())), preferred_element_type=f32)


def _mm_tn(a, b):
    n = a.shape[1]
    eye = jnp.where(lax.broadcasted_iota(jnp.int32, (n, n), 0) == lax.broadcasted_iota(jnp.int32, (n, n), 1),
                    1.0, 0.0).astype(bf16)
    at = lax.dot_general(eye, a.astype(bf16), (((1,), (1,)), ((), ())), preferred_element_type=f32)
    return jnp.dot(at.astype(bf16), b.astype(bf16), preferred_element_type=f32)


def _split3(x):
    hi = x.astype(bf16)
    r = x - hi.astype(f32)
    mid = r.astype(bf16)
    lo = (r - mid.astype(f32)).astype(bf16)
    return hi, mid, lo


def _mm01(m01, x):
    hi, mid, lo = _split3(x)
    d = lambda t: jnp.dot(m01, t, preferred_element_type=f32)
    return d(hi) + d(mid) + d(lo)


def _mm3(a, b):
    ah = a.astype(bf16)
    al = (a - ah.astype(f32)).astype(bf16)
    bh = b.astype(bf16)
    bl = (b - bh.astype(f32)).astype(bf16)
    d = lambda x, y: jnp.dot(x, y, preferred_element_type=f32)
    return d(ah, bh) + d(ah, bl) + d(al, bh)


def _sigmoid(x):
    return 1.0 / (1.0 + jnp.exp(-x))


def _silu(x):
    return x * _sigmoid(x)


def _softplus(x):
    return jnp.maximum(x, 0.0) + jnp.log(1.0 + jnp.exp(-jnp.abs(x)))


def _chunk_masks(rev):
    i = lax.broadcasted_iota(jnp.int32, (CHUNK, CHUNK), 0)
    j = lax.broadcasted_iota(jnp.int32, (CHUNK, CHUNK), 1)
    if rev:
        incl, strict = j >= i, j > i
    else:
        incl, strict = j <= i, j < i
    ll = jnp.where(incl, 1.0, 0.0).astype(bf16)
    lls = jnp.where(incl, 0.0, 1.0).astype(bf16)
    uu = jnp.where(strict, 1.0, 0.0)
    return incl, strict, ll, lls, uu


def _fill_ext(ext_ref, main_ref, prev_ref, next_ref, sfirst, slast):
    ext_ref[0:SUBLANES, :] = jnp.where(sfirst == 1, 0.0, prev_ref[...])
    ext_ref[SUBLANES:SUBLANES + TB, :] = main_ref[...]
    ext_ref[SUBLANES + TB:2 * SUBLANES + TB, :] = jnp.where(slast == 1, 0.0, next_ref[...])


def _conv_tile(ext_ref, w_ref, r0, col0, width, bias=None):
    acc = None
    for t in range(CONV_W):
        rows = ext_ref[pl.ds(SUBLANES + r0 - CONV_W // 2 + t, CHUNK), col0:col0 + width]
        term = rows * w_ref[t:t + 1, col0:col0 + width]
        acc = term if acc is None else acc + term
    if bias is not None:
        acc = acc + bias
    return _silu(acc)


INV_BLOCK = 16


def _block_masks():
    i = lax.broadcasted_iota(jnp.int32, (CHUNK, CHUNK), 0)
    j = lax.broadcasted_iota(jnp.int32, (CHUNK, CHUNK), 1)
    blk16 = (i // INV_BLOCK) == (j // INV_BLOCK)
    blk32 = (i // (2 * INV_BLOCK)) == (j // (2 * INV_BLOCK))
    return blk16, blk32


def _unit_tri_inverse_many(lmats, eye, blk16, blk32):
    ps = list(lmats)
    ld = {p: jnp.where(blk16, lmats[p], 0.0) for p in ps}
    t = {p: eye - ld[p] for p in ps}
    pw = {p: _mm3(ld[p], ld[p]) for p in ps}
    for it in range(3):
        t = {p: t[p] + _mm3(t[p], pw[p]) for p in ps}
        if it < 2:
            pw = {p: _mm3(pw[p], pw[p]) for p in ps}
    in32 = jnp.logical_and(blk32, jnp.logical_not(blk16))
    m = {p: _mm3(t[p], jnp.where(in32, lmats[p], 0.0)) for p in ps}
    t = {p: t[p] - _mm3(m[p], t[p]) for p in ps}
    m = {p: _mm3(t[p], jnp.where(blk32, 0.0, lmats[p])) for p in ps}
    return {p: t[p] - _mm3(m[p], t[p]) for p in ps}


def _gdn_kernel(order_ref, seq_ref, first_ref, last_ref,
                main_ref, prev_ref, next_ref, small_ref, convw_ref, par_ref, s0_ref,
                o_ref, sfin_ref, s_ref, ext_ref, *, rev):
    del order_ref, seq_ref
    step = pl.program_id(0)
    sfirst = first_ref[step]
    slast = last_ref[step]
    pfirst, plast = (slast, sfirst) if rev else (sfirst, slast)
    d = 1 if rev else 0

    @pl.when(pfirst == 1)
    def _():
        s_ref[...] = s0_ref[...]

    _fill_ext(ext_ref, main_ref, prev_ref, next_ref, sfirst, slast)

    incl, strict, ll, lls, uu = _chunk_masks(rev)
    eye = jnp.where(lax.broadcasted_iota(jnp.int32, (CHUNK, CHUNK), 0)
                    == lax.broadcasted_iota(jnp.int32, (CHUNK, CHUNK), 1), 1.0, 0.0)
    blk16, blk32 = _block_masks()
    neg_a = -jnp.exp(par_ref[0:1, :])
    dt_bias = par_ref[1:2, :]
    n_chunks = TB // CHUNK

    chunk_order = [n_chunks - 1 - cc if rev else cc for cc in range(n_chunks)]
    heads = range(GDN_HEADS)
    eye_k = jnp.where(lax.broadcasted_iota(jnp.int32, (GDN_DK, GDN_DK), 0)
                      == lax.broadcasted_iota(jnp.int32, (GDN_DK, GDN_DK), 1), 1.0, 0.0).astype(bf16)

    gates = {}
    for c in chunk_order:
        sm = small_ref[c * CHUNK:(c + 1) * CHUNK, :]
        la_blk = neg_a * _softplus(sm + dt_bias)
        gates[c] = (la_blk, _sigmoid(sm))
    g_blk = {c: _mm01(ll, gates[c][0]) for c in chunk_order}
    gr_blk = {c: _mm01(lls, gates[c][0]) for c in chunk_order}
    gl_blk = {c: jnp.sum(gates[c][0], axis=0, keepdims=True) for c in chunk_order}

    probs = [(c, h) for c in chunk_order for h in heads]
    ia = lambda h: SM_GA + d * GDN_HEADS + h
    ib = lambda h: SM_GB + d * GDN_HEADS + h
    col = lambda blk, lane: blk[:, lane:lane + 1]
    beta = {p: col(gates[p[0]][1], ib(p[1])) for p in probs}
    q, k, v = {}, {}, {}
    for p in probs:
        c, h = p
        r0 = c * CHUNK
        qq = _conv_tile(ext_ref, convw_ref, r0, h * GDN_DK, GDN_DK)
        kx = _conv_tile(ext_ref, convw_ref, r0, GDN_HEADS * GDN_DK + h * GDN_DK, GDN_DK)
        v[p] = _conv_tile(ext_ref, convw_ref, r0, 2 * GDN_HEADS * GDN_DK + h * GDN_DV, GDN_DV)
        q[p] = qq * lax.rsqrt(jnp.sum(qq * qq, axis=-1, keepdims=True) + EPS) * (GDN_DK ** -0.5)
        k[p] = kx * lax.rsqrt(jnp.sum(kx * kx, axis=-1, keepdims=True) + EPS)
    dmat = {p: _mm01(ll, col(gates[p[0]][0], ia(p[1])) * uu) for p in probs}
    kk = {p: _mm_nt(k[p], k[p]) for p in probs}
    qk = {p: _mm_nt(q[p], k[p]) for p in probs}
    decay = {p: jnp.where(incl, jnp.exp(dmat[p]), 0.0) for p in probs}
    lmat = {p: jnp.where(strict, kk[p] * beta[p] * decay[p], 0.0) for p in probs}
    qk = {p: jnp.where(incl, qk[p] * decay[p], 0.0) for p in probs}
    tinv = _unit_tri_inverse_many(lmat, eye, blk16, blk32)
    uw = {}
    for p in probs:
        e_g = jnp.exp(col(g_blk[p[0]], ia(p[1])))
        rhs = jnp.concatenate([v[p] * beta[p], k[p] * beta[p] * e_g], axis=1)
        uw[p] = _mm3(tinv[p], rhs)
        q[p] = q[p] * e_g
    kdt = {}
    for p in probs:
        k_dec = (k[p] * jnp.exp(col(gr_blk[p[0]], ia(p[1])))).astype(bf16)
        kdt[p] = lax.dot_general(eye_k, k_dec, (((1,), (1,)), ((), ())), preferred_element_type=f32)

    for c in chunk_order:
        ps = [(c, h) for h in heads]
        s_old = {p: s_ref[p[1]] for p in ps}
        ws = {p: _mm(uw[p][:, GDN_DV:], s_old[p]) for p in ps}
        qs = {p: _mm(q[p], s_old[p]) for p in ps}
        v_new = {p: uw[p][:, :GDN_DV] - ws[p] for p in ps}
        o = {p: qs[p] + _mm(qk[p], v_new[p]) for p in ps}
        for p in ps:
            h = p[1]
            s_ref[h] = s_old[p] * jnp.exp(col(gl_blk[c], ia(h))) + _mm(kdt[p], v_new[p])
            o_ref[c * CHUNK:(c + 1) * CHUNK, h * GDN_DV:(h + 1) * GDN_DV] = o[p]

    @pl.when(plast == 1)
    def _():
        sfin_ref[...] = s_ref[...]


def _halo_maps(n_rows8):
    prev_map = lambda i, order, *_: (jnp.maximum(order[i] * (TB // SUBLANES) - 1, 0), 0)
    next_map = lambda i, order, *_: (jnp.minimum((order[i] + 1) * (TB // SUBLANES), n_rows8 - 1), 0)
    return prev_map, next_map


def gdn_scan(lay, proj, convw, par, s0, layer, rev):
    prev_map, next_map = _halo_maps(lay.m // SUBLANES)
    blk_map = lambda i, order, *_: (order[i], 0)
    seq_map = lambda i, order, seq, *_: (seq[i], 0, 0, 0)
    grid_spec = pltpu.PrefetchScalarGridSpec(
        num_scalar_prefetch=4,
        grid=(lay.n_blk,),
        in_specs=[
            pl.BlockSpec((TB, GDN_QKV), blk_map),
            pl.BlockSpec((SUBLANES, GDN_QKV), prev_map),
            pl.BlockSpec((SUBLANES, GDN_QKV), next_map),
            pl.BlockSpec((TB, LANES), lambda i, order, *_: (order[i], COL_SMALL // LANES)),
            pl.BlockSpec((None, CONV_W, GDN_QKV), lambda i, *_: (layer, 0, 0)),
            pl.BlockSpec((None, SUBLANES, LANES), lambda i, *_: (layer, 0, 0)),
            pl.BlockSpec((None, GDN_HEADS, GDN_DK, GDN_DV), seq_map),
        ],
        out_specs=[
            pl.BlockSpec((TB, GDN_W), blk_map),
            pl.BlockSpec((None, GDN_HEADS, GDN_DK, GDN_DV), seq_map),
        ],
        scratch_shapes=[
            pltpu.VMEM((GDN_HEADS, GDN_DK, GDN_DV), f32),
            pltpu.VMEM((TB + 2 * SUBLANES, GDN_QKV), f32),
        ],
    )
    return pl.pallas_call(
        functools.partial(_gdn_kernel, rev=rev),
        grid_spec=grid_spec,
        out_shape=[
            jax.ShapeDtypeStruct((lay.m, GDN_W), f32),
            jax.ShapeDtypeStruct((lay.n_seq, GDN_HEADS, GDN_DK, GDN_DV), f32),
        ],
        compiler_params=pltpu.CompilerParams(dimension_semantics=("arbitrary",)),
        name="gdn_bwd" if rev else "gdn_fwd",
    )(*lay.scan_tables(rev), proj, proj, proj, proj, convw, par, s0)


SSM_PAIRS = SSM_HEADS // 2
PAIRS_PER_GROUP = SSM_PAIRS // SSM_GROUPS
COL_B = SSM_INNER
COL_C = SSM_INNER + SSM_GROUPS * SSM_DSTATE


def _ssd_kernel(order_ref, seq_ref, first_ref, last_ref,
                main_ref, prev_ref, next_ref, small_ref, convw_ref, cpar_ref, par_ref, h0_ref,
                y_ref, hfin_ref, h_ref, ext_ref, *, rev):
    del order_ref, seq_ref
    step = pl.program_id(0)
    sfirst = first_ref[step]
    slast = last_ref[step]
    pfirst, plast = (slast, sfirst) if rev else (sfirst, slast)
    d = 1 if rev else 0

    @pl.when(pfirst == 1)
    def _():
        h_ref[...] = h0_ref[...]

    _fill_ext(ext_ref, main_ref, prev_ref, next_ref, sfirst, slast)

    incl, _, ll, lls, uu = _chunk_masks(rev)
    lane_lo = lax.broadcasted_iota(jnp.int32, (CHUNK, LANES), 1) < SSM_DSTATE
    row_lo = lax.broadcasted_iota(jnp.int32, (LANES, LANES), 0) < SSM_HEADDIM
    neg_a = -jnp.exp(par_ref[0:1, :])
    dt_bias = par_ref[1:2, :]
    n_chunks = TB // CHUNK

    for cc in range(n_chunks):
        c = n_chunks - 1 - cc if rev else cc
        r0 = c * CHUNK
        sm = small_ref[r0:r0 + CHUNK, :]
        dt_blk = _softplus(sm + dt_bias)
        dta_blk = dt_blk * neg_a
        acum_blk = _mm01(ll, dta_blk)
        ar_blk = _mm01(lls, dta_blk)
        al_blk = jnp.sum(dta_blk, axis=0, keepdims=True)
        b_pair = _conv_tile(ext_ref, convw_ref, r0, COL_B, LANES, cpar_ref[0:1, COL_B:COL_B + LANES])
        c_pair = _conv_tile(ext_ref, convw_ref, r0, COL_C, LANES, cpar_ref[0:1, COL_C:COL_C + LANES])
        for g in range(SSM_GROUPS):
            gmask = lane_lo if g == 0 else jnp.logical_not(lane_lo)
            c_g = jnp.where(gmask, c_pair, 0.0)
            b_g = jnp.where(gmask, b_pair, 0.0)
            cb = _mm_nt(c_g, b_pair)
            for pp in range(PAIRS_PER_GROUP):
                p = g * PAIRS_PER_GROUP + pp
                col = p * LANES
                xs = _conv_tile(ext_ref, convw_ref, r0, col, LANES, cpar_ref[0:1, col:col + LANES])
                cols = []
                for hh in (2 * p, 2 * p + 1):
                    il = SM_DT + d * SSM_HEADS + hh
                    dta_c = dta_blk[:, il:il + 1]
                    dmat = _mm01(ll, dta_c * uu)
                    m_h = cb * jnp.where(incl, jnp.exp(dmat), 0.0)
                    cols.append((dt_blk[:, il:il + 1], acum_blk[:, il:il + 1], ar_blk[:, il:il + 1],
                                 al_blk[:, il:il + 1], m_h))
                (dt0, ac0, ar0, al0, m0), (dt1, ac1, ar1, al1, m1) = cols
                xdt = xs * jnp.where(lane_lo, dt0, dt1)
                y = jnp.where(lane_lo, _mm(m0, xdt), _mm(m1, xdt))
                st_new = jnp.where(row_lo, _mm_tn(xdt, b_g * jnp.exp(ar0)), _mm_tn(xdt, b_g * jnp.exp(ar1)))
                h_pair = h_ref[col:col + LANES, :]
                y = y + _mm_nt(c_g, h_pair) * jnp.where(lane_lo, jnp.exp(ac0), jnp.exp(ac1))
                if not rev:
                    y = y + xs * cpar_ref[1:2, col:col + LANES]
                h_ref[col:col + LANES, :] = h_pair * jnp.where(row_lo, jnp.exp(al0), jnp.exp(al1)) + st_new
                y_ref[r0:r0 + CHUNK, col:col + LANES] = y

    @pl.when(plast == 1)
    def _():
        hfin_ref[...] = h_ref[...]


def ssd_scan(lay, proj, convw, cpar, par, h0, layer, rev):
    prev_map, next_map = _halo_maps(lay.m // SUBLANES)
    cblk = COL_XBC // SSM_XBC
    blk_map = lambda i, order, *_: (order[i], 0)
    seq_map = lambda i, order, seq, *_: (seq[i], 0, 0)
    grid_spec = pltpu.PrefetchScalarGridSpec(
        num_scalar_prefetch=4,
        grid=(lay.n_blk,),
        in_specs=[
            pl.BlockSpec((TB, SSM_XBC), lambda i, order, *_: (order[i], cblk)),
            pl.BlockSpec((SUBLANES, SSM_XBC), lambda i, *a: (prev_map(i, *a)[0], cblk)),
            pl.BlockSpec((SUBLANES, SSM_XBC), lambda i, *a: (next_map(i, *a)[0], cblk)),
            pl.BlockSpec((TB, LANES), lambda i, order, *_: (order[i], COL_SMALL // LANES)),
            pl.BlockSpec((None, CONV_W, SSM_XBC), lambda i, *_: (layer, 0, 0)),
            pl.BlockSpec((None, SUBLANES, SSM_XBC), lambda i, *_: (layer, 0, 0)),
            pl.BlockSpec((None, SUBLANES, LANES), lambda i, *_: (layer, 0, 0)),
            pl.BlockSpec((None, SSM_INNER, LANES), seq_map),
        ],
        out_specs=[
            pl.BlockSpec((TB, SSM_INNER), blk_map),
            pl.BlockSpec((None, SSM_INNER, LANES), seq_map),
        ],
        scratch_shapes=[
            pltpu.VMEM((SSM_INNER, LANES), f32),
            pltpu.VMEM((TB + 2 * SUBLANES, SSM_XBC), f32),
        ],
    )
    return pl.pallas_call(
        functools.partial(_ssd_kernel, rev=rev),
        grid_spec=grid_spec,
        out_shape=[
            jax.ShapeDtypeStruct((lay.m, SSM_INNER), f32),
            jax.ShapeDtypeStruct((lay.n_seq, SSM_INNER, LANES), f32),
        ],
        compiler_params=pltpu.CompilerParams(dimension_semantics=("arbitrary",)),
        name="ssd_bwd" if rev else "ssd_fwd",
    )(*lay.scan_tables(rev), proj, proj, proj, proj, convw, cpar, par, h0)


def ssm_state_to_lanes(h):
    n = h.shape[0]
    hg = h.reshape(n, SSM_GROUPS, SSM_HEADS // SSM_GROUPS * SSM_HEADDIM, SSM_DSTATE)
    z = jnp.zeros_like(hg[:, 0])
    return jnp.concatenate([jnp.concatenate([hg[:, 0], z], -1), jnp.concatenate([z, hg[:, 1]], -1)], 1)


def ssm_state_from_lanes(hl):
    n = hl.shape[0]
    r = hl.reshape(n, SSM_GROUPS, SSM_INNER // SSM_GROUPS, SSM_GROUPS, SSM_DSTATE)
    h = jnp.stack([r[:, g, :, g, :] for g in range(SSM_GROUPS)], 1)
    return h.reshape(n, SSM_HEADS, SSM_HEADDIM, SSM_DSTATE)


ADA_TN = 1536
VMEM_LIMIT = 56 * 1024 * 1024


def _ada_kernel(c_ref, w_ref, b_ref, o_ref):
    s = _silu(c_ref[...])
    o_ref[...] = jnp.dot(s.astype(bf16), w_ref[...].astype(bf16), preferred_element_type=f32) + b_ref[...]


def ada_mod(cvec, w_ada, b_ada, layer):
    n = 6 * D_MODEL
    out = pl.pallas_call(
        _ada_kernel,
        grid=(n // ADA_TN,),
        in_specs=[
            pl.BlockSpec((SUBLANES, D_MODEL), lambda j: (0, 0)),
            pl.BlockSpec((None, D_MODEL, ADA_TN), lambda j: (layer, 0, j)),
            pl.BlockSpec((None, 1, ADA_TN), lambda j: (layer, 0, j)),
        ],
        out_specs=pl.BlockSpec((SUBLANES, ADA_TN), lambda j: (0, j)),
        out_shape=jax.ShapeDtypeStruct((SUBLANES, n), f32),
        compiler_params=pltpu.CompilerParams(dimension_semantics=("arbitrary",), vmem_limit_bytes=VMEM_LIMIT),
        name="ada_mod",
    )(cvec, w_ada, b_ada)
    return out.reshape(SUBLANES, 6, D_MODEL)


MOD_SH1, MOD_SC1, MOD_G1, MOD_SH2, MOD_SC2, MOD_G2 = range(6)


def _rms(x):
    return x * lax.rsqrt(jnp.mean(x * x, axis=-1, keepdims=True) + EPS)


PROJ_CHUNK = 512


def _in_proj_kernel(mod_ref, *refs, has_res):
    del mod_ref
    if has_res:
        x_ref, y_ref, pmods_ref, mods_ref, n1_ref, w_ref, proj_ref, xo_ref = refs
        x = x_ref[...] + pmods_ref[MOD_G2:MOD_G2 + 1, :] * y_ref[...]
        xo_ref[...] = x
    else:
        x_ref, mods_ref, n1_ref, w_ref, proj_ref = refs
        x = x_ref[...]
    h = _rms(x) * n1_ref[...]
    h = h * (1.0 + mods_ref[MOD_SC1:MOD_SC1 + 1, :]) + mods_ref[MOD_SH1:MOD_SH1 + 1, :]
    hb = h.astype(bf16)
    for c0 in range(0, PROJ_W, PROJ_CHUNK):
        wd = min(PROJ_CHUNK, PROJ_W - c0)
        proj_ref[:, c0:c0 + wd] = jnp.dot(hb, w_ref[:, c0:c0 + wd], preferred_element_type=f32)


def in_proj(lay, x, mods, norm1, w_in_r, layer, res=None):
    row_map = lambda i, mod: (i, 0)
    mod_map = lambda i, mod: (mod[i], 0, 0)
    in_specs = [pl.BlockSpec((TB, D_MODEL), row_map)]
    args = [x]
    if res is not None:
        in_specs += [pl.BlockSpec((TB, D_MODEL), row_map), pl.BlockSpec((None, 6, D_MODEL), mod_map)]
        args += [res[0], res[1]]
    in_specs += [
        pl.BlockSpec((None, 6, D_MODEL), mod_map),
        pl.BlockSpec((None, 1, D_MODEL), lambda i, mod: (layer, 0, 0)),
        pl.BlockSpec((D_MODEL, PROJ_W), lambda i, mod: (0, 0), pipeline_mode=pl.Buffered(1)),
    ]
    args += [mods, norm1, w_in_r]
    out_specs = [pl.BlockSpec((TB, PROJ_W), row_map)]
    out_shape = [jax.ShapeDtypeStruct((lay.m, PROJ_W), f32)]
    if res is not None:
        out_specs.append(pl.BlockSpec((TB, D_MODEL), row_map))
        out_shape.append(jax.ShapeDtypeStruct((lay.m, D_MODEL), f32))
    outs = pl.pallas_call(
        functools.partial(_in_proj_kernel, has_res=res is not None),
        grid_spec=pltpu.PrefetchScalarGridSpec(
            num_scalar_prefetch=1, grid=(lay.n_blk,), in_specs=in_specs, out_specs=out_specs),
        out_shape=out_shape,
        compiler_params=pltpu.CompilerParams(dimension_semantics=("arbitrary",), vmem_limit_bytes=VMEM_LIMIT),
        name="in_proj",
    )(jnp.asarray(lay.mod), *args)
    return (outs[0], outs[1]) if res is not None else (outs[0], x)


MLA_QK = MLA_NOPE + MLA_ROPE
Q_HEADS_W = MLA_HEADS * LANES
ROPE_HALF = MLA_ROPE // 2


def _rope_lanes(x, c, s1, s2):
    return x * c + pltpu.roll(x, ROPE_HALF, 1) * s1 + pltpu.roll(x, LANES - ROPE_HALF, 1) * s2


def _mla_prep_kernel(pos_ref, mq_ref, ckv_ref, small_ref, qn_ref, kvn_ref, wuq_ref, rc_ref, rs1_ref, rs2_ref,
                     q_ref, ckvn_ref, kpe_ref):
    del pos_ref
    c, s1, s2 = rc_ref[...], rs1_ref[...], rs2_ref[...]
    cq = (_rms(mq_ref[...]) * qn_ref[...]).astype(bf16)
    qf = jnp.dot(cq, wuq_ref[...], preferred_element_type=f32)
    scale = MLA_QK ** -0.5 * math.log2(math.e)
    for h in range(MLA_HEADS):
        xh = qf[:, h * LANES:(h + 1) * LANES]
        q_ref[:, h * LANES:(h + 1) * LANES] = (_rope_lanes(xh, c, s1, s2) * scale).astype(bf16)
    ckvn_ref[...] = _rms(ckv_ref[...]) * kvn_ref[...]
    lane = lax.broadcasted_iota(jnp.int32, (TB, LANES), 1)
    in_rope = jnp.logical_and(lane >= SM_KPE, lane < SM_KPE + MLA_ROPE)
    kpe_ref[...] = jnp.where(in_rope, _rope_lanes(small_ref[...], c, s1, s2), 0.0)


def rope_tables(lay):
    t = lay.t_lat
    rows = t // GRID_W
    row = jnp.repeat(jnp.arange(rows, dtype=f32), GRID_W)
    col = jnp.tile(jnp.arange(GRID_W, dtype=f32), rows)
    nf = MLA_ROPE // 4
    inv = jnp.power(ROPE_THETA, -jnp.arange(nf, dtype=f32) / nf)
    ang = jnp.concatenate([row[:, None] * inv, col[:, None] * inv], axis=-1)
    cos, sin = jnp.cos(ang), jnp.sin(ang)
    z = lambda n: jnp.zeros((t, n), f32)
    c = jnp.concatenate([jnp.ones((t, MLA_NOPE), f32), cos, cos, z(LANES - MLA_QK)], -1)
    s1 = jnp.concatenate([z(MLA_NOPE + ROPE_HALF), sin, z(LANES - MLA_QK)], -1)
    s2 = jnp.concatenate([z(MLA_NOPE), -sin, z(ROPE_HALF + LANES - MLA_QK)], -1)
    ident = jnp.concatenate([jnp.ones((TB, MLA_QK), f32), jnp.zeros((TB, LANES - MLA_QK), f32)], -1)
    zero = jnp.zeros((TB, LANES), f32)
    return (jnp.concatenate([ident, c]), jnp.concatenate([zero, s1]), jnp.concatenate([zero, s2]))


def _pos_blocks(lay):
    pos = [0] * (lay.m_ctx // TB)
    for _ in range(lay.n_lat):
        pos += [1 + j for j in range(lay.t_lat // TB)]
    return np.array(pos + pos[-1:], np.int32)


def mla_prep(lay, proj, q_norm, kv_norm, w_uq_r, tables, layer):
    row_map = lambda i, pos: (i, 0)
    tab_map = lambda i, pos: (pos[i], 0)
    return pl.pallas_call(
        _mla_prep_kernel,
        grid_spec=pltpu.PrefetchScalarGridSpec(
            num_scalar_prefetch=1, grid=(lay.n_blk,),
            in_specs=[
                pl.BlockSpec((TB, Q_RANK), lambda i, pos: (i, COL_MQ // Q_RANK)),
                pl.BlockSpec((TB, KV_RANK), lambda i, pos: (i, COL_CKV // KV_RANK)),
                pl.BlockSpec((TB, LANES), lambda i, pos: (i, COL_SMALL // LANES)),
                pl.BlockSpec((None, 1, Q_RANK), lambda i, pos: (layer, 0, 0)),
                pl.BlockSpec((None, 1, KV_RANK), lambda i, pos: (layer, 0, 0)),
                pl.BlockSpec((Q_RANK, Q_HEADS_W), lambda i, pos: (0, 0)),
                pl.BlockSpec((TB, LANES), tab_map),
                pl.BlockSpec((TB, LANES), tab_map),
                pl.BlockSpec((TB, LANES), tab_map),
            ],
            out_specs=[
                pl.BlockSpec((TB, Q_HEADS_W), row_map),
                pl.BlockSpec((TB, KV_RANK), row_map),
                pl.BlockSpec((TB, LANES), row_map),
            ]),
        out_shape=[
            jax.ShapeDtypeStruct((lay.m, Q_HEADS_W), bf16),
            jax.ShapeDtypeStruct((lay.m, KV_RANK), f32),
            jax.ShapeDtypeStruct((lay.m, LANES), f32),
        ],
        compiler_params=pltpu.CompilerParams(dimension_semantics=("arbitrary",)),
        name="mla_prep",
    )(jnp.asarray(_pos_blocks(lay)), proj, proj, proj, q_norm, kv_norm, w_uq_r, *tables)


V_PAIRS = MLA_HEADS // 2
KV_UP_W = MLA_HEADS * LANES + MLA_HEADS * MLA_VDIM


def _kv_up_kernel(ckv_ref, kpe_ref, w_ref, k_ref, v_ref):
    kv = jnp.dot(ckv_ref[...].astype(bf16), w_ref[...], preferred_element_type=f32)
    kpe = kpe_ref[...]
    for h in range(MLA_HEADS):
        k_ref[h] = (kv[:, h * LANES:(h + 1) * LANES] + kpe).astype(bf16)
    v0 = MLA_HEADS * LANES
    for p in range(V_PAIRS):
        v_ref[p] = kv[:, v0 + p * LANES:v0 + (p + 1) * LANES].astype(bf16)


def kv_up(ckv, kpe, w_ukv_r):
    r = ckv.shape[0]
    return pl.pallas_call(
        _kv_up_kernel,
        grid=(r // TB,),
        in_specs=[
            pl.BlockSpec((TB, KV_RANK), lambda i: (i, 0)),
            pl.BlockSpec((TB, LANES), lambda i: (i, 0)),
            pl.BlockSpec((KV_RANK, KV_UP_W), lambda i: (0, 0)),
        ],
        out_specs=[
            pl.BlockSpec((MLA_HEADS, TB, LANES), lambda i: (0, i, 0)),
            pl.BlockSpec((V_PAIRS, TB, LANES), lambda i: (0, i, 0)),
        ],
        out_shape=[
            jax.ShapeDtypeStruct((MLA_HEADS, r, LANES), bf16),
            jax.ShapeDtypeStruct((V_PAIRS, r, LANES), bf16),
        ],
        compiler_params=pltpu.CompilerParams(dimension_semantics=("arbitrary",)),
        name="kv_up",
    )(ckv, kpe, w_ukv_r)


ATT_KC = 512


def _attn_kernel(q_ref, k_ref, v_ref, o_ref, m_ref, l_ref, acc_ref, *, n_keys):
    lane_lo = lax.broadcasted_iota(jnp.int32, (TB, LANES), 1) < MLA_VDIM
    kc = min(ATT_KC, n_keys)
    hs = range(MLA_HEADS)
    m_ref[...] = jnp.full((MLA_HEADS, TB, LANES), -jnp.inf, f32)
    l_ref[...] = jnp.zeros((MLA_HEADS, TB, LANES), f32)
    acc_ref[...] = jnp.zeros((MLA_HEADS, TB, LANES), f32)

    def body(c, carry):
        c0 = pl.multiple_of(c * kc, kc)
        s = [lax.dot_general(q_ref[:, h * LANES:(h + 1) * LANES], k_ref[h, pl.ds(c0, kc), :],
                             (((1,), (1,)), ((), ())), preferred_element_type=f32) for h in hs]
        m_old = [m_ref[h] for h in hs]
        m_new = [jnp.maximum(m_old[h], jnp.max(s[h], axis=-1, keepdims=True)) for h in hs]
        alpha = [jnp.exp2(m_old[h] - m_new[h]) for h in hs]
        pe = [jnp.exp2(s[h] - m_new[h][:, 0:1]) for h in hs]
        for h in hs:
            l_ref[h] = alpha[h] * l_ref[h] + jnp.sum(pe[h], axis=-1, keepdims=True)
        pv = [jnp.dot(pe[h].astype(bf16), v_ref[h // 2, pl.ds(c0, kc), :], preferred_element_type=f32)
              for h in hs]
        for h in hs:
            acc_ref[h] = alpha[h] * acc_ref[h] + pv[h]
            m_ref[h] = m_new[h]
        return carry

    lax.fori_loop(0, n_keys // kc, body, 0)
    for p in range(V_PAIRS):
        o_ref[:, p * LANES:(p + 1) * LANES] = jnp.where(
            lane_lo, acc_ref[2 * p] / l_ref[2 * p], acc_ref[2 * p + 1] / l_ref[2 * p + 1])


def attention(q, k, v, q_blk0, n_seq, t_q, n_keys):
    qb = t_q // TB
    return pl.pallas_call(
        functools.partial(_attn_kernel, n_keys=n_keys),
        grid=(n_seq, qb),
        in_specs=[
            pl.BlockSpec((TB, Q_HEADS_W), lambda b, j: (q_blk0 + b * qb + j, 0)),
            pl.BlockSpec((MLA_HEADS, n_keys, LANES), lambda b, j: (0, b, 0)),
            pl.BlockSpec((V_PAIRS, n_keys, LANES), lambda b, j: (0, b, 0)),
        ],
        out_specs=pl.BlockSpec((TB, MLA_W), lambda b, j: (b * qb + j, 0)),
        out_shape=jax.ShapeDtypeStruct((n_seq * t_q, MLA_W), f32),
        scratch_shapes=[pltpu.VMEM((MLA_HEADS, TB, LANES), f32)] * 3,
        compiler_params=pltpu.CompilerParams(dimension_semantics=("arbitrary", "arbitrary"),
                                             vmem_limit_bytes=VMEM_LIMIT),
        name="mla_attn",
    )(q, k, v)


N_PAIRS = E_PER_GROUP * (E_PER_GROUP - 1) // 2
N_BUCKETS = N_EGROUPS * N_PAIRS
RT_BUCKET, RT_WLO, RT_WHI = 0, 1, 2


def _route(lg):
    lane = lax.broadcasted_iota(jnp.int32, lg.shape, 1)
    big = jnp.int32(LANES)
    rmax = lambda x: jnp.max(x, axis=-1, keepdims=True)
    rmin = lambda x: jnp.min(x, axis=-1, keepdims=True)
    rsum = lambda x: jnp.sum(x, axis=-1, keepdims=True)
    neg = -jnp.inf
    gmask = lane < N_EGROUPS
    gl = jnp.where(gmask, lg, neg)
    gmax = rmax(gl)
    gsum = rsum(jnp.where(gmask, jnp.exp(lg - gmax), 0.0))
    gsel = rmin(jnp.where(gl == gmax, lane, big))
    pg_sel = 1.0 / gsum
    e0 = N_EGROUPS + E_PER_GROUP * gsel
    emask = jnp.logical_and(lane >= e0, lane < e0 + E_PER_GROUP)
    el = jnp.where(emask, lg, neg)
    emax = rmax(el)
    ee = jnp.where(emask, jnp.exp(lg - emax), 0.0)
    pe = ee / rsum(ee)
    pe_m = jnp.where(emask, pe, -1.0)
    v1 = rmax(pe_m)
    i1 = rmin(jnp.where(pe_m == v1, lane, big))
    pe_m2 = jnp.where(lane == i1, -1.0, pe_m)
    v2 = rmax(pe_m2)
    i2 = rmin(jnp.where(pe_m2 == v2, lane, big))
    w1 = pg_sel * v1 / (v1 + v2)
    w2 = pg_sel * v2 / (v1 + v2)
    a1, a2 = i1 - e0, i2 - e0
    lo, hi = jnp.minimum(a1, a2), jnp.maximum(a1, a2)
    pair = (lo * (2 * E_PER_GROUP - 1 - lo)) // 2 + (hi - lo - 1)
    bucket = (gsel * N_PAIRS + pair).astype(f32)
    w_lo = jnp.where(a1 < a2, w1, w2)
    w_hi = jnp.where(a1 < a2, w2, w1)
    return jnp.where(lane == RT_BUCKET, bucket,
                     jnp.where(lane == RT_WLO, w_lo, jnp.where(lane == RT_WHI, w_hi, 0.0)))


def _out_proj_kernel(mod_ref, of_ref, ob_ref, gz_ref, yf_ref, yb_ref, sz_ref, om_ref, x_ref, mods_ref,
                     gn_ref, sn_ref, n2_ref, wout_ref, wr_ref, br_ref, x1_ref, h2_ref, rt_ref):
    del mod_ref
    acc = None
    for h in range(GDN_HEADS):
        sl = slice(h * GDN_DV, (h + 1) * GDN_DV)
        o = _rms(of_ref[:, sl] + ob_ref[:, sl]) * gn_ref[...] * _silu(gz_ref[:, sl])
        t = jnp.dot(o.astype(bf16), wout_ref[sl, :], preferred_element_type=f32)
        acc = t if acc is None else acc + t
    y = (yf_ref[...] + yb_ref[...]) * _silu(sz_ref[...])
    ys = _rms(y) * sn_ref[...]
    acc = acc + jnp.dot(ys.astype(bf16), wout_ref[GDN_W:GDN_W + SSM_INNER, :], preferred_element_type=f32)
    acc = acc + jnp.dot(om_ref[...].astype(bf16), wout_ref[GDN_W + SSM_INNER:MIX_W, :],
                        preferred_element_type=f32)
    x1 = x_ref[...] + mods_ref[MOD_G1:MOD_G1 + 1, :] * acc
    x1_ref[...] = x1
    h2 = _rms(x1) * n2_ref[...]
    h2 = h2 * (1.0 + mods_ref[MOD_SC2:MOD_SC2 + 1, :]) + mods_ref[MOD_SH2:MOD_SH2 + 1, :]
    h2_ref[...] = h2
    rt_ref[...] = _route(_mm3(h2, wr_ref[...]) + br_ref[...])


def out_proj(lay, o_f, o_b, proj, y_f, y_b, o_mla, x, mods, gdn_norm, ssm_norm, norm2, w_out_b, w_rt, b_rt, layer):
    row_map = lambda i, mod: (i, 0)
    lyr = lambda i, mod: (layer, 0, 0)
    return pl.pallas_call(
        _out_proj_kernel,
        grid_spec=pltpu.PrefetchScalarGridSpec(
            num_scalar_prefetch=1, grid=(lay.n_blk,),
            in_specs=[
                pl.BlockSpec((TB, GDN_W), row_map),
                pl.BlockSpec((TB, GDN_W), row_map),
                pl.BlockSpec((TB, GDN_W), lambda i, mod: (i, COL_GZ // GDN_W)),
                pl.BlockSpec((TB, SSM_INNER), row_map),
                pl.BlockSpec((TB, SSM_INNER), row_map),
                pl.BlockSpec((TB, SSM_INNER), lambda i, mod: (i, COL_SZ // SSM_INNER)),
                pl.BlockSpec((TB, MLA_W), row_map),
                pl.BlockSpec((TB, D_MODEL), row_map),
                pl.BlockSpec((None, 6, D_MODEL), lambda i, mod: (mod[i], 0, 0)),
                pl.BlockSpec((None, 1, GDN_DV), lyr),
                pl.BlockSpec((None, 1, SSM_INNER), lyr),
                pl.BlockSpec((None, 1, D_MODEL), lyr),
                pl.BlockSpec((MIX_W, D_MODEL), lambda i, mod: (0, 0), pipeline_mode=pl.Buffered(1)),
                pl.BlockSpec((None, D_MODEL, LANES), lyr),
                pl.BlockSpec((None, 1, LANES), lyr),
            ],
            out_specs=[
                pl.BlockSpec((TB, D_MODEL), row_map),
                pl.BlockSpec((TB, D_MODEL), row_map),
                pl.BlockSpec((TB, LANES), row_map),
            ]),
        out_shape=[
            jax.ShapeDtypeStruct((lay.m, D_MODEL), f32),
            jax.ShapeDtypeStruct((lay.m, D_MODEL), f32),
            jax.ShapeDtypeStruct((lay.m, LANES), f32),
        ],
        compiler_params=pltpu.CompilerParams(dimension_semantics=("arbitrary",), vmem_limit_bytes=VMEM_LIMIT),
        name="out_proj",
    )(jnp.asarray(lay.mod), o_f, o_b, proj, y_f, y_b, proj, o_mla, x, mods, gdn_norm, ssm_norm, norm2,
      w_out_b, w_rt, b_rt)


TME = 256
PAIR_LO = (0, 0, 0, 1, 1, 2)
PAIR_HI = (1, 2, 3, 2, 3, 3)


def _moe_kernel(te_ref, nv_ref, nu_ref, src_ref,
                h2_hbm, ws_ref, wg_ref, wu_ref, wd_ref, y_hbm, xbuf, ybuf, gsem, ssem):
    t = pl.program_id(0)
    j = pl.program_id(1)
    used = t < nu_ref[0]
    base = t * TME
    nvalid = nv_ref[t]

    def row_in(r):
        return pltpu.make_async_copy(h2_hbm.at[pl.ds(src_ref[base + r], 1), :], xbuf.at[pl.ds(r, 1), :], gsem)

    def row_out(r):
        return pltpu.make_async_copy(ybuf.at[pl.ds(r, 1), :], y_hbm.at[pl.ds(src_ref[base + r], 1), :], ssem)

    @pl.when(jnp.logical_and(used, j == 0))
    def _():
        def issue(r, c):
            row_in(r).start()
            return c
        lax.fori_loop(0, TME, issue, 0)

        def drain(r, c):
            row_in(r).wait()
            return c
        lax.fori_loop(0, TME, drain, 0)

    @pl.when(used)
    def _():
        xb = xbuf[...].astype(bf16)
        hg = jnp.dot(xb, wg_ref[...], preferred_element_type=f32)
        hu = jnp.dot(xb, wu_ref[...], preferred_element_type=f32)
        hdn = (_silu(hg) * hu).astype(bf16)
        yo = jnp.dot(hdn, wd_ref[...], preferred_element_type=f32)
        lane = lax.broadcasted_iota(jnp.int32, (TME, LANES), 1)
        wsel = jnp.sum(jnp.where(lane == j, ws_ref[...], 0.0), axis=-1, keepdims=True)

        @pl.when(j == 0)
        def _():
            ybuf[...] = wsel * yo

        @pl.when(j == 1)
        def _():
            ybuf[...] = ybuf[...] + wsel * yo

    @pl.when(jnp.logical_and(used, j == 1))
    def _():
        def issue(r, c):
            @pl.when(r < nvalid)
            def _():
                row_out(r).start()
            return c
        lax.fori_loop(0, TME, issue, 0)

        def drain(r, c):
            @pl.when(r < nvalid)
            def _():
                row_out(r).wait()
            return c
        lax.fori_loop(0, TME, drain, 0)


def moe_dispatch(lay, route):
    m = lay.m
    n_tiles = m // TME + N_BUCKETS
    bucket = route[:, RT_BUCKET].astype(jnp.int32)
    order = jnp.argsort(bucket, stable=True).astype(jnp.int32)
    counts = jnp.zeros((N_BUCKETS,), jnp.int32).at[bucket].add(1)
    ptiles = (counts + TME - 1) // TME
    pend = jnp.cumsum(ptiles)
    pstart = pend - ptiles
    start = jnp.cumsum(counts) - counts
    sb = bucket[order]
    pos = pstart[sb] * TME + (jnp.arange(m, dtype=jnp.int32) - start[sb])
    src = jnp.zeros((n_tiles * TME,), jnp.int32).at[pos].set(order)
    wlo = route[:, RT_WLO][order]
    whi = route[:, RT_WHI][order]
    n_used = pend[-1]
    tid = jnp.arange(n_tiles, dtype=jnp.int32)
    tb = jnp.minimum(jnp.searchsorted(pend, tid, side="right").astype(jnp.int32), N_BUCKETS - 1)
    tb = jnp.where(tid < n_used, tb, tb[jnp.maximum(n_used - 1, 0)])
    nvalid = jnp.clip(counts[tb] - (tid - pstart[tb]) * TME, 0, TME)
    nvalid = jnp.where(tid < n_used, nvalid, 0).astype(jnp.int32)
    grp, pr = tb // N_PAIRS, tb % N_PAIRS
    e_lo = grp * E_PER_GROUP + jnp.asarray(PAIR_LO, jnp.int32)[pr]
    e_hi = grp * E_PER_GROUP + jnp.asarray(PAIR_HI, jnp.int32)[pr]
    swap = (tid % 2) == 1
    te = jnp.stack([jnp.where(swap, e_hi, e_lo), jnp.where(swap, e_lo, e_hi)], -1).reshape(-1)
    swap_row = swap[pos // TME]
    w0 = jnp.where(swap_row, whi, wlo)
    w1 = jnp.where(swap_row, wlo, whi)
    ws = jnp.zeros((n_tiles * TME, LANES), f32).at[pos, 0].set(w0).at[pos, 1].set(w1)
    te = jnp.concatenate([te, te[-2:]])
    return te.astype(jnp.int32), nvalid, n_used.reshape(1).astype(jnp.int32), src, ws, n_tiles


def moe_apply(lay, h2, route, wg_b, wu_b, wd_b):
    te, nvalid, n_used, src, ws, n_tiles = moe_dispatch(lay, route)
    wmap = lambda t, j, te, *_: (te[2 * t + j], 0, 0)
    return pl.pallas_call(
        _moe_kernel,
        grid_spec=pltpu.PrefetchScalarGridSpec(
            num_scalar_prefetch=4, grid=(n_tiles, 2),
            in_specs=[
                pl.BlockSpec(memory_space=pl.ANY),
                pl.BlockSpec((TME, LANES), lambda t, j, *_: (t, 0)),
                pl.BlockSpec((None, D_MODEL, EXPERT_FF), wmap),
                pl.BlockSpec((None, D_MODEL, EXPERT_FF), wmap),
                pl.BlockSpec((None, EXPERT_FF, D_MODEL), wmap),
            ],
            out_specs=pl.BlockSpec(memory_space=pl.ANY),
            scratch_shapes=[
                pltpu.VMEM((TME, D_MODEL), f32),
                pltpu.VMEM((TME, D_MODEL), f32),
                pltpu.SemaphoreType.DMA,
                pltpu.SemaphoreType.DMA,
            ]),
        out_shape=jax.ShapeDtypeStruct((lay.m, D_MODEL), f32),
        compiler_params=pltpu.CompilerParams(dimension_semantics=("arbitrary", "arbitrary"),
                                             vmem_limit_bytes=VMEM_LIMIT),
        name="moe",
    )(te, nvalid, n_used, src, h2, ws, wg_b, wu_b, wd_b)


def _final_kernel(mod_ref, x_ref, y_ref, mods_ref, g_ref, o_ref):
    del mod_ref
    x = x_ref[...] + mods_ref[MOD_G2:MOD_G2 + 1, :] * y_ref[...]
    o_ref[...] = _rms(x) * g_ref[...]


def final_norm_apply(lay, x, ytok, mods, g):
    row_map = lambda i, mod: (i, 0)
    return pl.pallas_call(
        _final_kernel,
        grid_spec=pltpu.PrefetchScalarGridSpec(
            num_scalar_prefetch=1, grid=(lay.n_blk,),
            in_specs=[
                pl.BlockSpec((TB, D_MODEL), row_map),
                pl.BlockSpec((TB, D_MODEL), row_map),
                pl.BlockSpec((None, 6, D_MODEL), lambda i, mod: (mod[i], 0, 0)),
                pl.BlockSpec((1, D_MODEL), lambda i, mod: (0, 0)),
            ],
            out_specs=pl.BlockSpec((TB, D_MODEL), row_map)),
        out_shape=jax.ShapeDtypeStruct((lay.m, D_MODEL), f32),
        compiler_params=pltpu.CompilerParams(dimension_semantics=("arbitrary",)),
        name="final_norm",
    )(jnp.asarray(lay.mod), x, ytok, mods, g)


def _reorder_w_in(w):
    g_qkv, g_z, g_a, g_b, s_z, s_xbc, s_dt, m_q, m_kv = jnp.split(w, np.cumsum(IN_SIZES)[:-1], axis=1)
    ckv, kpe = m_kv[:, :KV_RANK], m_kv[:, KV_RANK:]
    z = lambda n: jnp.zeros((w.shape[0], n), w.dtype)
    small = jnp.concatenate([g_a, g_b, s_dt, z(SM_KPE - SM_DT - 2 * SSM_HEADS), kpe, z(LANES - SM_KPE - MLA_ROPE)], 1)
    return jnp.concatenate([g_qkv, g_z, s_z, m_q, ckv, s_xbc, small], 1).astype(bf16)


def _reorder_w_uq(w):
    r = w.reshape(Q_RANK, MLA_HEADS, MLA_QK)
    return jnp.pad(r, ((0, 0), (0, 0), (0, LANES - MLA_QK))).reshape(Q_RANK, Q_HEADS_W).astype(bf16)


def _reorder_w_ukv(w):
    r = w.reshape(KV_RANK, MLA_HEADS, MLA_NOPE + MLA_VDIM)
    k = jnp.pad(r[:, :, :MLA_NOPE], ((0, 0), (0, 0), (0, LANES - MLA_NOPE))).reshape(KV_RANK, MLA_HEADS * LANES)
    v = r[:, :, MLA_NOPE:].reshape(KV_RANK, MLA_HEADS * MLA_VDIM)
    return jnp.concatenate([k, v], 1).astype(bf16)


def _lane_row(vals, lane0):
    n = vals.shape[-1]
    return jnp.pad(vals, ((0, 0), (lane0, LANES - lane0 - n)))


def _par_rows(row0, row1):
    z = jnp.zeros_like(row0)
    return jnp.stack([row0, row1] + [z] * (SUBLANES - 2), axis=1)


def kernel(x_prompt, x_sample, cache_ckv, cache_kpe, state_gdn, state_ssm, c, c_ctx, w_ada, b_ada, norm1, norm2, w_in, gdn_conv, gdn_A_log, gdn_dt_bias, gdn_norm, ssm_conv, ssm_conv_bias, ssm_A_log, ssm_dt_bias, ssm_D, ssm_norm, mla_q_norm, mla_w_uq, mla_kv_norm, mla_w_ukv, w_out, router_group, router_group_bias, router_expert, router_expert_bias, moe_w_gate, moe_w_up, moe_w_down, final_norm):
    n_ctx, t_ctx, _ = x_prompt.shape
    n_lat, t_lat, _ = x_sample.shape
    past = cache_ckv.shape[2]
    depth = w_in.shape[0]
    assert n_lat < SUBLANES and t_ctx % TB == 0 and t_lat % TB == 0 and past % TB == 0
    lay = Layout(n_ctx, t_ctx, n_lat, t_lat)
    m_ctx = lay.m_ctx

    x = jnp.concatenate([x_prompt.reshape(m_ctx, D_MODEL), x_sample.reshape(lay.m_lat, D_MODEL)], 0)
    cvec = jnp.concatenate([c_ctx[None], c, jnp.zeros((SUBLANES - 1 - n_lat, D_MODEL), f32)], 0)
    tables = rope_tables(lay)

    b_ada3 = b_ada[:, None, :]
    norm1_3, norm2_3 = norm1[:, None, :], norm2[:, None, :]
    gdn_par = _par_rows(_lane_row(gdn_A_log.reshape(depth, -1), SM_GA), _lane_row(gdn_dt_bias.reshape(depth, -1), SM_GA))
    ssm_par = _par_rows(_lane_row(ssm_A_log.reshape(depth, -1), SM_DT), _lane_row(ssm_dt_bias.reshape(depth, -1), SM_DT))
    d_lanes = jnp.pad(jnp.repeat(ssm_D, SSM_HEADDIM, axis=-1), ((0, 0), (0, SSM_XBC - SSM_INNER)))
    ssm_cpar = _par_rows(ssm_conv_bias, d_lanes)
    w_rt = jnp.pad(jnp.concatenate([router_group, router_expert], -1),
                   ((0, 0), (0, 0), (0, LANES - N_EGROUPS - N_EXPERTS)))
    b_rt = jnp.pad(jnp.concatenate([router_group_bias, router_expert_bias], -1),
                   ((0, 0), (0, LANES - N_EGROUPS - N_EXPERTS)))[:, None, :]
    kpe_cache = jnp.pad(cache_kpe, ((0, 0), (0, 0), (0, 0), (SM_KPE, LANES - SM_KPE - MLA_ROPE)))

    ckvs, kpes, gdns, ssms = [], [], [], []
    ytok = mods_prev = None
    for l in range(depth):
        mods = ada_mod(cvec, w_ada, b_ada3, l)
        res = None if l == 0 else (ytok, mods_prev)
        proj, x = in_proj(lay, x, mods, norm1_3, _reorder_w_in(w_in[l]), l, res)

        zg = jnp.zeros((n_ctx, GDN_HEADS, GDN_DK, GDN_DV), f32)
        o_f, sg_f = gdn_scan(lay, proj, gdn_conv, gdn_par, jnp.concatenate([zg, state_gdn[:, l, 0]], 0), l, False)
        o_b, sg_b = gdn_scan(lay, proj, gdn_conv, gdn_par, jnp.concatenate([zg, state_gdn[:, l, 1]], 0), l, True)

        zs = jnp.zeros((n_ctx, SSM_INNER, LANES), f32)
        y_f, hs_f = ssd_scan(lay, proj, ssm_conv, ssm_cpar, ssm_par,
                             jnp.concatenate([zs, ssm_state_to_lanes(state_ssm[:, l, 0])], 0), l, False)
        y_b, hs_b = ssd_scan(lay, proj, ssm_conv, ssm_cpar, ssm_par,
                             jnp.concatenate([zs, ssm_state_to_lanes(state_ssm[:, l, 1])], 0), l, True)

        q, ckvn, kpe = mla_prep(lay, proj, mla_q_norm[:, None, :], mla_kv_norm[:, None, :],
                                _reorder_w_uq(mla_w_uq[l]), tables, l)
        w_ukv_r = _reorder_w_ukv(mla_w_ukv[l])
        k_ctx, v_ctx = kv_up(ckvn[:m_ctx], kpe[:m_ctx], w_ukv_r)
        ckv_lat = jnp.concatenate([cache_ckv[:, l], ckvn[m_ctx:].reshape(n_lat, t_lat, KV_RANK)], 1)
        kpe_lat = jnp.concatenate([kpe_cache[:, l], kpe[m_ctx:].reshape(n_lat, t_lat, LANES)], 1)
        k_lat, v_lat = kv_up(ckv_lat.reshape(-1, KV_RANK), kpe_lat.reshape(-1, LANES), w_ukv_r)
        o_mla = jnp.concatenate([
            attention(q, k_ctx, v_ctx, 0, n_ctx, t_ctx, t_ctx),
            attention(q, k_lat, v_lat, m_ctx // TB, n_lat, t_lat, past + t_lat)], 0)

        x, h2, route = out_proj(lay, o_f, o_b, proj, y_f, y_b, o_mla, x, mods, gdn_norm[:, None, :],
                                ssm_norm[:, None, :], norm2_3, w_out[l].astype(bf16), w_rt, b_rt, l)
        ytok = moe_apply(lay, h2, route, moe_w_gate[l].astype(bf16), moe_w_up[l].astype(bf16),
                         moe_w_down[l].astype(bf16))
        mods_prev = mods

        ckvs.append(ckvn[:m_ctx].reshape(n_ctx, t_ctx, KV_RANK))
        kpes.append(kpe[:m_ctx, SM_KPE:SM_KPE + MLA_ROPE].reshape(n_ctx, t_ctx, MLA_ROPE))
        gdns.append(jnp.stack([sg_f[:n_ctx], sg_b[:n_ctx]], 1))
        ssms.append(jnp.stack([ssm_state_from_lanes(hs_f[:n_ctx]), ssm_state_from_lanes(hs_b[:n_ctx])], 1))

    y = final_norm_apply(lay, x, ytok, mods_prev, final_norm[None, :])
    return (y[:m_ctx].reshape(n_ctx, t_ctx, D_MODEL), y[m_ctx:].reshape(n_lat, t_lat, D_MODEL),
            jnp.stack(ckvs, 1), jnp.stack(kpes, 1), jnp.stack(gdns, 1), jnp.stack(ssms, 1))
```

```python
import functools
import math

import numpy as np
import jax
import jax.numpy as jnp
from jax import lax
from jax.experimental import pallas as pl
from jax.experimental.pallas import tpu as pltpu

f32 = jnp.float32
bf16 = jnp.bfloat16

D_MODEL = 2048
DEPTH = 2
GRID_W = 64
EPS = 1e-6
CONV_W = 5
CHUNK = 64

GDN_HEADS = 4
GDN_DK = 128
GDN_DV = 128
GDN_W = GDN_HEADS * GDN_DV
GDN_QKV = 2 * GDN_HEADS * GDN_DK + GDN_HEADS * GDN_DV

SSM_HEADS = 16
SSM_HEADDIM = 64
SSM_GROUPS = 2
SSM_DSTATE = 64
SSM_INNER = SSM_HEADS * SSM_HEADDIM
SSM_XBC = SSM_INNER + 2 * SSM_GROUPS * SSM_DSTATE

MLA_HEADS = 8
MLA_NOPE = 64
MLA_ROPE = 32
MLA_VDIM = 64
Q_RANK = 512
KV_RANK = 256
MLA_W = MLA_HEADS * MLA_VDIM
ROPE_THETA = 10000.0

MIX_W = GDN_W + SSM_INNER + MLA_W
IN_SIZES = (GDN_QKV, GDN_W, 2 * GDN_HEADS, 2 * GDN_HEADS, SSM_INNER, SSM_XBC, 2 * SSM_HEADS, Q_RANK, KV_RANK + MLA_ROPE)

N_EGROUPS = 4
E_PER_GROUP = 4
N_EXPERTS = N_EGROUPS * E_PER_GROUP
EXPERT_FF = 512

LANES = 128
SUBLANES = 8

TB = 256

COL_QKV = 0
COL_GZ = 1536
COL_SZ = 2048
COL_MQ = 3072
COL_CKV = 3584
COL_XBC = 3840
COL_SMALL = 5120
PROJ_W = 5248
SM_GA, SM_GB, SM_DT, SM_KPE = 0, 8, 16, 64


class Layout:
    def __init__(self, n_ctx, t_ctx, n_lat, t_lat):
        self.n_ctx, self.t_ctx, self.n_lat, self.t_lat = n_ctx, t_ctx, n_lat, t_lat
        self.m_ctx = n_ctx * t_ctx
        self.m_lat = n_lat * t_lat
        self.m = self.m_ctx + self.m_lat
        self.n_seq = n_ctx + n_lat
        bc, bl = t_ctx // TB, t_lat // TB
        seq, first, last, mod = [], [], [], []
        for s in range(n_ctx):
            for j in range(bc):
                seq.append(s); first.append(int(j == 0)); last.append(int(j == bc - 1)); mod.append(0)
        for s in range(n_lat):
            for j in range(bl):
                seq.append(n_ctx + s); first.append(int(j == 0)); last.append(int(j == bl - 1)); mod.append(1 + s)
        self.n_blk = len(seq)
        self.seq = np.array(seq, np.int32)
        self.first = np.array(first, np.int32)
        self.last = np.array(last, np.int32)
        self.mod = np.array(mod + mod[-1:], np.int32)
        fwd = np.arange(self.n_blk, dtype=np.int32)
        bwd = []
        i = 0
        while i < self.n_blk:
            j = i
            while self.last[j] == 0:
                j += 1
            bwd.extend(range(j, i - 1, -1))
            i = j + 1
        self.order = {False: fwd, True: np.array(bwd, np.int32)}

    def scan_tables(self, rev):
        order = self.order[rev]
        pad = lambda a: jnp.asarray(np.concatenate([a, a[-1:]]))
        return pad(order), pad(self.seq[order]), pad(self.first[order]), pad(self.last[order])


def _mm(a, b):
    return jnp.dot(a.astype(bf16), b.astype(bf16), preferred_element_type=f32)


def _mm_nt(a, b):
    return lax.dot_general(a.astype(bf16), b.astype(bf16), (((1,), (1,)), ((), ())), preferred_element_type=f32)


def _mm_tn(a, b):
    n = a.shape[1]
    eye = jnp.where(lax.broadcasted_iota(jnp.int32, (n, n), 0) == lax.broadcasted_iota(jnp.int32, (n, n), 1),
                    1.0, 0.0).astype(bf16)
    at = lax.dot_general(eye, a.astype(bf16), (((1,), (1,)), ((), ())), preferred_element_type=f32)
    return jnp.dot(at.astype(bf16), b.astype(bf16), preferred_element_type=f32)


def _split3(x):
    hi = x.astype(bf16)
    r = x - hi.astype(f32)
    mid = r.astype(bf16)
    lo = (r - mid.astype(f32)).astype(bf16)
    return hi, mid, lo


def _mm01(m01, x):
    hi, mid, lo = _split3(x)
    d = lambda t: jnp.dot(m01, t, preferred_element_type=f32)
    return d(hi) + d(mid) + d(lo)


def _mm3(a, b):
    ah = a.astype(bf16)
    al = (a - ah.astype(f32)).astype(bf16)
    bh = b.astype(bf16)
    bl = (b - bh.astype(f32)).astype(bf16)
    d = lambda x, y: jnp.dot(x, y, preferred_element_type=f32)
    return d(ah, bh) + d(ah, bl) + d(al, bh)


def _sigmoid(x):
    return 1.0 / (1.0 + jnp.exp(-x))


def _silu(x):
    return x * _sigmoid(x)


def _softplus(x):
    return jnp.maximum(x, 0.0) + jnp.log(1.0 + jnp.exp(-jnp.abs(x)))


def _chunk_masks(rev):
    i = lax.broadcasted_iota(jnp.int32, (CHUNK, CHUNK), 0)
    j = lax.broadcasted_iota(jnp.int32, (CHUNK, CHUNK), 1)
    if rev:
        incl, strict = j >= i, j > i
    else:
        incl, strict = j <= i, j < i
    ll = jnp.where(incl, 1.0, 0.0).astype(bf16)
    lls = jnp.where(incl, 0.0, 1.0).astype(bf16)
    uu = jnp.where(strict, 1.0, 0.0)
    return incl, strict, ll, lls, uu


def _fill_ext(ext_ref, main_ref, prev_ref, next_ref, sfirst, slast):
    ext_ref[0:SUBLANES, :] = jnp.where(sfirst == 1, 0.0, prev_ref[...])
    ext_ref[SUBLANES:SUBLANES + TB, :] = main_ref[...]
    ext_ref[SUBLANES + TB:2 * SUBLANES + TB, :] = jnp.where(slast == 1, 0.0, next_ref[...])


def _conv_tile(ext_ref, w_ref, r0, col0, width, bias=None):
    acc = None
    for t in range(CONV_W):
        rows = ext_ref[pl.ds(SUBLANES + r0 - CONV_W // 2 + t, CHUNK), col0:col0 + width]
        term = rows * w_ref[t:t + 1, col0:col0 + width]
        acc = term if acc is None else acc + term
    if bias is not None:
        acc = acc + bias
    return _silu(acc)


INV_BLOCK = 16


def _block_masks():
    i = lax.broadcasted_iota(jnp.int32, (CHUNK, CHUNK), 0)
    j = lax.broadcasted_iota(jnp.int32, (CHUNK, CHUNK), 1)
    blk16 = (i // INV_BLOCK) == (j // INV_BLOCK)
    blk32 = (i // (2 * INV_BLOCK)) == (j // (2 * INV_BLOCK))
    return blk16, blk32


def _unit_tri_inverse_many(lmats, eye, blk16, blk32):
    ps = list(lmats)
    ld = {p: jnp.where(blk16, lmats[p], 0.0) for p in ps}
    t = {p: eye - ld[p] for p in ps}
    pw = {p: _mm3(ld[p], ld[p]) for p in ps}
    for it in range(3):
        t = {p: t[p] + _mm3(t[p], pw[p]) for p in ps}
        if it < 2:
            pw = {p: _mm3(pw[p], pw[p]) for p in ps}
    in32 = jnp.logical_and(blk32, jnp.logical_not(blk16))
    m = {p: _mm3(t[p], jnp.where(in32, lmats[p], 0.0)) for p in ps}
    t = {p: t[p] - _mm3(m[p], t[p]) for p in ps}
    m = {p: _mm3(t[p], jnp.where(blk32, 0.0, lmats[p])) for p in ps}
    return {p: t[p] - _mm3(m[p], t[p]) for p in ps}


def _gdn_kernel(order_ref, seq_ref, first_ref, last_ref,
                main_ref, prev_ref, next_ref, small_ref, convw_ref, par_ref, s0_ref,
                o_ref, sfin_ref, s_ref, ext_ref, *, rev):
    del order_ref, seq_ref
    step = pl.program_id(0)
    sfirst = first_ref[step]
    slast = last_ref[step]
    pfirst, plast = (slast, sfirst) if rev else (sfirst, slast)
    d = 1 if rev else 0

    @pl.when(pfirst == 1)
    def _():
        s_ref[...] = s0_ref[...]

    _fill_ext(ext_ref, main_ref, prev_ref, next_ref, sfirst, slast)

    incl, strict, ll, lls, uu = _chunk_masks(rev)
    eye = jnp.where(lax.broadcasted_iota(jnp.int32, (CHUNK, CHUNK), 0)
                    == lax.broadcasted_iota(jnp.int32, (CHUNK, CHUNK), 1), 1.0, 0.0)
    blk16, blk32 = _block_masks()
    neg_a = -jnp.exp(par_ref[0:1, :])
    dt_bias = par_ref[1:2, :]
    n_chunks = TB // CHUNK

    chunk_order = [n_chunks - 1 - cc if rev else cc for cc in range(n_chunks)]
    heads = range(GDN_HEADS)
    eye_k = jnp.where(lax.broadcasted_iota(jnp.int32, (GDN_DK, GDN_DK), 0)
                      == lax.broadcasted_iota(jnp.int32, (GDN_DK, GDN_DK), 1), 1.0, 0.0).astype(bf16)

    gates = {}
    for c in chunk_order:
        sm = small_ref[c * CHUNK:(c + 1) * CHUNK, :]
        la_blk = neg_a * _softplus(sm + dt_bias)
        gates[c] = (la_blk, _sigmoid(sm))
    g_blk = {c: _mm01(ll, gates[c][0]) for c in chunk_order}
    gr_blk = {c: _mm01(lls, gates[c][0]) for c in chunk_order}
    gl_blk = {c: jnp.sum(gates[c][0], axis=0, keepdims=True) for c in chunk_order}

    probs = [(c, h) for c in chunk_order for h in heads]
    ia = lambda h: SM_GA + d * GDN_HEADS + h
    ib = lambda h: SM_GB + d * GDN_HEADS + h
    col = lambda blk, lane: blk[:, lane:lane + 1]
    beta = {p: col(gates[p[0]][1], ib(p[1])) for p in probs}
    q, k, v = {}, {}, {}
    for p in probs:
        c, h = p
        r0 = c * CHUNK
        qq = _conv_tile(ext_ref, convw_ref, r0, h * GDN_DK, GDN_DK)
        kx = _conv_tile(ext_ref, convw_ref, r0, GDN_HEADS * GDN_DK + h * GDN_DK, GDN_DK)
        v[p] = _conv_tile(ext_ref, convw_ref, r0, 2 * GDN_HEADS * GDN_DK + h * GDN_DV, GDN_DV)
        q[p] = qq * lax.rsqrt(jnp.sum(qq * qq, axis=-1, keepdims=True) + EPS) * (GDN_DK ** -0.5)
        k[p] = kx * lax.rsqrt(jnp.sum(kx * kx, axis=-1, keepdims=True) + EPS)
    dmat = {p: _mm01(ll, col(gates[p[0]][0], ia(p[1])) * uu) for p in probs}
    kk = {p: _mm_nt(k[p], k[p]) for p in probs}
    qk = {p: _mm_nt(q[p], k[p]) for p in probs}
    decay = {p: jnp.where(incl, jnp.exp(dmat[p]), 0.0) for p in probs}
    lmat = {p: jnp.where(strict, kk[p] * beta[p] * decay[p], 0.0) for p in probs}
    qk = {p: jnp.where(incl, qk[p] * decay[p], 0.0) for p in probs}
    tinv = _unit_tri_inverse_many(lmat, eye, blk16, blk32)
    uw = {}
    for p in probs:
        e_g = jnp.exp(col(g_blk[p[0]], ia(p[1])))
        rhs = jnp.concatenate([v[p] * beta[p], k[p] * beta[p] * e_g], axis=1)
        uw[p] = _mm3(tinv[p], rhs)
        q[p] = q[p] * e_g
    kdt = {}
    for p in probs:
        k_dec = (k[p] * jnp.exp(col(gr_blk[p[0]], ia(p[1])))).astype(bf16)
        kdt[p] = lax.dot_general(eye_k, k_dec, (((1,), (1,)), ((), ())), preferred_element_type=f32)

    for c in chunk_order:
        ps = [(c, h) for h in heads]
        s_old = {p: s_ref[p[1]] for p in ps}
        ws = {p: _mm(uw[p][:, GDN_DV:], s_old[p]) for p in ps}
        qs = {p: _mm(q[p], s_old[p]) for p in ps}
        v_new = {p: uw[p][:, :GDN_DV] - ws[p] for p in ps}
        o = {p: qs[p] + _mm(qk[p], v_new[p]) for p in ps}
        for p in ps:
            h = p[1]
            s_ref[h] = s_old[p] * jnp.exp(col(gl_blk[c], ia(h))) + _mm(kdt[p], v_new[p])
            o_ref[c * CHUNK:(c + 1) * CHUNK, h * GDN_DV:(h + 1) * GDN_DV] = o[p]

    @pl.when(plast == 1)
    def _():
        sfin_ref[...] = s_ref[...]


def _halo_maps(n_rows8):
    prev_map = lambda i, order, *_: (jnp.maximum(order[i] * (TB // SUBLANES) - 1, 0), 0)
    next_map = lambda i, order, *_: (jnp.minimum((order[i] + 1) * (TB // SUBLANES), n_rows8 - 1), 0)
    return prev_map, next_map


def gdn_scan(lay, proj, convw, par, s0, layer, rev):
    prev_map, next_map = _halo_maps(lay.m // SUBLANES)
    blk_map = lambda i, order, *_: (order[i], 0)
    seq_map = lambda i, order, seq, *_: (seq[i], 0, 0, 0)
    grid_spec = pltpu.PrefetchScalarGridSpec(
        num_scalar_prefetch=4,
        grid=(lay.n_blk,),
        in_specs=[
            pl.BlockSpec((TB, GDN_QKV), blk_map),
            pl.BlockSpec((SUBLANES, GDN_QKV), prev_map),
            pl.BlockSpec((SUBLANES, GDN_QKV), next_map),
            pl.BlockSpec((TB, LANES), lambda i, order, *_: (order[i], COL_SMALL // LANES)),
            pl.BlockSpec((None, CONV_W, GDN_QKV), lambda i, *_: (layer, 0, 0)),
            pl.BlockSpec((None, SUBLANES, LANES), lambda i, *_: (layer, 0, 0)),
            pl.BlockSpec((None, GDN_HEADS, GDN_DK, GDN_DV), seq_map),
        ],
        out_specs=[
            pl.BlockSpec((TB, GDN_W), blk_map),
            pl.BlockSpec((None, GDN_HEADS, GDN_DK, GDN_DV), seq_map),
        ],
        scratch_shapes=[
            pltpu.VMEM((GDN_HEADS, GDN_DK, GDN_DV), f32),
            pltpu.VMEM((TB + 2 * SUBLANES, GDN_QKV), f32),
        ],
    )
    return pl.pallas_call(
        functools.partial(_gdn_kernel, rev=rev),
        grid_spec=grid_spec,
        out_shape=[
            jax.ShapeDtypeStruct((lay.m, GDN_W), f32),
            jax.ShapeDtypeStruct((lay.n_seq, GDN_HEADS, GDN_DK, GDN_DV), f32),
        ],
        compiler_params=pltpu.CompilerParams(dimension_semantics=("arbitrary",)),
        name="gdn_bwd" if rev else "gdn_fwd",
    )(*lay.scan_tables(rev), proj, proj, proj, proj, convw, par, s0)


SSM_PAIRS = SSM_HEADS // 2
PAIRS_PER_GROUP = SSM_PAIRS // SSM_GROUPS
COL_B = SSM_INNER
COL_C = SSM_INNER + SSM_GROUPS * SSM_DSTATE


def _ssd_kernel(order_ref, seq_ref, first_ref, last_ref,
                main_ref, prev_ref, next_ref, small_ref, convw_ref, cpar_ref, par_ref, h0_ref,
                y_ref, hfin_ref, h_ref, ext_ref, *, rev):
    del order_ref, seq_ref
    step = pl.program_id(0)
    sfirst = first_ref[step]
    slast = last_ref[step]
    pfirst, plast = (slast, sfirst) if rev else (sfirst, slast)
    d = 1 if rev else 0

    @pl.when(pfirst == 1)
    def _():
        h_ref[...] = h0_ref[...]

    _fill_ext(ext_ref, main_ref, prev_ref, next_ref, sfirst, slast)

    incl, _, ll, lls, uu = _chunk_masks(rev)
    lane_lo = lax.broadcasted_iota(jnp.int32, (CHUNK, LANES), 1) < SSM_DSTATE
    row_lo = lax.broadcasted_iota(jnp.int32, (LANES, LANES), 0) < SSM_HEADDIM
    eye_p = jnp.where(lax.broadcasted_iota(jnp.int32, (LANES, LANES), 0)
                      == lax.broadcasted_iota(jnp.int32, (LANES, LANES), 1), 1.0, 0.0).astype(bf16)
    neg_a = -jnp.exp(par_ref[0:1, :])
    dt_bias = par_ref[1:2, :]
    n_chunks = TB // CHUNK

    for cc in range(n_chunks):
        c = n_chunks - 1 - cc if rev else cc
        r0 = c * CHUNK
        sm = small_ref[r0:r0 + CHUNK, :]
        dt_blk = _softplus(sm + dt_bias)
        dta_blk = dt_blk * neg_a
        acum_blk = _mm01(ll, dta_blk)
        ar_blk = _mm01(lls, dta_blk)
        al_blk = jnp.sum(dta_blk, axis=0, keepdims=True)
        b_pair = _conv_tile(ext_ref, convw_ref, r0, COL_B, LANES, cpar_ref[0:1, COL_B:COL_B + LANES])
        c_pair = _conv_tile(ext_ref, convw_ref, r0, COL_C, LANES, cpar_ref[0:1, COL_C:COL_C + LANES])
        c_g, b_g, cb = [], [], []
        for g in range(SSM_GROUPS):
            gmask = lane_lo if g == 0 else jnp.logical_not(lane_lo)
            c_g.append(jnp.where(gmask, c_pair, 0.0))
            b_g.append(jnp.where(gmask, b_pair, 0.0))
        cb = [_mm_nt(c_g[g], b_pair) for g in range(SSM_GROUPS)]
        pairs = range(SSM_PAIRS)
        grp = lambda p: p // PAIRS_PER_GROUP
        lane_of = lambda hh: SM_DT + d * SSM_HEADS + hh
        colv = lambda blk, hh: blk[:, lane_of(hh):lane_of(hh) + 1]
        both = lambda blk, p: jnp.where(lane_lo, colv(blk, 2 * p), colv(blk, 2 * p + 1))
        xs = [_conv_tile(ext_ref, convw_ref, r0, p * LANES, LANES, cpar_ref[0:1, p * LANES:(p + 1) * LANES])
              for p in pairs]
        dmat = [_mm01(ll, colv(dta_blk, hh) * uu) for hh in range(SSM_HEADS)]
        m_h = [cb[grp(hh // 2)] * jnp.where(incl, jnp.exp(dmat[hh]), 0.0) for hh in range(SSM_HEADS)]
        xdt = [xs[p] * both(dt_blk, p) for p in pairs]
        y = [jnp.where(lane_lo, _mm(m_h[2 * p], xdt[p]), _mm(m_h[2 * p + 1], xdt[p])) for p in pairs]
        xdt_t = [lax.dot_general(eye_p, xdt[p].astype(bf16), (((1,), (1,)), ((), ())),
                                 preferred_element_type=f32).astype(bf16) for p in pairs]
        st_new = [jnp.where(row_lo,
                            jnp.dot(xdt_t[p], (b_g[grp(p)] * jnp.exp(colv(ar_blk, 2 * p))).astype(bf16),
                                    preferred_element_type=f32),
                            jnp.dot(xdt_t[p], (b_g[grp(p)] * jnp.exp(colv(ar_blk, 2 * p + 1))).astype(bf16),
                                    preferred_element_type=f32)) for p in pairs]
        h_pair = [h_ref[p * LANES:(p + 1) * LANES, :] for p in pairs]
        y_off = [_mm_nt(c_g[grp(p)], h_pair[p]) for p in pairs]
        for p in pairs:
            col = p * LANES
            yp = y[p] + y_off[p] * jnp.exp(both(acum_blk, p))
            if not rev:
                yp = yp + xs[p] * cpar_ref[1:2, col:col + LANES]
            dec = jnp.where(row_lo, jnp.exp(colv(al_blk, 2 * p)), jnp.exp(colv(al_blk, 2 * p + 1)))
            h_ref[col:col + LANES, :] = h_pair[p] * dec + st_new[p]
            y_ref[r0:r0 + CHUNK, col:col + LANES] = yp

    @pl.when(plast == 1)
    def _():
        hfin_ref[...] = h_ref[...]


def ssd_scan(lay, proj, convw, cpar, par, h0, layer, rev):
    prev_map, next_map = _halo_maps(lay.m // SUBLANES)
    cblk = COL_XBC // SSM_XBC
    blk_map = lambda i, order, *_: (order[i], 0)
    seq_map = lambda i, order, seq, *_: (seq[i], 0, 0)
    grid_spec = pltpu.PrefetchScalarGridSpec(
        num_scalar_prefetch=4,
        grid=(lay.n_blk,),
        in_specs=[
            pl.BlockSpec((TB, SSM_XBC), lambda i, order, *_: (order[i], cblk)),
            pl.BlockSpec((SUBLANES, SSM_XBC), lambda i, *a: (prev_map(i, *a)[0], cblk)),
            pl.BlockSpec((SUBLANES, SSM_XBC), lambda i, *a: (next_map(i, *a)[0], cblk)),
            pl.BlockSpec((TB, LANES), lambda i, order, *_: (order[i], COL_SMALL // LANES)),
            pl.BlockSpec((None, CONV_W, SSM_XBC), lambda i, *_: (layer, 0, 0)),
            pl.BlockSpec((None, SUBLANES, SSM_XBC), lambda i, *_: (layer, 0, 0)),
            pl.BlockSpec((None, SUBLANES, LANES), lambda i, *_: (layer, 0, 0)),
            pl.BlockSpec((None, SSM_INNER, LANES), seq_map),
        ],
        out_specs=[
            pl.BlockSpec((TB, SSM_INNER), blk_map),
            pl.BlockSpec((None, SSM_INNER, LANES), seq_map),
        ],
        scratch_shapes=[
            pltpu.VMEM((SSM_INNER, LANES), f32),
            pltpu.VMEM((TB + 2 * SUBLANES, SSM_XBC), f32),
        ],
    )
    return pl.pallas_call(
        functools.partial(_ssd_kernel, rev=rev),
        grid_spec=grid_spec,
        out_shape=[
            jax.ShapeDtypeStruct((lay.m, SSM_INNER), f32),
            jax.ShapeDtypeStruct((lay.n_seq, SSM_INNER, LANES), f32),
        ],
        compiler_params=pltpu.CompilerParams(dimension_semantics=("arbitrary",)),
        name="ssd_bwd" if rev else "ssd_fwd",
    )(*lay.scan_tables(rev), proj, proj, proj, proj, convw, cpar, par, h0)


def ssm_state_to_lanes(h):
    n = h.shape[0]
    hg = h.reshape(n, SSM_GROUPS, SSM_HEADS // SSM_GROUPS * SSM_HEADDIM, SSM_DSTATE)
    z = jnp.zeros_like(hg[:, 0])
    return jnp.concatenate([jnp.concatenate([hg[:, 0], z], -1), jnp.concatenate([z, hg[:, 1]], -1)], 1)


def ssm_state_from_lanes(hl):
    n = hl.shape[0]
    r = hl.reshape(n, SSM_GROUPS, SSM_INNER // SSM_GROUPS, SSM_GROUPS, SSM_DSTATE)
    h = jnp.stack([r[:, g, :, g, :] for g in range(SSM_GROUPS)], 1)
    return h.reshape(n, SSM_HEADS, SSM_HEADDIM, SSM_DSTATE)


ADA_TN = 1536
VMEM_LIMIT = 56 * 1024 * 1024


def _ada_kernel(c_ref, w_ref, b_ref, o_ref):
    s = _silu(c_ref[...])
    o_ref[...] = jnp.dot(s.astype(bf16), w_ref[...].astype(bf16), preferred_element_type=f32) + b_ref[...]


def ada_mod(cvec, w_ada, b_ada, layer):
    n = 6 * D_MODEL
    out = pl.pallas_call(
        _ada_kernel,
        grid=(n // ADA_TN,),
        in_specs=[
            pl.BlockSpec((SUBLANES, D_MODEL), lambda j: (0, 0)),
            pl.BlockSpec((None, D_MODEL, ADA_TN), lambda j: (layer, 0, j)),
            pl.BlockSpec((None, 1, ADA_TN), lambda j: (layer, 0, j)),
        ],
        out_specs=pl.BlockSpec((SUBLANES, ADA_TN), lambda j: (0, j)),
        out_shape=jax.ShapeDtypeStruct((SUBLANES, n), f32),
        compiler_params=pltpu.CompilerParams(dimension_semantics=("arbitrary",), vmem_limit_bytes=VMEM_LIMIT),
        name="ada_mod",
    )(cvec, w_ada, b_ada)
    return out.reshape(SUBLANES, 6, D_MODEL)


MOD_SH1, MOD_SC1, MOD_G1, MOD_SH2, MOD_SC2, MOD_G2 = range(6)


def _rms(x):
    return x * lax.rsqrt(jnp.mean(x * x, axis=-1, keepdims=True) + EPS)


PROJ_CHUNK = 512


def _in_proj_kernel(mod_ref, *refs, has_res):
    del mod_ref
    if has_res:
        x_ref, y_ref, pmods_ref, mods_ref, n1_ref, w_ref, proj_ref, xo_ref = refs
        x = x_ref[...] + pmods_ref[MOD_G2:MOD_G2 + 1, :] * y_ref[...]
        xo_ref[...] = x
    else:
        x_ref, mods_ref, n1_ref, w_ref, proj_ref = refs
        x = x_ref[...]
    h = _rms(x) * n1_ref[...]
    h = h * (1.0 + mods_ref[MOD_SC1:MOD_SC1 + 1, :]) + mods_ref[MOD_SH1:MOD_SH1 + 1, :]
    hb = h.astype(bf16)
    for c0 in range(0, PROJ_W, PROJ_CHUNK):
        wd = min(PROJ_CHUNK, PROJ_W - c0)
        proj_ref[:, c0:c0 + wd] = jnp.dot(hb, w_ref[:, c0:c0 + wd], preferred_element_type=f32)


def in_proj(lay, x, mods, norm1, w_in_r, layer, res=None):
    row_map = lambda i, mod: (i, 0)
    mod_map = lambda i, mod: (mod[i], 0, 0)
    in_specs = [pl.BlockSpec((TB, D_MODEL), row_map)]
    args = [x]
    if res is not None:
        in_specs += [pl.BlockSpec((TB, D_MODEL), row_map), pl.BlockSpec((None, 6, D_MODEL), mod_map)]
        args += [res[0], res[1]]
    in_specs += [
        pl.BlockSpec((None, 6, D_MODEL), mod_map),
        pl.BlockSpec((None, 1, D_MODEL), lambda i, mod: (layer, 0, 0)),
        pl.BlockSpec((D_MODEL, PROJ_W), lambda i, mod: (0, 0), pipeline_mode=pl.Buffered(1)),
    ]
    args += [mods, norm1, w_in_r]
    out_specs = [pl.BlockSpec((TB, PROJ_W), row_map)]
    out_shape = [jax.ShapeDtypeStruct((lay.m, PROJ_W), f32)]
    if res is not None:
        out_specs.append(pl.BlockSpec((TB, D_MODEL), row_map))
        out_shape.append(jax.ShapeDtypeStruct((lay.m, D_MODEL), f32))
    outs = pl.pallas_call(
        functools.partial(_in_proj_kernel, has_res=res is not None),
        grid_spec=pltpu.PrefetchScalarGridSpec(
            num_scalar_prefetch=1, grid=(lay.n_blk,), in_specs=in_specs, out_specs=out_specs),
        out_shape=out_shape,
        compiler_params=pltpu.CompilerParams(dimension_semantics=("arbitrary",), vmem_limit_bytes=VMEM_LIMIT),
        name="in_proj",
    )(jnp.asarray(lay.mod), *args)
    return (outs[0], outs[1]) if res is not None else (outs[0], x)


MLA_QK = MLA_NOPE + MLA_ROPE
Q_HEADS_W = MLA_HEADS * LANES
ROPE_HALF = MLA_ROPE // 2


def _rope_lanes(x, c, s1, s2):
    return x * c + pltpu.roll(x, ROPE_HALF, 1) * s1 + pltpu.roll(x, LANES - ROPE_HALF, 1) * s2


def _mla_prep_kernel(pos_ref, mq_ref, ckv_ref, small_ref, qn_ref, kvn_ref, wuq_ref, rc_ref, rs1_ref, rs2_ref,
                     q_ref, ckvn_ref, kpe_ref):
    del pos_ref
    c, s1, s2 = rc_ref[...], rs1_ref[...], rs2_ref[...]
    cq = (_rms(mq_ref[...]) * qn_ref[...]).astype(bf16)
    qf = jnp.dot(cq, wuq_ref[...], preferred_element_type=f32)
    scale = MLA_QK ** -0.5 * math.log2(math.e)
    for h in range(MLA_HEADS):
        xh = qf[:, h * LANES:(h + 1) * LANES]
        q_ref[:, h * LANES:(h + 1) * LANES] = (_rope_lanes(xh, c, s1, s2) * scale).astype(bf16)
    ckvn_ref[...] = _rms(ckv_ref[...]) * kvn_ref[...]
    lane = lax.broadcasted_iota(jnp.int32, (TB, LANES), 1)
    in_rope = jnp.logical_and(lane >= SM_KPE, lane < SM_KPE + MLA_ROPE)
    kpe_ref[...] = jnp.where(in_rope, _rope_lanes(small_ref[...], c, s1, s2), 0.0)


def rope_tables(lay):
    t = lay.t_lat
    rows = t // GRID_W
    row = jnp.repeat(jnp.arange(rows, dtype=f32), GRID_W)
    col = jnp.tile(jnp.arange(GRID_W, dtype=f32), rows)
    nf = MLA_ROPE // 4
    inv = jnp.power(ROPE_THETA, -jnp.arange(nf, dtype=f32) / nf)
    ang = jnp.concatenate([row[:, None] * inv, col[:, None] * inv], axis=-1)
    cos, sin = jnp.cos(ang), jnp.sin(ang)
    z = lambda n: jnp.zeros((t, n), f32)
    c = jnp.concatenate([jnp.ones((t, MLA_NOPE), f32), cos, cos, z(LANES - MLA_QK)], -1)
    s1 = jnp.concatenate([z(MLA_NOPE + ROPE_HALF), sin, z(LANES - MLA_QK)], -1)
    s2 = jnp.concatenate([z(MLA_NOPE), -sin, z(ROPE_HALF + LANES - MLA_QK)], -1)
    ident = jnp.concatenate([jnp.ones((TB, MLA_QK), f32), jnp.zeros((TB, LANES - MLA_QK), f32)], -1)
    zero = jnp.zeros((TB, LANES), f32)
    return (jnp.concatenate([ident, c]), jnp.concatenate([zero, s1]), jnp.concatenate([zero, s2]))


def _pos_blocks(lay):
    pos = [0] * (lay.m_ctx // TB)
    for _ in range(lay.n_lat):
        pos += [1 + j for j in range(lay.t_lat // TB)]
    return np.array(pos + pos[-1:], np.int32)


def mla_prep(lay, proj, q_norm, kv_norm, w_uq_r, tables, layer):
    row_map = lambda i, pos: (i, 0)
    tab_map = lambda i, pos: (pos[i], 0)
    return pl.pallas_call(
        _mla_prep_kernel,
        grid_spec=pltpu.PrefetchScalarGridSpec(
            num_scalar_prefetch=1, grid=(lay.n_blk,),
            in_specs=[
                pl.BlockSpec((TB, Q_RANK), lambda i, pos: (i, COL_MQ // Q_RANK)),
                pl.BlockSpec((TB, KV_RANK), lambda i, pos: (i, COL_CKV // KV_RANK)),
                pl.BlockSpec((TB, LANES), lambda i, pos: (i, COL_SMALL // LANES)),
                pl.BlockSpec((None, 1, Q_RANK), lambda i, pos: (layer, 0, 0)),
                pl.BlockSpec((None, 1, KV_RANK), lambda i, pos: (layer, 0, 0)),
                pl.BlockSpec((Q_RANK, Q_HEADS_W), lambda i, pos: (0, 0)),
                pl.BlockSpec((TB, LANES), tab_map),
                pl.BlockSpec((TB, LANES), tab_map),
                pl.BlockSpec((TB, LANES), tab_map),
            ],
            out_specs=[
                pl.BlockSpec((TB, Q_HEADS_W), row_map),
                pl.BlockSpec((TB, KV_RANK), row_map),
                pl.BlockSpec((TB, LANES), row_map),
            ]),
        out_shape=[
            jax.ShapeDtypeStruct((lay.m, Q_HEADS_W), bf16),
            jax.ShapeDtypeStruct((lay.m, KV_RANK), f32),
            jax.ShapeDtypeStruct((lay.m, LANES), f32),
        ],
        compiler_params=pltpu.CompilerParams(dimension_semantics=("arbitrary",)),
        name="mla_prep",
    )(jnp.asarray(_pos_blocks(lay)), proj, proj, proj, q_norm, kv_norm, w_uq_r, *tables)


V_PAIRS = MLA_HEADS // 2
KV_UP_W = 2 * MLA_HEADS * LANES


def _kv_up_kernel(ckv_ref, kpe_ref, w_ref, k_ref, v_ref):
    kv = jnp.dot(ckv_ref[...].astype(bf16), w_ref[...], preferred_element_type=f32)
    kpe = kpe_ref[...]
    for h in range(MLA_HEADS):
        k_ref[h] = (kv[:, h * LANES:(h + 1) * LANES] + kpe).astype(bf16)
    v0 = MLA_HEADS * LANES
    lane_lo = lax.broadcasted_iota(jnp.int32, (TB, LANES), 1) < MLA_VDIM
    for h in range(MLA_HEADS):
        v_ref[h] = jnp.where(lane_lo, kv[:, v0 + h * LANES:v0 + (h + 1) * LANES], 1.0).astype(bf16)


def kv_up(ckv, kpe, w_ukv_r):
    r = ckv.shape[0]
    return pl.pallas_call(
        _kv_up_kernel,
        grid=(r // TB,),
        in_specs=[
            pl.BlockSpec((TB, KV_RANK), lambda i: (i, 0)),
            pl.BlockSpec((TB, LANES), lambda i: (i, 0)),
            pl.BlockSpec((KV_RANK, KV_UP_W), lambda i: (0, 0)),
        ],
        out_specs=[
            pl.BlockSpec((MLA_HEADS, TB, LANES), lambda i: (0, i, 0)),
            pl.BlockSpec((MLA_HEADS, TB, LANES), lambda i: (0, i, 0)),
        ],
        out_shape=[
            jax.ShapeDtypeStruct((MLA_HEADS, r, LANES), bf16),
            jax.ShapeDtypeStruct((MLA_HEADS, r, LANES), bf16),
        ],
        compiler_params=pltpu.CompilerParams(dimension_semantics=("arbitrary",)),
        name="kv_up",
    )(ckv, kpe, w_ukv_r)


ATT_KC = 512


def _attn_kernel(q_ref, k_ref, v_ref, o_ref, m_ref, acc_ref, *, n_keys):
    lane_lo = lax.broadcasted_iota(jnp.int32, (TB, LANES), 1) < MLA_VDIM
    kc = min(ATT_KC, n_keys)
    hs = range(MLA_HEADS)
    m_ref[...] = jnp.full((MLA_HEADS, TB, LANES), -jnp.inf, f32)
    acc_ref[...] = jnp.zeros((MLA_HEADS, TB, LANES), f32)

    def body(c, carry):
        c0 = pl.multiple_of(c * kc, kc)
        score = lambda h: lax.dot_general(q_ref[:, h * LANES:(h + 1) * LANES], k_ref[h, pl.ds(c0, kc), :],
                                          (((1,), (1,)), ((), ())), preferred_element_type=f32)
        s_next = score(0)
        for h in hs:
            s = s_next
            if h + 1 < MLA_HEADS:
                s_next = score(h + 1)
            m_old = m_ref[h]
            m_new = jnp.maximum(m_old, jnp.max(s, axis=-1, keepdims=True))
            alpha = jnp.exp2(m_old - m_new)
            pe = jnp.exp2(s - m_new[:, 0:1])
            pv = jnp.dot(pe.astype(bf16), v_ref[h, pl.ds(c0, kc), :], preferred_element_type=f32)
            acc_ref[h] = alpha * acc_ref[h] + pv
            m_ref[h] = m_new
        return carry

    lax.fori_loop(0, n_keys // kc, body, 0)
    half = LANES // 2
    for p in range(V_PAIRS):
        a_even = acc_ref[2 * p]
        a_odd = pltpu.roll(acc_ref[2 * p + 1], half, 1)
        o_even = a_even * pltpu.roll(1.0 / a_even, half, 1)
        o_odd = a_odd * pltpu.roll(1.0 / a_odd, half, 1)
        o_ref[:, p * LANES:(p + 1) * LANES] = jnp.where(lane_lo, o_even, o_odd)


def attention(q, k, v, q_blk0, n_seq, t_q, n_keys):
    qb = t_q // TB
    return pl.pallas_call(
        functools.partial(_attn_kernel, n_keys=n_keys),
        grid=(n_seq, qb),
        in_specs=[
            pl.BlockSpec((TB, Q_HEADS_W), lambda b, j: (q_blk0 + b * qb + j, 0)),
            pl.BlockSpec((MLA_HEADS, n_keys, LANES), lambda b, j: (0, b, 0)),
            pl.BlockSpec((MLA_HEADS, n_keys, LANES), lambda b, j: (0, b, 0)),
        ],
        out_specs=pl.BlockSpec((TB, MLA_W), lambda b, j: (b * qb + j, 0)),
        out_shape=jax.ShapeDtypeStruct((n_seq * t_q, MLA_W), f32),
        scratch_shapes=[pltpu.VMEM((MLA_HEADS, TB, LANES), f32)] * 2,
        compiler_params=pltpu.CompilerParams(dimension_semantics=("arbitrary", "arbitrary"),
                                             vmem_limit_bytes=VMEM_LIMIT),
        name="mla_attn",
    )(q, k, v)


N_PAIRS = E_PER_GROUP * (E_PER_GROUP - 1) // 2
N_BUCKETS = N_EGROUPS * N_PAIRS
RT_BUCKET, RT_WLO, RT_WHI = 0, 1, 2
H2X_W = D_MODEL + LANES


def _route(lg):
    lane = lax.broadcasted_iota(jnp.int32, lg.shape, 1)
    big = jnp.int32(LANES)
    rmax = lambda x: jnp.max(x, axis=-1, keepdims=True)
    rmin = lambda x: jnp.min(x, axis=-1, keepdims=True)
    rsum = lambda x: jnp.sum(x, axis=-1, keepdims=True)
    neg = -jnp.inf
    gmask = lane < N_EGROUPS
    gl = jnp.where(gmask, lg, neg)
    gmax = rmax(gl)
    gsum = rsum(jnp.where(gmask, jnp.exp(lg - gmax), 0.0))
    gsel = rmin(jnp.where(gl == gmax, lane, big))
    pg_sel = 1.0 / gsum
    e0 = N_EGROUPS + E_PER_GROUP * gsel
    emask = jnp.logical_and(lane >= e0, lane < e0 + E_PER_GROUP)
    el = jnp.where(emask, lg, neg)
    emax = rmax(el)
    ee = jnp.where(emask, jnp.exp(lg - emax), 0.0)
    pe = ee / rsum(ee)
    pe_m = jnp.where(emask, pe, -1.0)
    v1 = rmax(pe_m)
    i1 = rmin(jnp.where(pe_m == v1, lane, big))
    pe_m2 = jnp.where(lane == i1, -1.0, pe_m)
    v2 = rmax(pe_m2)
    i2 = rmin(jnp.where(pe_m2 == v2, lane, big))
    w1 = pg_sel * v1 / (v1 + v2)
    w2 = pg_sel * v2 / (v1 + v2)
    a1, a2 = i1 - e0, i2 - e0
    lo, hi = jnp.minimum(a1, a2), jnp.maximum(a1, a2)
    pair = (lo * (2 * E_PER_GROUP - 1 - lo)) // 2 + (hi - lo - 1)
    bucket = (gsel * N_PAIRS + pair).astype(f32)
    w_lo = jnp.where(a1 < a2, w1, w2)
    w_hi = jnp.where(a1 < a2, w2, w1)
    return jnp.where(lane == RT_BUCKET, bucket,
                     jnp.where(lane == RT_WLO, w_lo, jnp.where(lane == RT_WHI, w_hi, 0.0)))


def _out_proj_kernel(mod_ref, of_ref, ob_ref, gz_ref, yf_ref, yb_ref, sz_ref, om_ref, x_ref, mods_ref,
                     gn_ref, sn_ref, n2_ref, wout_ref, wr_ref, br_ref, x1_ref, h2_ref, rt_ref):
    del mod_ref
    acc = None
    for h in range(GDN_HEADS):
        sl = slice(h * GDN_DV, (h + 1) * GDN_DV)
        o = _rms(of_ref[:, sl] + ob_ref[:, sl]) * gn_ref[...] * _silu(gz_ref[:, sl])
        t = jnp.dot(o.astype(bf16), wout_ref[sl, :], preferred_element_type=f32)
        acc = t if acc is None else acc + t
    y = (yf_ref[...] + yb_ref[...]) * _silu(sz_ref[...])
    ys = _rms(y) * sn_ref[...]
    acc = acc + jnp.dot(ys.astype(bf16), wout_ref[GDN_W:GDN_W + SSM_INNER, :], preferred_element_type=f32)
    acc = acc + jnp.dot(om_ref[...].astype(bf16), wout_ref[GDN_W + SSM_INNER:MIX_W, :],
                        preferred_element_type=f32)
    x1 = x_ref[...] + mods_ref[MOD_G1:MOD_G1 + 1, :] * acc
    x1_ref[...] = x1
    h2 = _rms(x1) * n2_ref[...]
    h2 = h2 * (1.0 + mods_ref[MOD_SC2:MOD_SC2 + 1, :]) + mods_ref[MOD_SH2:MOD_SH2 + 1, :]
    rt = _route(_mm3(h2, wr_ref[...]) + br_ref[...])
    h2_ref[:, 0:D_MODEL] = h2
    h2_ref[:, D_MODEL:H2X_W] = rt
    rt_ref[...] = rt


def out_proj(lay, o_f, o_b, proj, y_f, y_b, o_mla, x, mods, gdn_norm, ssm_norm, norm2, w_out_b, w_rt, b_rt, layer):
    row_map = lambda i, mod: (i, 0)
    lyr = lambda i, mod: (layer, 0, 0)
    return pl.pallas_call(
        _out_proj_kernel,
        grid_spec=pltpu.PrefetchScalarGridSpec(
            num_scalar_prefetch=1, grid=(lay.n_blk,),
            in_specs=[
                pl.BlockSpec((TB, GDN_W), row_map),
                pl.BlockSpec((TB, GDN_W), row_map),
                pl.BlockSpec((TB, GDN_W), lambda i, mod: (i, COL_GZ // GDN_W)),
                pl.BlockSpec((TB, SSM_INNER), row_map),
                pl.BlockSpec((TB, SSM_INNER), row_map),
                pl.BlockSpec((TB, SSM_INNER), lambda i, mod: (i, COL_SZ // SSM_INNER)),
                pl.BlockSpec((TB, MLA_W), row_map),
                pl.BlockSpec((TB, D_MODEL), row_map),
                pl.BlockSpec((None, 6, D_MODEL), lambda i, mod: (mod[i], 0, 0)),
                pl.BlockSpec((None, 1, GDN_DV), lyr),
                pl.BlockSpec((None, 1, SSM_INNER), lyr),
                pl.BlockSpec((None, 1, D_MODEL), lyr),
                pl.BlockSpec((MIX_W, D_MODEL), lambda i, mod: (0, 0), pipeline_mode=pl.Buffered(1)),
                pl.BlockSpec((None, D_MODEL, LANES), lyr),
                pl.BlockSpec((None, 1, LANES), lyr),
            ],
            out_specs=[
                pl.BlockSpec((TB, D_MODEL), row_map),
                pl.BlockSpec((TB, H2X_W), row_map),
                pl.BlockSpec((TB, LANES), row_map),
            ]),
        out_shape=[
            jax.ShapeDtypeStruct((lay.m, D_MODEL), f32),
            jax.ShapeDtypeStruct((lay.m, H2X_W), f32),
            jax.ShapeDtypeStruct((lay.m, LANES), f32),
        ],
        compiler_params=pltpu.CompilerParams(dimension_semantics=("arbitrary",), vmem_limit_bytes=VMEM_LIMIT),
        name="out_proj",
    )(jnp.asarray(lay.mod), o_f, o_b, proj, y_f, y_b, proj, o_mla, x, mods, gdn_norm, ssm_norm, norm2,
      w_out_b, w_rt, b_rt)


TME = 256
PAIR_LO = (0, 0, 0, 1, 1, 2)
PAIR_HI = (1, 2, 3, 2, 3, 3)


DMA_UNROLL = 8


def _moe_kernel(te_ref, nu_ref, src_ref, dst_ref,
                h2_hbm, wg_ref, wu_ref, wd_ref, y_hbm, xbuf, ybuf, gsem, ssem):
    del te_ref
    t = pl.program_id(0)
    j = pl.program_id(1)
    n_used = nu_ref[0]
    used = t < n_used
    slot = t % 2

    def gather_rows(tile, sl):
        def issue(r, c):
            pltpu.make_async_copy(h2_hbm.at[pl.ds(src_ref[tile * TME + r], 1), :],
                                  xbuf.at[sl, pl.ds(r, 1), :], gsem.at[sl]).start()
            return c
        lax.fori_loop(0, TME, issue, 0, unroll=DMA_UNROLL)

    def scatter_rows(tile, sl):
        def issue(r, c):
            pltpu.make_async_copy(ybuf.at[sl, pl.ds(r, 1), :],
                                  y_hbm.at[pl.ds(dst_ref[tile * TME + r], 1), :], ssem.at[sl]).start()
            return c
        lax.fori_loop(0, TME, issue, 0, unroll=DMA_UNROLL)

    def wait_gather(sl):
        pltpu.make_async_copy(h2_hbm.at[pl.ds(0, TME), :], xbuf.at[sl], gsem.at[sl]).wait()

    def wait_scatter(sl):
        pltpu.make_async_copy(ybuf.at[sl], y_hbm.at[pl.ds(0, TME), :], ssem.at[sl]).wait()

    @pl.when(jnp.logical_and(used, jnp.logical_and(t == 0, j == 0)))
    def _():
        gather_rows(0, 0)
        n_tok = y_hbm.shape[0] - TME
        ybuf[1] = jnp.zeros((TME, D_MODEL), f32)
        spare = pltpu.make_async_copy(ybuf.at[1], y_hbm.at[pl.ds(n_tok, TME), :], ssem.at[1])
        spare.start()
        spare.wait()

    @pl.when(jnp.logical_and(used, j == 0))
    def _():
        wait_gather(slot)

    @pl.when(jnp.logical_and(t + 1 < n_used, j == 1))
    def _():
        gather_rows(t + 1, 1 - slot)

    @pl.when(used)
    def _():
        xb = xbuf[slot, :, 0:D_MODEL].astype(bf16)
        hg = jnp.dot(xb, wg_ref[...], preferred_element_type=f32)
        hu = jnp.dot(xb, wu_ref[...], preferred_element_type=f32)
        hdn = (_silu(hg) * hu).astype(bf16)
        yo = jnp.dot(hdn, wd_ref[...], preferred_element_type=f32)
        lane = lax.broadcasted_iota(jnp.int32, (TME, LANES), 1)
        w_lane = jnp.where(j == slot, RT_WLO, RT_WHI)
        wsel = jnp.sum(jnp.where(lane == w_lane, xbuf[slot, :, D_MODEL:H2X_W], 0.0), axis=-1, keepdims=True)

        @pl.when(j == 0)
        def _():
            ybuf[slot] = wsel * yo

        @pl.when(j == 1)
        def _():
            ybuf[slot] = ybuf[slot] + wsel * yo

    @pl.when(jnp.logical_and(used, j == 1))
    def _():
        @pl.when(t > 0)
        def _():
            wait_scatter(1 - slot)

        scatter_rows(t, slot)

        @pl.when(t == n_used - 1)
        def _():
            wait_scatter(slot)


def moe_dispatch(lay, route):
    m = lay.m
    n_tiles = m // TME + N_BUCKETS
    bucket = route[:, RT_BUCKET].astype(jnp.int32)
    order = jnp.argsort(bucket, stable=True).astype(jnp.int32)
    counts = jnp.zeros((N_BUCKETS,), jnp.int32).at[bucket].add(1)
    ptiles = (counts + TME - 1) // TME
    pend = jnp.cumsum(ptiles)
    pstart = pend - ptiles
    start = jnp.cumsum(counts) - counts
    sb = bucket[order]
    pos = pstart[sb] * TME + (jnp.arange(m, dtype=jnp.int32) - start[sb])
    src = jnp.zeros((n_tiles * TME,), jnp.int32).at[pos].set(order)
    dst = (m + jnp.arange(n_tiles * TME, dtype=jnp.int32) % TME).at[pos].set(order)
    n_used = pend[-1]
    tid = jnp.arange(n_tiles, dtype=jnp.int32)
    tb = jnp.minimum(jnp.searchsorted(pend, tid, side="right").astype(jnp.int32), N_BUCKETS - 1)
    tb = jnp.where(tid < n_used, tb, tb[jnp.maximum(n_used - 1, 0)])
    grp, pr = tb // N_PAIRS, tb % N_PAIRS
    e_lo = grp * E_PER_GROUP + jnp.asarray(PAIR_LO, jnp.int32)[pr]
    e_hi = grp * E_PER_GROUP + jnp.asarray(PAIR_HI, jnp.int32)[pr]
    swap = (tid % 2) == 1
    te = jnp.stack([jnp.where(swap, e_hi, e_lo), jnp.where(swap, e_lo, e_hi)], -1).reshape(-1)
    te = jnp.concatenate([te, te[-2:]])
    return te.astype(jnp.int32), n_used.reshape(1).astype(jnp.int32), src, dst, n_tiles


def moe_apply(lay, h2, route, wg_b, wu_b, wd_b):
    te, n_used, src, dst, n_tiles = moe_dispatch(lay, route)
    wmap = lambda t, j, te, *_: (te[2 * t + j], 0, 0)
    return pl.pallas_call(
        _moe_kernel,
        grid_spec=pltpu.PrefetchScalarGridSpec(
            num_scalar_prefetch=4, grid=(n_tiles, 2),
            in_specs=[
                pl.BlockSpec(memory_space=pl.ANY),
                pl.BlockSpec((None, D_MODEL, EXPERT_FF), wmap),
                pl.BlockSpec((None, D_MODEL, EXPERT_FF), wmap),
                pl.BlockSpec((None, EXPERT_FF, D_MODEL), wmap),
            ],
            out_specs=pl.BlockSpec(memory_space=pl.ANY),
            scratch_shapes=[
                pltpu.VMEM((2, TME, H2X_W), f32),
                pltpu.VMEM((2, TME, D_MODEL), f32),
                pltpu.SemaphoreType.DMA((2,)),
                pltpu.SemaphoreType.DMA((2,)),
            ]),
        out_shape=jax.ShapeDtypeStruct((lay.m + TME, D_MODEL), f32),
        compiler_params=pltpu.CompilerParams(dimension_semantics=("arbitrary", "arbitrary"),
                                             vmem_limit_bytes=VMEM_LIMIT),
        name="moe",
    )(te, n_used, src, dst, h2, wg_b, wu_b, wd_b)


def _final_kernel(mod_ref, x_ref, y_ref, mods_ref, g_ref, o_ref):
    del mod_ref
    x = x_ref[...] + mods_ref[MOD_G2:MOD_G2 + 1, :] * y_ref[...]
    o_ref[...] = _rms(x) * g_ref[...]


def final_norm_apply(lay, x, ytok, mods, g):
    row_map = lambda i, mod: (i, 0)
    return pl.pallas_call(
        _final_kernel,
        grid_spec=pltpu.PrefetchScalarGridSpec(
            num_scalar_prefetch=1, grid=(lay.n_blk,),
            in_specs=[
                pl.BlockSpec((TB, D_MODEL), row_map),
                pl.BlockSpec((TB, D_MODEL), row_map),
                pl.BlockSpec((None, 6, D_MODEL), lambda i, mod: (mod[i], 0, 0)),
                pl.BlockSpec((1, D_MODEL), lambda i, mod: (0, 0)),
            ],
            out_specs=pl.BlockSpec((TB, D_MODEL), row_map)),
        out_shape=jax.ShapeDtypeStruct((lay.m, D_MODEL), f32),
        compiler_params=pltpu.CompilerParams(dimension_semantics=("arbitrary",)),
        name="final_norm",
    )(jnp.asarray(lay.mod), x, ytok, mods, g)


def _reorder_w_in(w):
    g_qkv, g_z, g_a, g_b, s_z, s_xbc, s_dt, m_q, m_kv = jnp.split(w, np.cumsum(IN_SIZES)[:-1], axis=1)
    ckv, kpe = m_kv[:, :KV_RANK], m_kv[:, KV_RANK:]
    z = lambda n: jnp.zeros((w.shape[0], n), w.dtype)
    small = jnp.concatenate([g_a, g_b, s_dt, z(SM_KPE - SM_DT - 2 * SSM_HEADS), kpe, z(LANES - SM_KPE - MLA_ROPE)], 1)
    return jnp.concatenate([g_qkv, g_z, s_z, m_q, ckv, s_xbc, small], 1).astype(bf16)


def _reorder_w_uq(w):
    r = w.reshape(Q_RANK, MLA_HEADS, MLA_QK)
    return jnp.pad(r, ((0, 0), (0, 0), (0, LANES - MLA_QK))).reshape(Q_RANK, Q_HEADS_W).astype(bf16)


def _reorder_w_ukv(w):
    r = w.reshape(KV_RANK, MLA_HEADS, MLA_NOPE + MLA_VDIM)
    k = jnp.pad(r[:, :, :MLA_NOPE], ((0, 0), (0, 0), (0, LANES - MLA_NOPE))).reshape(KV_RANK, MLA_HEADS * LANES)
    v = jnp.pad(r[:, :, MLA_NOPE:], ((0, 0), (0, 0), (0, LANES - MLA_VDIM))).reshape(KV_RANK, MLA_HEADS * LANES)
    return jnp.concatenate([k, v], 1).astype(bf16)


def _lane_row(vals, lane0):
    n = vals.shape[-1]
    return jnp.pad(vals, ((0, 0), (lane0, LANES - lane0 - n)))


def _par_rows(row0, row1):
    z = jnp.zeros_like(row0)
    return jnp.stack([row0, row1] + [z] * (SUBLANES - 2), axis=1)


def kernel(x_prompt, x_sample, cache_ckv, cache_kpe, state_gdn, state_ssm, c, c_ctx, w_ada, b_ada, norm1, norm2, w_in, gdn_conv, gdn_A_log, gdn_dt_bias, gdn_norm, ssm_conv, ssm_conv_bias, ssm_A_log, ssm_dt_bias, ssm_D, ssm_norm, mla_q_norm, mla_w_uq, mla_kv_norm, mla_w_ukv, w_out, router_group, router_group_bias, router_expert, router_expert_bias, moe_w_gate, moe_w_up, moe_w_down, final_norm):
    n_ctx, t_ctx, _ = x_prompt.shape
    n_lat, t_lat, _ = x_sample.shape
    past = cache_ckv.shape[2]
    depth = w_in.shape[0]
    assert n_lat < SUBLANES and t_ctx % TB == 0 and t_lat % TB == 0 and past % TB == 0
    lay = Layout(n_ctx, t_ctx, n_lat, t_lat)
    m_ctx = lay.m_ctx

    x = jnp.concatenate([x_prompt.reshape(m_ctx, D_MODEL), x_sample.reshape(lay.m_lat, D_MODEL)], 0)
    cvec = jnp.concatenate([c_ctx[None], c, jnp.zeros((SUBLANES - 1 - n_lat, D_MODEL), f32)], 0)
    tables = rope_tables(lay)

    b_ada3 = b_ada[:, None, :]
    norm1_3, norm2_3 = norm1[:, None, :], norm2[:, None, :]
    gdn_par = _par_rows(_lane_row(gdn_A_log.reshape(depth, -1), SM_GA), _lane_row(gdn_dt_bias.reshape(depth, -1), SM_GA))
    ssm_par = _par_rows(_lane_row(ssm_A_log.reshape(depth, -1), SM_DT), _lane_row(ssm_dt_bias.reshape(depth, -1), SM_DT))
    d_lanes = jnp.pad(jnp.repeat(ssm_D, SSM_HEADDIM, axis=-1), ((0, 0), (0, SSM_XBC - SSM_INNER)))
    ssm_cpar = _par_rows(ssm_conv_bias, d_lanes)
    w_rt = jnp.pad(jnp.concatenate([router_group, router_expert], -1),
                   ((0, 0), (0, 0), (0, LANES - N_EGROUPS - N_EXPERTS)))
    b_rt = jnp.pad(jnp.concatenate([router_group_bias, router_expert_bias], -1),
                   ((0, 0), (0, LANES - N_EGROUPS - N_EXPERTS)))[:, None, :]
    kpe_cache = jnp.pad(cache_kpe, ((0, 0), (0, 0), (0, 0), (SM_KPE, LANES - SM_KPE - MLA_ROPE)))

    ckvs, kpes, gdns, ssms = [], [], [], []
    ytok = mods_prev = None
    for l in range(depth):
        mods = ada_mod(cvec, w_ada, b_ada3, l)
        res = None if l == 0 else (ytok, mods_prev)
        proj, x = in_proj(lay, x, mods, norm1_3, _reorder_w_in(w_in[l]), l, res)

        zg = jnp.zeros((n_ctx, GDN_HEADS, GDN_DK, GDN_DV), f32)
        o_f, sg_f = gdn_scan(lay, proj, gdn_conv, gdn_par, jnp.concatenate([zg, state_gdn[:, l, 0]], 0), l, False)
        o_b, sg_b = gdn_scan(lay, proj, gdn_conv, gdn_par, jnp.concatenate([zg, state_gdn[:, l, 1]], 0), l, True)

        zs = jnp.zeros((n_ctx, SSM_INNER, LANES), f32)
        y_f, hs_f = ssd_scan(lay, proj, ssm_conv, ssm_cpar, ssm_par,
                             jnp.concatenate([zs, ssm_state_to_lanes(state_ssm[:, l, 0])], 0), l, False)
        y_b, hs_b = ssd_scan(lay, proj, ssm_conv, ssm_cpar, ssm_par,
                             jnp.concatenate([zs, ssm_state_to_lanes(state_ssm[:, l, 1])], 0), l, True)

        q, ckvn, kpe = mla_prep(lay, proj, mla_q_norm[:, None, :], mla_kv_norm[:, None, :],
                                _reorder_w_uq(mla_w_uq[l]), tables, l)
        w_ukv_r = _reorder_w_ukv(mla_w_ukv[l])
        k_ctx, v_ctx = kv_up(ckvn[:m_ctx], kpe[:m_ctx], w_ukv_r)
        ckv_lat = jnp.concatenate([cache_ckv[:, l], ckvn[m_ctx:].reshape(n_lat, t_lat, KV_RANK)], 1)
        kpe_lat = jnp.concatenate([kpe_cache[:, l], kpe[m_ctx:].reshape(n_lat, t_lat, LANES)], 1)
        k_lat, v_lat = kv_up(ckv_lat.reshape(-1, KV_RANK), kpe_lat.reshape(-1, LANES), w_ukv_r)
        o_mla = jnp.concatenate([
            attention(q, k_ctx, v_ctx, 0, n_ctx, t_ctx, t_ctx),
            attention(q, k_lat, v_lat, m_ctx // TB, n_lat, t_lat, past + t_lat)], 0)

        x, h2, route = out_proj(lay, o_f, o_b, proj, y_f, y_b, o_mla, x, mods, gdn_norm[:, None, :],
                                ssm_norm[:, None, :], norm2_3, w_out[l].astype(bf16), w_rt, b_rt, l)
        ytok = moe_apply(lay, h2, route, moe_w_gate[l].astype(bf16), moe_w_up[l].astype(bf16),
                         moe_w_down[l].astype(bf16))
        mods_prev = mods

        ckvs.append(ckvn[:m_ctx].reshape(n_ctx, t_ctx, KV_RANK))
        kpes.append(kpe[:m_ctx, SM_KPE:SM_KPE + MLA_ROPE].reshape(n_ctx, t_ctx, MLA_ROPE))
        gdns.append(jnp.stack([sg_f[:n_ctx], sg_b[:n_ctx]], 1))
        ssms.append(jnp.stack([ssm_state_from_lanes(hs_f[:n_ctx]), ssm_state_from_lanes(hs_b[:n_ctx])], 1))

    y = final_norm_apply(lay, x, ytok, mods_prev, final_norm[None, :])
    return (y[:m_ctx].reshape(n_ctx, t_ctx, D_MODEL), y[m_ctx:].reshape(n_lat, t_lat, D_MODEL),
            jnp.stack(ckvs, 1), jnp.stack(kpes, 1), jnp.stack(gdns, 1), jnp.stack(ssms, 1))
```

```python
import functools
import math

import numpy as np
import jax
import jax.numpy as jnp
from jax import lax
from jax.experimental import pallas as pl
from jax.experimental.pallas import tpu as pltpu

f32 = jnp.float32
bf16 = jnp.bfloat16

D_MODEL = 2048
DEPTH = 2
GRID_W = 64
EPS = 1e-6
CONV_W = 5
CHUNK = 64

GDN_HEADS = 4
GDN_DK = 128
GDN_DV = 128
GDN_W = GDN_HEADS * GDN_DV
GDN_QKV = 2 * GDN_HEADS * GDN_DK + GDN_HEADS * GDN_DV

SSM_HEADS = 16
SSM_HEADDIM = 64
SSM_GROUPS = 2
SSM_DSTATE = 64
SSM_INNER = SSM_HEADS * SSM_HEADDIM
SSM_XBC = SSM_INNER + 2 * SSM_GROUPS * SSM_DSTATE

MLA_HEADS = 8
MLA_NOPE = 64
MLA_ROPE = 32
MLA_VDIM = 64
Q_RANK = 512
KV_RANK = 256
MLA_W = MLA_HEADS * MLA_VDIM
ROPE_THETA = 10000.0

MIX_W = GDN_W + SSM_INNER + MLA_W
IN_SIZES = (GDN_QKV, GDN_W, 2 * GDN_HEADS, 2 * GDN_HEADS, SSM_INNER, SSM_XBC, 2 * SSM_HEADS, Q_RANK, KV_RANK + MLA_ROPE)

N_EGROUPS = 4
E_PER_GROUP = 4
N_EXPERTS = N_EGROUPS * E_PER_GROUP
EXPERT_FF = 512

LANES = 128
SUBLANES = 8

TB = 256

COL_QKV = 0
COL_GZ = 1536
COL_SZ = 2048
COL_MQ = 3072
COL_CKV = 3584
COL_XBC = 3840
COL_SMALL = 5120
PROJ_W = 5248
SM_GA, SM_GB, SM_DT, SM_KPE = 0, 8, 16, 64


class Layout:
    def __init__(self, n_ctx, t_ctx, n_lat, t_lat):
        self.n_ctx, self.t_ctx, self.n_lat, self.t_lat = n_ctx, t_ctx, n_lat, t_lat
        self.m_ctx = n_ctx * t_ctx
        self.m_lat = n_lat * t_lat
        self.m = self.m_ctx + self.m_lat
        self.n_seq = n_ctx + n_lat
        bc, bl = t_ctx // TB, t_lat // TB
        seq, first, last, mod = [], [], [], []
        for s in range(n_ctx):
            for j in range(bc):
                seq.append(s); first.append(int(j == 0)); last.append(int(j == bc - 1)); mod.append(0)
        for s in range(n_lat):
            for j in range(bl):
                seq.append(n_ctx + s); first.append(int(j == 0)); last.append(int(j == bl - 1)); mod.append(1 + s)
        self.n_blk = len(seq)
        self.seq = np.array(seq, np.int32)
        self.first = np.array(first, np.int32)
        self.last = np.array(last, np.int32)
        self.mod = np.array(mod + mod[-1:], np.int32)
        fwd = np.arange(self.n_blk, dtype=np.int32)
        bwd = []
        i = 0
        while i < self.n_blk:
            j = i
            while self.last[j] == 0:
                j += 1
            bwd.extend(range(j, i - 1, -1))
            i = j + 1
        self.order = {False: fwd, True: np.array(bwd, np.int32)}

    def scan_tables(self, rev):
        order = self.order[rev]
        pad = lambda a: jnp.asarray(np.concatenate([a, a[-1:]]))
        return pad(order), pad(self.seq[order]), pad(self.first[order]), pad(self.last[order])


def _mm(a, b):
    return jnp.dot(a.astype(bf16), b.astype(bf16), preferred_element_type=f32)


def _mm_nt(a, b):
    return lax.dot_general(a.astype(bf16), b.astype(bf16), (((1,), (1,)), ((), ())), preferred_element_type=f32)


def _mm_tn(a, b):
    n = a.shape[1]
    eye = jnp.where(lax.broadcasted_iota(jnp.int32, (n, n), 0) == lax.broadcasted_iota(jnp.int32, (n, n), 1),
                    1.0, 0.0).astype(bf16)
    at = lax.dot_general(eye, a.astype(bf16), (((1,), (1,)), ((), ())), preferred_element_type=f32)
    return jnp.dot(at.astype(bf16), b.astype(bf16), preferred_element_type=f32)


def _split3(x):
    hi = x.astype(bf16)
    r = x - hi.astype(f32)
    mid = r.astype(bf16)
    lo = (r - mid.astype(f32)).astype(bf16)
    return hi, mid, lo


def _mm01(m01, x):
    hi, mid, lo = _split3(x)
    d = lambda t: jnp.dot(m01, t, preferred_element_type=f32)
    return d(hi) + d(mid) + d(lo)


def _mm3(a, b):
    ah = a.astype(bf16)
    al = (a - ah.astype(f32)).astype(bf16)
    bh = b.astype(bf16)
    bl = (b - bh.astype(f32)).astype(bf16)
    d = lambda x, y: jnp.dot(x, y, preferred_element_type=f32)
    return d(ah, bh) + d(ah, bl) + d(al, bh)


def _sigmoid(x):
    return 1.0 / (1.0 + jnp.exp(-x))


def _silu(x):
    return x * _sigmoid(x)


def _softplus(x):
    return jnp.maximum(x, 0.0) + jnp.log(1.0 + jnp.exp(-jnp.abs(x)))


def _chunk_masks(rev):
    i = lax.broadcasted_iota(jnp.int32, (CHUNK, CHUNK), 0)
    j = lax.broadcasted_iota(jnp.int32, (CHUNK, CHUNK), 1)
    if rev:
        incl, strict = j >= i, j > i
    else:
        incl, strict = j <= i, j < i
    ll = jnp.where(incl, 1.0, 0.0).astype(bf16)
    lls = jnp.where(incl, 0.0, 1.0).astype(bf16)
    uu = jnp.where(strict, 1.0, 0.0)
    return incl, strict, ll, lls, uu


def _fill_ext(ext_ref, main_ref, prev_ref, next_ref, sfirst, slast):
    ext_ref[0:SUBLANES, :] = jnp.where(sfirst == 1, 0.0, prev_ref[...])
    ext_ref[SUBLANES:SUBLANES + TB, :] = main_ref[...]
    ext_ref[SUBLANES + TB:2 * SUBLANES + TB, :] = jnp.where(slast == 1, 0.0, next_ref[...])


def _conv_tile(ext_ref, w_ref, r0, col0, width, bias=None):
    acc = None
    for t in range(CONV_W):
        rows = ext_ref[pl.ds(SUBLANES + r0 - CONV_W // 2 + t, CHUNK), col0:col0 + width]
        term = rows * w_ref[t:t + 1, col0:col0 + width]
        acc = term if acc is None else acc + term
    if bias is not None:
        acc = acc + bias
    return _silu(acc)


INV_BLOCK = 16


def _block_masks():
    i = lax.broadcasted_iota(jnp.int32, (CHUNK, CHUNK), 0)
    j = lax.broadcasted_iota(jnp.int32, (CHUNK, CHUNK), 1)
    blk16 = (i // INV_BLOCK) == (j // INV_BLOCK)
    blk32 = (i // (2 * INV_BLOCK)) == (j // (2 * INV_BLOCK))
    return blk16, blk32


def _unit_tri_inverse_many(lmats, eye, blk16, blk32):
    ps = list(lmats)
    ld = {p: jnp.where(blk16, lmats[p], 0.0) for p in ps}
    t = {p: eye - ld[p] for p in ps}
    pw = {p: _mm3(ld[p], ld[p]) for p in ps}
    for it in range(3):
        t = {p: t[p] + _mm3(t[p], pw[p]) for p in ps}
        if it < 2:
            pw = {p: _mm3(pw[p], pw[p]) for p in ps}
    in32 = jnp.logical_and(blk32, jnp.logical_not(blk16))
    m = {p: _mm3(t[p], jnp.where(in32, lmats[p], 0.0)) for p in ps}
    t = {p: t[p] - _mm3(m[p], t[p]) for p in ps}
    m = {p: _mm3(t[p], jnp.where(blk32, 0.0, lmats[p])) for p in ps}
    return {p: t[p] - _mm3(m[p], t[p]) for p in ps}


def _gdn_kernel(order_ref, seq_ref, first_ref, last_ref,
                main_ref, prev_ref, next_ref, small_ref, convw_ref, par_ref, s0_ref,
                o_ref, sfin_ref, s_ref, ext_ref, *, rev):
    del order_ref, seq_ref
    step = pl.program_id(0)
    sfirst = first_ref[step]
    slast = last_ref[step]
    pfirst, plast = (slast, sfirst) if rev else (sfirst, slast)
    d = 1 if rev else 0

    @pl.when(pfirst == 1)
    def _():
        s_ref[...] = s0_ref[...]

    _fill_ext(ext_ref, main_ref, prev_ref, next_ref, sfirst, slast)

    incl, strict, ll, lls, uu = _chunk_masks(rev)
    eye = jnp.where(lax.broadcasted_iota(jnp.int32, (CHUNK, CHUNK), 0)
                    == lax.broadcasted_iota(jnp.int32, (CHUNK, CHUNK), 1), 1.0, 0.0)
    blk16, blk32 = _block_masks()
    neg_a = -jnp.exp(par_ref[0:1, :])
    dt_bias = par_ref[1:2, :]
    n_chunks = TB // CHUNK

    chunk_order = [n_chunks - 1 - cc if rev else cc for cc in range(n_chunks)]
    heads = range(GDN_HEADS)
    eye_k = jnp.where(lax.broadcasted_iota(jnp.int32, (GDN_DK, GDN_DK), 0)
                      == lax.broadcasted_iota(jnp.int32, (GDN_DK, GDN_DK), 1), 1.0, 0.0).astype(bf16)

    gates = {}
    for c in chunk_order:
        sm = small_ref[c * CHUNK:(c + 1) * CHUNK, :]
        la_blk = neg_a * _softplus(sm + dt_bias)
        gates[c] = (la_blk, _sigmoid(sm))
    g_blk = {c: _mm01(ll, gates[c][0]) for c in chunk_order}
    gr_blk = {c: _mm01(lls, gates[c][0]) for c in chunk_order}
    gl_blk = {c: jnp.sum(gates[c][0], axis=0, keepdims=True) for c in chunk_order}

    probs = [(c, h) for c in chunk_order for h in heads]
    ia = lambda h: SM_GA + d * GDN_HEADS + h
    ib = lambda h: SM_GB + d * GDN_HEADS + h
    col = lambda blk, lane: blk[:, lane:lane + 1]
    beta = {p: col(gates[p[0]][1], ib(p[1])) for p in probs}
    q, k, v = {}, {}, {}
    for p in probs:
        c, h = p
        r0 = c * CHUNK
        qq = _conv_tile(ext_ref, convw_ref, r0, h * GDN_DK, GDN_DK)
        kx = _conv_tile(ext_ref, convw_ref, r0, GDN_HEADS * GDN_DK + h * GDN_DK, GDN_DK)
        v[p] = _conv_tile(ext_ref, convw_ref, r0, 2 * GDN_HEADS * GDN_DK + h * GDN_DV, GDN_DV)
        q[p] = qq * lax.rsqrt(jnp.sum(qq * qq, axis=-1, keepdims=True) + EPS) * (GDN_DK ** -0.5)
        k[p] = kx * lax.rsqrt(jnp.sum(kx * kx, axis=-1, keepdims=True) + EPS)
    dmat = {p: _mm01(ll, col(gates[p[0]][0], ia(p[1])) * uu) for p in probs}
    kk = {p: _mm_nt(k[p], k[p]) for p in probs}
    qk = {p: _mm_nt(q[p], k[p]) for p in probs}
    decay = {p: jnp.where(incl, jnp.exp(dmat[p]), 0.0) for p in probs}
    lmat = {p: jnp.where(strict, kk[p] * beta[p] * decay[p], 0.0) for p in probs}
    qk = {p: jnp.where(incl, qk[p] * decay[p], 0.0) for p in probs}
    tinv = _unit_tri_inverse_many(lmat, eye, blk16, blk32)
    uw = {}
    for p in probs:
        e_g = jnp.exp(col(g_blk[p[0]], ia(p[1])))
        rhs = jnp.concatenate([v[p] * beta[p], k[p] * beta[p] * e_g], axis=1)
        uw[p] = _mm3(tinv[p], rhs)
        q[p] = q[p] * e_g
    kdt = {}
    for p in probs:
        k_dec = (k[p] * jnp.exp(col(gr_blk[p[0]], ia(p[1])))).astype(bf16)
        kdt[p] = lax.dot_general(eye_k, k_dec, (((1,), (1,)), ((), ())), preferred_element_type=f32)

    for c in chunk_order:
        ps = [(c, h) for h in heads]
        s_old = {p: s_ref[p[1]] for p in ps}
        ws = {p: _mm(uw[p][:, GDN_DV:], s_old[p]) for p in ps}
        qs = {p: _mm(q[p], s_old[p]) for p in ps}
        v_new = {p: uw[p][:, :GDN_DV] - ws[p] for p in ps}
        o = {p: qs[p] + _mm(qk[p], v_new[p]) for p in ps}
        for p in ps:
            h = p[1]
            s_ref[h] = s_old[p] * jnp.exp(col(gl_blk[c], ia(h))) + _mm(kdt[p], v_new[p])
            o_ref[c * CHUNK:(c + 1) * CHUNK, h * GDN_DV:(h + 1) * GDN_DV] = o[p]

    @pl.when(plast == 1)
    def _():
        sfin_ref[...] = s_ref[...]


def _halo_maps(n_rows8):
    prev_map = lambda i, order, *_: (jnp.maximum(order[i] * (TB // SUBLANES) - 1, 0), 0)
    next_map = lambda i, order, *_: (jnp.minimum((order[i] + 1) * (TB // SUBLANES), n_rows8 - 1), 0)
    return prev_map, next_map


def gdn_scan(lay, proj, convw, par, s0, layer, rev):
    prev_map, next_map = _halo_maps(lay.m // SUBLANES)
    blk_map = lambda i, order, *_: (order[i], 0)
    seq_map = lambda i, order, seq, *_: (seq[i], 0, 0, 0)
    grid_spec = pltpu.PrefetchScalarGridSpec(
        num_scalar_prefetch=4,
        grid=(lay.n_blk,),
        in_specs=[
            pl.BlockSpec((TB, GDN_QKV), blk_map),
            pl.BlockSpec((SUBLANES, GDN_QKV), prev_map),
            pl.BlockSpec((SUBLANES, GDN_QKV), next_map),
            pl.BlockSpec((TB, LANES), lambda i, order, *_: (order[i], COL_SMALL // LANES)),
            pl.BlockSpec((None, CONV_W, GDN_QKV), lambda i, *_: (layer, 0, 0)),
            pl.BlockSpec((None, SUBLANES, LANES), lambda i, *_: (layer, 0, 0)),
            pl.BlockSpec((None, GDN_HEADS, GDN_DK, GDN_DV), seq_map),
        ],
        out_specs=[
            pl.BlockSpec((TB, GDN_W), blk_map),
            pl.BlockSpec((None, GDN_HEADS, GDN_DK, GDN_DV), seq_map),
        ],
        scratch_shapes=[
            pltpu.VMEM((GDN_HEADS, GDN_DK, GDN_DV), f32),
            pltpu.VMEM((TB + 2 * SUBLANES, GDN_QKV), f32),
        ],
    )
    return pl.pallas_call(
        functools.partial(_gdn_kernel, rev=rev),
        grid_spec=grid_spec,
        out_shape=[
            jax.ShapeDtypeStruct((lay.m, GDN_W), f32),
            jax.ShapeDtypeStruct((lay.n_seq, GDN_HEADS, GDN_DK, GDN_DV), f32),
        ],
        compiler_params=pltpu.CompilerParams(dimension_semantics=("arbitrary",)),
        name="gdn_bwd" if rev else "gdn_fwd",
    )(*lay.scan_tables(rev), proj, proj, proj, proj, convw, par, s0)


SSM_PAIRS = SSM_HEADS // 2
PAIRS_PER_GROUP = SSM_PAIRS // SSM_GROUPS
COL_B = SSM_INNER
COL_C = SSM_INNER + SSM_GROUPS * SSM_DSTATE


def _ssd_kernel(order_ref, seq_ref, first_ref, last_ref,
                main_ref, prev_ref, next_ref, small_ref, convw_ref, cpar_ref, par_ref, h0_ref,
                y_ref, hfin_ref, h_ref, ext_ref, *, rev):
    del order_ref, seq_ref
    step = pl.program_id(0)
    sfirst = first_ref[step]
    slast = last_ref[step]
    pfirst, plast = (slast, sfirst) if rev else (sfirst, slast)
    d = 1 if rev else 0

    @pl.when(pfirst == 1)
    def _():
        h_ref[...] = h0_ref[...]

    _fill_ext(ext_ref, main_ref, prev_ref, next_ref, sfirst, slast)

    incl, _, ll, lls, uu = _chunk_masks(rev)
    lane_lo = lax.broadcasted_iota(jnp.int32, (CHUNK, LANES), 1) < SSM_DSTATE
    row_lo = lax.broadcasted_iota(jnp.int32, (LANES, LANES), 0) < SSM_HEADDIM
    eye_p = jnp.where(lax.broadcasted_iota(jnp.int32, (LANES, LANES), 0)
                      == lax.broadcasted_iota(jnp.int32, (LANES, LANES), 1), 1.0, 0.0).astype(bf16)
    neg_a = -jnp.exp(par_ref[0:1, :])
    dt_bias = par_ref[1:2, :]
    n_chunks = TB // CHUNK

    for cc in range(n_chunks):
        c = n_chunks - 1 - cc if rev else cc
        r0 = c * CHUNK
        sm = small_ref[r0:r0 + CHUNK, :]
        dt_blk = _softplus(sm + dt_bias)
        dta_blk = dt_blk * neg_a
        acum_blk = _mm01(ll, dta_blk)
        ar_blk = _mm01(lls, dta_blk)
        al_blk = jnp.sum(dta_blk, axis=0, keepdims=True)
        b_pair = _conv_tile(ext_ref, convw_ref, r0, COL_B, LANES, cpar_ref[0:1, COL_B:COL_B + LANES])
        c_pair = _conv_tile(ext_ref, convw_ref, r0, COL_C, LANES, cpar_ref[0:1, COL_C:COL_C + LANES])
        c_g, b_g, cb = [], [], []
        for g in range(SSM_GROUPS):
            gmask = lane_lo if g == 0 else jnp.logical_not(lane_lo)
            c_g.append(jnp.where(gmask, c_pair, 0.0))
            b_g.append(jnp.where(gmask, b_pair, 0.0))
        cb = [_mm_nt(c_g[g], b_pair) for g in range(SSM_GROUPS)]
        pairs = range(SSM_PAIRS)
        grp = lambda p: p // PAIRS_PER_GROUP
        lane_of = lambda hh: SM_DT + d * SSM_HEADS + hh
        colv = lambda blk, hh: blk[:, lane_of(hh):lane_of(hh) + 1]
        both = lambda blk, p: jnp.where(lane_lo, colv(blk, 2 * p), colv(blk, 2 * p + 1))
        xs = [_conv_tile(ext_ref, convw_ref, r0, p * LANES, LANES, cpar_ref[0:1, p * LANES:(p + 1) * LANES])
              for p in pairs]
        dmat = [_mm01(ll, colv(dta_blk, hh) * uu) for hh in range(SSM_HEADS)]
        m_h = [cb[grp(hh // 2)] * jnp.where(incl, jnp.exp(dmat[hh]), 0.0) for hh in range(SSM_HEADS)]
        xdt = [xs[p] * both(dt_blk, p) for p in pairs]
        y = [jnp.where(lane_lo, _mm(m_h[2 * p], xdt[p]), _mm(m_h[2 * p + 1], xdt[p])) for p in pairs]
        xdt_t = [lax.dot_general(eye_p, xdt[p].astype(bf16), (((1,), (1,)), ((), ())),
                                 preferred_element_type=f32).astype(bf16) for p in pairs]
        st_new = [jnp.where(row_lo,
                            jnp.dot(xdt_t[p], (b_g[grp(p)] * jnp.exp(colv(ar_blk, 2 * p))).astype(bf16),
                                    preferred_element_type=f32),
                            jnp.dot(xdt_t[p], (b_g[grp(p)] * jnp.exp(colv(ar_blk, 2 * p + 1))).astype(bf16),
                                    preferred_element_type=f32)) for p in pairs]
        h_pair = [h_ref[p * LANES:(p + 1) * LANES, :] for p in pairs]
        y_off = [_mm_nt(c_g[grp(p)], h_pair[p]) for p in pairs]
        for p in pairs:
            col = p * LANES
            yp = y[p] + y_off[p] * jnp.exp(both(acum_blk, p))
            if not rev:
                yp = yp + xs[p] * cpar_ref[1:2, col:col + LANES]
            dec = jnp.where(row_lo, jnp.exp(colv(al_blk, 2 * p)), jnp.exp(colv(al_blk, 2 * p + 1)))
            h_ref[col:col + LANES, :] = h_pair[p] * dec + st_new[p]
            y_ref[r0:r0 + CHUNK, col:col + LANES] = yp

    @pl.when(plast == 1)
    def _():
        hfin_ref[...] = h_ref[...]


def ssd_scan(lay, proj, convw, cpar, par, h0, layer, rev):
    prev_map, next_map = _halo_maps(lay.m // SUBLANES)
    cblk = COL_XBC // SSM_XBC
    blk_map = lambda i, order, *_: (order[i], 0)
    seq_map = lambda i, order, seq, *_: (seq[i], 0, 0)
    grid_spec = pltpu.PrefetchScalarGridSpec(
        num_scalar_prefetch=4,
        grid=(lay.n_blk,),
        in_specs=[
            pl.BlockSpec((TB, SSM_XBC), lambda i, order, *_: (order[i], cblk)),
            pl.BlockSpec((SUBLANES, SSM_XBC), lambda i, *a: (prev_map(i, *a)[0], cblk)),
            pl.BlockSpec((SUBLANES, SSM_XBC), lambda i, *a: (next_map(i, *a)[0], cblk)),
            pl.BlockSpec((TB, LANES), lambda i, order, *_: (order[i], COL_SMALL // LANES)),
            pl.BlockSpec((None, CONV_W, SSM_XBC), lambda i, *_: (layer, 0, 0)),
            pl.BlockSpec((None, SUBLANES, SSM_XBC), lambda i, *_: (layer, 0, 0)),
            pl.BlockSpec((None, SUBLANES, LANES), lambda i, *_: (layer, 0, 0)),
            pl.BlockSpec((None, SSM_INNER, LANES), seq_map),
        ],
        out_specs=[
            pl.BlockSpec((TB, SSM_INNER), blk_map),
            pl.BlockSpec((None, SSM_INNER, LANES), seq_map),
        ],
        scratch_shapes=[
            pltpu.VMEM((SSM_INNER, LANES), f32),
            pltpu.VMEM((TB + 2 * SUBLANES, SSM_XBC), f32),
        ],
    )
    return pl.pallas_call(
        functools.partial(_ssd_kernel, rev=rev),
        grid_spec=grid_spec,
        out_shape=[
            jax.ShapeDtypeStruct((lay.m, SSM_INNER), f32),
            jax.ShapeDtypeStruct((lay.n_seq, SSM_INNER, LANES), f32),
        ],
        compiler_params=pltpu.CompilerParams(dimension_semantics=("arbitrary",)),
        name="ssd_bwd" if rev else "ssd_fwd",
    )(*lay.scan_tables(rev), proj, proj, proj, proj, convw, cpar, par, h0)


def ssm_state_to_lanes(h):
    n = h.shape[0]
    hg = h.reshape(n, SSM_GROUPS, SSM_HEADS // SSM_GROUPS * SSM_HEADDIM, SSM_DSTATE)
    z = jnp.zeros_like(hg[:, 0])
    return jnp.concatenate([jnp.concatenate([hg[:, 0], z], -1), jnp.concatenate([z, hg[:, 1]], -1)], 1)


def ssm_state_from_lanes(hl):
    n = hl.shape[0]
    r = hl.reshape(n, SSM_GROUPS, SSM_INNER // SSM_GROUPS, SSM_GROUPS, SSM_DSTATE)
    h = jnp.stack([r[:, g, :, g, :] for g in range(SSM_GROUPS)], 1)
    return h.reshape(n, SSM_HEADS, SSM_HEADDIM, SSM_DSTATE)


ADA_TN = 1536
VMEM_LIMIT = 56 * 1024 * 1024


def _ada_kernel(c_ref, w_ref, b_ref, o_ref):
    s = _silu(c_ref[...])
    o_ref[...] = jnp.dot(s.astype(bf16), w_ref[...].astype(bf16), preferred_element_type=f32) + b_ref[...]


def ada_mod(cvec, w_ada, b_ada, layer):
    n = 6 * D_MODEL
    out = pl.pallas_call(
        _ada_kernel,
        grid=(n // ADA_TN,),
        in_specs=[
            pl.BlockSpec((SUBLANES, D_MODEL), lambda j: (0, 0)),
            pl.BlockSpec((None, D_MODEL, ADA_TN), lambda j: (layer, 0, j)),
            pl.BlockSpec((None, 1, ADA_TN), lambda j: (layer, 0, j)),
        ],
        out_specs=pl.BlockSpec((SUBLANES, ADA_TN), lambda j: (0, j)),
        out_shape=jax.ShapeDtypeStruct((SUBLANES, n), f32),
        compiler_params=pltpu.CompilerParams(dimension_semantics=("arbitrary",), vmem_limit_bytes=VMEM_LIMIT),
        name="ada_mod",
    )(cvec, w_ada, b_ada)
    return out.reshape(SUBLANES, 6, D_MODEL)


MOD_SH1, MOD_SC1, MOD_G1, MOD_SH2, MOD_SC2, MOD_G2 = range(6)


def _rms(x):
    return x * lax.rsqrt(jnp.mean(x * x, axis=-1, keepdims=True) + EPS)


PROJ_CHUNK = 512


def _in_proj_kernel(mod_ref, *refs, has_res):
    del mod_ref
    if has_res:
        x_ref, y_ref, pmods_ref, mods_ref, n1_ref, w_ref, proj_ref, xo_ref = refs
        x = x_ref[...] + pmods_ref[MOD_G2:MOD_G2 + 1, :] * y_ref[...]
        xo_ref[...] = x
    else:
        x_ref, mods_ref, n1_ref, w_ref, proj_ref = refs
        x = x_ref[...]
    h = _rms(x) * n1_ref[...]
    h = h * (1.0 + mods_ref[MOD_SC1:MOD_SC1 + 1, :]) + mods_ref[MOD_SH1:MOD_SH1 + 1, :]
    hb = h.astype(bf16)
    for c0 in range(0, PROJ_W, PROJ_CHUNK):
        wd = min(PROJ_CHUNK, PROJ_W - c0)
        proj_ref[:, c0:c0 + wd] = jnp.dot(hb, w_ref[:, c0:c0 + wd], preferred_element_type=f32)


def in_proj(lay, x, mods, norm1, w_in_r, layer, res=None):
    row_map = lambda i, mod: (i, 0)
    mod_map = lambda i, mod: (mod[i], 0, 0)
    in_specs = [pl.BlockSpec((TB, D_MODEL), row_map)]
    args = [x]
    if res is not None:
        in_specs += [pl.BlockSpec((TB, D_MODEL), row_map), pl.BlockSpec((None, 6, D_MODEL), mod_map)]
        args += [res[0], res[1]]
    in_specs += [
        pl.BlockSpec((None, 6, D_MODEL), mod_map),
        pl.BlockSpec((None, 1, D_MODEL), lambda i, mod: (layer, 0, 0)),
        pl.BlockSpec((D_MODEL, PROJ_W), lambda i, mod: (0, 0), pipeline_mode=pl.Buffered(1)),
    ]
    args += [mods, norm1, w_in_r]
    out_specs = [pl.BlockSpec((TB, PROJ_W), row_map)]
    out_shape = [jax.ShapeDtypeStruct((lay.m, PROJ_W), f32)]
    if res is not None:
        out_specs.append(pl.BlockSpec((TB, D_MODEL), row_map))
        out_shape.append(jax.ShapeDtypeStruct((lay.m, D_MODEL), f32))
    outs = pl.pallas_call(
        functools.partial(_in_proj_kernel, has_res=res is not None),
        grid_spec=pltpu.PrefetchScalarGridSpec(
            num_scalar_prefetch=1, grid=(lay.n_blk,), in_specs=in_specs, out_specs=out_specs),
        out_shape=out_shape,
        compiler_params=pltpu.CompilerParams(dimension_semantics=("arbitrary",), vmem_limit_bytes=VMEM_LIMIT),
        name="in_proj",
    )(jnp.asarray(lay.mod), *args)
    return (outs[0], outs[1]) if res is not None else (outs[0], x)


MLA_QK = MLA_NOPE + MLA_ROPE
Q_HEADS_W = MLA_HEADS * LANES
ROPE_HALF = MLA_ROPE // 2


def _rope_lanes(x, c, s1, s2):
    return x * c + pltpu.roll(x, ROPE_HALF, 1) * s1 + pltpu.roll(x, LANES - ROPE_HALF, 1) * s2


def _mla_prep_kernel(pos_ref, mq_ref, ckv_ref, small_ref, qn_ref, kvn_ref, wuq_ref, rc_ref, rs1_ref, rs2_ref,
                     q_ref, ckvn_ref, kpe_ref):
    del pos_ref
    c, s1, s2 = rc_ref[...], rs1_ref[...], rs2_ref[...]
    cq = (_rms(mq_ref[...]) * qn_ref[...]).astype(bf16)
    qf = jnp.dot(cq, wuq_ref[...], preferred_element_type=f32)
    scale = MLA_QK ** -0.5 * math.log2(math.e)
    for h in range(MLA_HEADS):
        xh = qf[:, h * LANES:(h + 1) * LANES]
        q_ref[:, h * LANES:(h + 1) * LANES] = (_rope_lanes(xh, c, s1, s2) * scale).astype(bf16)
    ckvn_ref[...] = _rms(ckv_ref[...]) * kvn_ref[...]
    lane = lax.broadcasted_iota(jnp.int32, (TB, LANES), 1)
    in_rope = jnp.logical_and(lane >= SM_KPE, lane < SM_KPE + MLA_ROPE)
    kpe_ref[...] = jnp.where(in_rope, _rope_lanes(small_ref[...], c, s1, s2), 0.0)


def rope_tables(lay):
    t = lay.t_lat
    rows = t // GRID_W
    row = jnp.repeat(jnp.arange(rows, dtype=f32), GRID_W)
    col = jnp.tile(jnp.arange(GRID_W, dtype=f32), rows)
    nf = MLA_ROPE // 4
    inv = jnp.power(ROPE_THETA, -jnp.arange(nf, dtype=f32) / nf)
    ang = jnp.concatenate([row[:, None] * inv, col[:, None] * inv], axis=-1)
    cos, sin = jnp.cos(ang), jnp.sin(ang)
    z = lambda n: jnp.zeros((t, n), f32)
    c = jnp.concatenate([jnp.ones((t, MLA_NOPE), f32), cos, cos, z(LANES - MLA_QK)], -1)
    s1 = jnp.concatenate([z(MLA_NOPE + ROPE_HALF), sin, z(LANES - MLA_QK)], -1)
    s2 = jnp.concatenate([z(MLA_NOPE), -sin, z(ROPE_HALF + LANES - MLA_QK)], -1)
    ident = jnp.concatenate([jnp.ones((TB, MLA_QK), f32), jnp.zeros((TB, LANES - MLA_QK), f32)], -1)
    zero = jnp.zeros((TB, LANES), f32)
    return (jnp.concatenate([ident, c]), jnp.concatenate([zero, s1]), jnp.concatenate([zero, s2]))


def _pos_blocks(lay):
    pos = [0] * (lay.m_ctx // TB)
    for _ in range(lay.n_lat):
        pos += [1 + j for j in range(lay.t_lat // TB)]
    return np.array(pos + pos[-1:], np.int32)


def mla_prep(lay, proj, q_norm, kv_norm, w_uq_r, tables, layer):
    row_map = lambda i, pos: (i, 0)
    tab_map = lambda i, pos: (pos[i], 0)
    return pl.pallas_call(
        _mla_prep_kernel,
        grid_spec=pltpu.PrefetchScalarGridSpec(
            num_scalar_prefetch=1, grid=(lay.n_blk,),
            in_specs=[
                pl.BlockSpec((TB, Q_RANK), lambda i, pos: (i, COL_MQ // Q_RANK)),
                pl.BlockSpec((TB, KV_RANK), lambda i, pos: (i, COL_CKV // KV_RANK)),
                pl.BlockSpec((TB, LANES), lambda i, pos: (i, COL_SMALL // LANES)),
                pl.BlockSpec((None, 1, Q_RANK), lambda i, pos: (layer, 0, 0)),
                pl.BlockSpec((None, 1, KV_RANK), lambda i, pos: (layer, 0, 0)),
                pl.BlockSpec((Q_RANK, Q_HEADS_W), lambda i, pos: (0, 0)),
                pl.BlockSpec((TB, LANES), tab_map),
                pl.BlockSpec((TB, LANES), tab_map),
                pl.BlockSpec((TB, LANES), tab_map),
            ],
            out_specs=[
                pl.BlockSpec((TB, Q_HEADS_W), row_map),
                pl.BlockSpec((TB, KV_RANK), row_map),
                pl.BlockSpec((TB, LANES), row_map),
            ]),
        out_shape=[
            jax.ShapeDtypeStruct((lay.m, Q_HEADS_W), bf16),
            jax.ShapeDtypeStruct((lay.m, KV_RANK), f32),
            jax.ShapeDtypeStruct((lay.m, LANES), f32),
        ],
        compiler_params=pltpu.CompilerParams(dimension_semantics=("arbitrary",)),
        name="mla_prep",
    )(jnp.asarray(_pos_blocks(lay)), proj, proj, proj, q_norm, kv_norm, w_uq_r, *tables)


V_PAIRS = MLA_HEADS // 2
KV_UP_W = 2 * MLA_HEADS * LANES


def _kv_up_kernel(ckv_ref, kpe_ref, w_ref, k_ref, v_ref):
    kv = jnp.dot(ckv_ref[...].astype(bf16), w_ref[...], preferred_element_type=f32)
    kpe = kpe_ref[...]
    for h in range(MLA_HEADS):
        k_ref[h] = (kv[:, h * LANES:(h + 1) * LANES] + kpe).astype(bf16)
    v0 = MLA_HEADS * LANES
    lane_lo = lax.broadcasted_iota(jnp.int32, (TB, LANES), 1) < MLA_VDIM
    for h in range(MLA_HEADS):
        v_ref[h] = jnp.where(lane_lo, kv[:, v0 + h * LANES:v0 + (h + 1) * LANES], 1.0).astype(bf16)


def kv_up(ckv, kpe, w_ukv_r):
    r = ckv.shape[0]
    return pl.pallas_call(
        _kv_up_kernel,
        grid=(r // TB,),
        in_specs=[
            pl.BlockSpec((TB, KV_RANK), lambda i: (i, 0)),
            pl.BlockSpec((TB, LANES), lambda i: (i, 0)),
            pl.BlockSpec((KV_RANK, KV_UP_W), lambda i: (0, 0)),
        ],
        out_specs=[
            pl.BlockSpec((MLA_HEADS, TB, LANES), lambda i: (0, i, 0)),
            pl.BlockSpec((MLA_HEADS, TB, LANES), lambda i: (0, i, 0)),
        ],
        out_shape=[
            jax.ShapeDtypeStruct((MLA_HEADS, r, LANES), bf16),
            jax.ShapeDtypeStruct((MLA_HEADS, r, LANES), bf16),
        ],
        compiler_params=pltpu.CompilerParams(dimension_semantics=("arbitrary",)),
        name="kv_up",
    )(ckv, kpe, w_ukv_r)


ATT_KC = 512


def _attn_kernel(q_ref, k_ref, v_ref, o_ref, m_ref, acc_ref, *, n_keys):
    lane_lo = lax.broadcasted_iota(jnp.int32, (TB, LANES), 1) < MLA_VDIM
    kc = min(ATT_KC, n_keys)
    hs = range(MLA_HEADS)
    m_ref[...] = jnp.full((MLA_HEADS, TB, LANES), -jnp.inf, f32)
    acc_ref[...] = jnp.zeros((MLA_HEADS, TB, LANES), f32)

    def body(c, carry):
        c0 = pl.multiple_of(c * kc, kc)
        score = lambda h: lax.dot_general(q_ref[:, h * LANES:(h + 1) * LANES], k_ref[h, pl.ds(c0, kc), :],
                                          (((1,), (1,)), ((), ())), preferred_element_type=f32)
        s_next = score(0)
        for h in hs:
            s = s_next
            if h + 1 < MLA_HEADS:
                s_next = score(h + 1)
            m_old = m_ref[h]
            m_new = jnp.maximum(m_old, jnp.max(s, axis=-1, keepdims=True))
            alpha = jnp.exp2(m_old - m_new)
            pe = jnp.exp2(s - m_new[:, 0:1])
            pv = jnp.dot(pe.astype(bf16), v_ref[h, pl.ds(c0, kc), :], preferred_element_type=f32)
            acc_ref[h] = alpha * acc_ref[h] + pv
            m_ref[h] = m_new
        return carry

    lax.fori_loop(0, n_keys // kc, body, 0)
    half = LANES // 2
    for p in range(V_PAIRS):
        a_even = acc_ref[2 * p]
        a_odd = pltpu.roll(acc_ref[2 * p + 1], half, 1)
        o_even = a_even * pltpu.roll(1.0 / a_even, half, 1)
        o_odd = a_odd * pltpu.roll(1.0 / a_odd, half, 1)
        o_ref[:, p * LANES:(p + 1) * LANES] = jnp.where(lane_lo, o_even, o_odd)


def attention(q, k, v, q_blk0, n_seq, t_q, n_keys):
    qb = t_q // TB
    return pl.pallas_call(
        functools.partial(_attn_kernel, n_keys=n_keys),
        grid=(n_seq, qb),
        in_specs=[
            pl.BlockSpec((TB, Q_HEADS_W), lambda b, j: (q_blk0 + b * qb + j, 0)),
            pl.BlockSpec((MLA_HEADS, n_keys, LANES), lambda b, j: (0, b, 0)),
            pl.BlockSpec((MLA_HEADS, n_keys, LANES), lambda b, j: (0, b, 0)),
        ],
        out_specs=pl.BlockSpec((TB, MLA_W), lambda b, j: (b * qb + j, 0)),
        out_shape=jax.ShapeDtypeStruct((n_seq * t_q, MLA_W), f32),
        scratch_shapes=[pltpu.VMEM((MLA_HEADS, TB, LANES), f32)] * 2,
        compiler_params=pltpu.CompilerParams(dimension_semantics=("arbitrary", "arbitrary"),
                                             vmem_limit_bytes=VMEM_LIMIT),
        name="mla_attn",
    )(q, k, v)


N_PAIRS = E_PER_GROUP * (E_PER_GROUP - 1) // 2
N_BUCKETS = N_EGROUPS * N_PAIRS
RT_BUCKET, RT_WLO, RT_WHI = 0, 1, 2
H2X_W = D_MODEL + LANES


def _route(lg):
    lane = lax.broadcasted_iota(jnp.int32, lg.shape, 1)
    big = jnp.int32(LANES)
    rmax = lambda x: jnp.max(x, axis=-1, keepdims=True)
    rmin = lambda x: jnp.min(x, axis=-1, keepdims=True)
    rsum = lambda x: jnp.sum(x, axis=-1, keepdims=True)
    neg = -jnp.inf
    gmask = lane < N_EGROUPS
    gl = jnp.where(gmask, lg, neg)
    gmax = rmax(gl)
    gsum = rsum(jnp.where(gmask, jnp.exp(lg - gmax), 0.0))
    gsel = rmin(jnp.where(gl == gmax, lane, big))
    pg_sel = 1.0 / gsum
    e0 = N_EGROUPS + E_PER_GROUP * gsel
    emask = jnp.logical_and(lane >= e0, lane < e0 + E_PER_GROUP)
    el = jnp.where(emask, lg, neg)
    emax = rmax(el)
    ee = jnp.where(emask, jnp.exp(lg - emax), 0.0)
    pe = ee / rsum(ee)
    pe_m = jnp.where(emask, pe, -1.0)
    v1 = rmax(pe_m)
    i1 = rmin(jnp.where(pe_m == v1, lane, big))
    pe_m2 = jnp.where(lane == i1, -1.0, pe_m)
    v2 = rmax(pe_m2)
    i2 = rmin(jnp.where(pe_m2 == v2, lane, big))
    w1 = pg_sel * v1 / (v1 + v2)
    w2 = pg_sel * v2 / (v1 + v2)
    a1, a2 = i1 - e0, i2 - e0
    lo, hi = jnp.minimum(a1, a2), jnp.maximum(a1, a2)
    pair = (lo * (2 * E_PER_GROUP - 1 - lo)) // 2 + (hi - lo - 1)
    bucket = (gsel * N_PAIRS + pair).astype(f32)
    w_lo = jnp.where(a1 < a2, w1, w2)
    w_hi = jnp.where(a1 < a2, w2, w1)
    return jnp.where(lane == RT_BUCKET, bucket,
                     jnp.where(lane == RT_WLO, w_lo, jnp.where(lane == RT_WHI, w_hi, 0.0)))


def _out_proj_kernel(mod_ref, of_ref, ob_ref, gz_ref, yf_ref, yb_ref, sz_ref, om_ref, x_ref, mods_ref,
                     gn_ref, sn_ref, n2_ref, wout_ref, wr_ref, br_ref, x1_ref, h2_ref, rt_ref):
    del mod_ref
    acc = None
    for h in range(GDN_HEADS):
        sl = slice(h * GDN_DV, (h + 1) * GDN_DV)
        o = _rms(of_ref[:, sl] + ob_ref[:, sl]) * gn_ref[...] * _silu(gz_ref[:, sl])
        t = jnp.dot(o.astype(bf16), wout_ref[sl, :], preferred_element_type=f32)
        acc = t if acc is None else acc + t
    y = (yf_ref[...] + yb_ref[...]) * _silu(sz_ref[...])
    ys = _rms(y) * sn_ref[...]
    acc = acc + jnp.dot(ys.astype(bf16), wout_ref[GDN_W:GDN_W + SSM_INNER, :], preferred_element_type=f32)
    acc = acc + jnp.dot(om_ref[...].astype(bf16), wout_ref[GDN_W + SSM_INNER:MIX_W, :],
                        preferred_element_type=f32)
    x1 = x_ref[...] + mods_ref[MOD_G1:MOD_G1 + 1, :] * acc
    x1_ref[...] = x1
    h2 = _rms(x1) * n2_ref[...]
    h2 = h2 * (1.0 + mods_ref[MOD_SC2:MOD_SC2 + 1, :]) + mods_ref[MOD_SH2:MOD_SH2 + 1, :]
    rt = _route(_mm3(h2, wr_ref[...]) + br_ref[...])
    h2_ref[:, 0:D_MODEL] = h2
    h2_ref[:, D_MODEL:H2X_W] = rt
    rt_ref[...] = rt


def out_proj(lay, o_f, o_b, proj, y_f, y_b, o_mla, x, mods, gdn_norm, ssm_norm, norm2, w_out_b, w_rt, b_rt, layer):
    row_map = lambda i, mod: (i, 0)
    lyr = lambda i, mod: (layer, 0, 0)
    return pl.pallas_call(
        _out_proj_kernel,
        grid_spec=pltpu.PrefetchScalarGridSpec(
            num_scalar_prefetch=1, grid=(lay.n_blk,),
            in_specs=[
                pl.BlockSpec((TB, GDN_W), row_map),
                pl.BlockSpec((TB, GDN_W), row_map),
                pl.BlockSpec((TB, GDN_W), lambda i, mod: (i, COL_GZ // GDN_W)),
                pl.BlockSpec((TB, SSM_INNER), row_map),
                pl.BlockSpec((TB, SSM_INNER), row_map),
                pl.BlockSpec((TB, SSM_INNER), lambda i, mod: (i, COL_SZ // SSM_INNER)),
                pl.BlockSpec((TB, MLA_W), row_map),
                pl.BlockSpec((TB, D_MODEL), row_map),
                pl.BlockSpec((None, 6, D_MODEL), lambda i, mod: (mod[i], 0, 0)),
                pl.BlockSpec((None, 1, GDN_DV), lyr),
                pl.BlockSpec((None, 1, SSM_INNER), lyr),
                pl.BlockSpec((None, 1, D_MODEL), lyr),
                pl.BlockSpec((None, MIX_W, D_MODEL), lyr, pipeline_mode=pl.Buffered(1)),
                pl.BlockSpec((None, D_MODEL, LANES), lyr),
                pl.BlockSpec((None, 1, LANES), lyr),
            ],
            out_specs=[
                pl.BlockSpec((TB, D_MODEL), row_map),
                pl.BlockSpec((TB, H2X_W), row_map),
                pl.BlockSpec((TB, LANES), row_map),
            ]),
        out_shape=[
            jax.ShapeDtypeStruct((lay.m, D_MODEL), f32),
            jax.ShapeDtypeStruct((lay.m, H2X_W), f32),
            jax.ShapeDtypeStruct((lay.m, LANES), f32),
        ],
        compiler_params=pltpu.CompilerParams(dimension_semantics=("arbitrary",), vmem_limit_bytes=VMEM_LIMIT),
        name="out_proj",
    )(jnp.asarray(lay.mod), o_f, o_b, proj, y_f, y_b, proj, o_mla, x, mods, gdn_norm, ssm_norm, norm2,
      w_out_b, w_rt, b_rt)


TME = 256
PAIR_LO = (0, 0, 0, 1, 1, 2)
PAIR_HI = (1, 2, 3, 2, 3, 3)


DMA_UNROLL = 8


def _moe_kernel(te_ref, nu_ref, ts_ref, nv_ref, order_ref,
                h2_hbm, wg_ref, wu_ref, wd_ref, y_hbm, xbuf, ybuf, gsem, ssem):
    del te_ref
    t = pl.program_id(0)
    j = pl.program_id(1)
    n_used = nu_ref[0]
    used = t < n_used
    slot = t % 2
    n_tok = y_hbm.shape[0] - TME

    def gather_rows(tile, sl):
        base = ts_ref[tile]

        def issue(r, c):
            pltpu.make_async_copy(h2_hbm.at[pl.ds(order_ref[base + r], 1), :],
                                  xbuf.at[sl, pl.ds(r, 1), :], gsem.at[sl]).start()
            return c
        lax.fori_loop(0, TME, issue, 0, unroll=DMA_UNROLL)

    def scatter_rows(tile, sl):
        base = ts_ref[tile]
        nvalid = nv_ref[tile]

        def issue(r, c):
            row = jnp.where(r < nvalid, order_ref[base + r], n_tok + r)
            pltpu.make_async_copy(ybuf.at[sl, pl.ds(r, 1), :],
                                  y_hbm.at[pl.ds(row, 1), :], ssem.at[sl]).start()
            return c
        lax.fori_loop(0, TME, issue, 0, unroll=DMA_UNROLL)

    def wait_gather(sl):
        pltpu.make_async_copy(h2_hbm.at[pl.ds(0, TME), :], xbuf.at[sl], gsem.at[sl]).wait()

    def wait_scatter(sl):
        pltpu.make_async_copy(ybuf.at[sl], y_hbm.at[pl.ds(0, TME), :], ssem.at[sl]).wait()

    @pl.when(jnp.logical_and(used, jnp.logical_and(t == 0, j == 0)))
    def _():
        gather_rows(0, 0)
        ybuf[1] = jnp.zeros((TME, D_MODEL), f32)
        spare = pltpu.make_async_copy(ybuf.at[1], y_hbm.at[pl.ds(n_tok, TME), :], ssem.at[1])
        spare.start()
        spare.wait()

    @pl.when(jnp.logical_and(used, j == 0))
    def _():
        wait_gather(slot)

    @pl.when(jnp.logical_and(t + 1 < n_used, j == 1))
    def _():
        gather_rows(t + 1, 1 - slot)

    @pl.when(used)
    def _():
        xb = xbuf[slot, :, 0:D_MODEL].astype(bf16)
        hg = jnp.dot(xb, wg_ref[...], preferred_element_type=f32)
        hu = jnp.dot(xb, wu_ref[...], preferred_element_type=f32)
        hdn = (_silu(hg) * hu).astype(bf16)
        yo = jnp.dot(hdn, wd_ref[...], preferred_element_type=f32)
        lane = lax.broadcasted_iota(jnp.int32, (TME, LANES), 1)
        w_lane = jnp.where(j == slot, RT_WLO, RT_WHI)
        wsel = jnp.sum(jnp.where(lane == w_lane, xbuf[slot, :, D_MODEL:H2X_W], 0.0), axis=-1, keepdims=True)

        @pl.when(j == 0)
        def _():
            ybuf[slot] = wsel * yo

        @pl.when(j == 1)
        def _():
            ybuf[slot] = ybuf[slot] + wsel * yo

    @pl.when(jnp.logical_and(used, j == 1))
    def _():
        @pl.when(t > 0)
        def _():
            wait_scatter(1 - slot)

        scatter_rows(t, slot)

        @pl.when(t == n_used - 1)
        def _():
            wait_scatter(slot)


def moe_dispatch(lay, route):
    m = lay.m
    n_tiles = m // TME + N_BUCKETS
    bucket = route[:, RT_BUCKET].astype(jnp.int32)
    order = jnp.argsort(bucket).astype(jnp.int32)
    counts = jnp.sum(bucket[:, None] == jnp.arange(N_BUCKETS, dtype=jnp.int32)[None, :], axis=0, dtype=jnp.int32)
    ptiles = (counts + TME - 1) // TME
    pend = jnp.cumsum(ptiles)
    pstart = pend - ptiles
    start = jnp.cumsum(counts) - counts
    n_used = pend[-1]
    tid = jnp.arange(n_tiles, dtype=jnp.int32)
    tid_c = jnp.minimum(tid, n_used - 1)
    tb = jnp.minimum(jnp.sum(pend[None, :] <= tid_c[:, None], axis=1, dtype=jnp.int32), N_BUCKETS - 1)
    sel = tb[:, None] == jnp.arange(N_BUCKETS, dtype=jnp.int32)[None, :]
    pick = lambda v: jnp.sum(jnp.where(sel, v[None, :], 0), axis=1, dtype=jnp.int32)
    off = (tid_c - pick(pstart)) * TME
    tstart = pick(start) + off
    nvalid = jnp.where(tid < n_used, jnp.clip(pick(counts) - off, 0, TME), 0).astype(jnp.int32)
    order_p = jnp.concatenate([order, jnp.zeros((TME,), jnp.int32)])
    grp, pr = tb // N_PAIRS, tb % N_PAIRS
    e_lo = grp * E_PER_GROUP + jnp.asarray(PAIR_LO, jnp.int32)[pr]
    e_hi = grp * E_PER_GROUP + jnp.asarray(PAIR_HI, jnp.int32)[pr]
    swap = (tid % 2) == 1
    te = jnp.stack([jnp.where(swap, e_hi, e_lo), jnp.where(swap, e_lo, e_hi)], -1).reshape(-1)
    te = jnp.concatenate([te, te[-2:]])
    return (te.astype(jnp.int32), n_used.reshape(1).astype(jnp.int32), tstart.astype(jnp.int32), nvalid,
            order_p, n_tiles)


def moe_apply(lay, h2, route, wg_b, wu_b, wd_b, layer):
    te, n_used, tstart, nvalid, order_p, n_tiles = moe_dispatch(lay, route)
    wmap = lambda t, j, te, *_: (layer, te[2 * t + j], 0, 0)
    return pl.pallas_call(
        _moe_kernel,
        grid_spec=pltpu.PrefetchScalarGridSpec(
            num_scalar_prefetch=5, grid=(n_tiles, 2),
            in_specs=[
                pl.BlockSpec(memory_space=pl.ANY),
                pl.BlockSpec((None, None, D_MODEL, EXPERT_FF), wmap),
                pl.BlockSpec((None, None, D_MODEL, EXPERT_FF), wmap),
                pl.BlockSpec((None, None, EXPERT_FF, D_MODEL), wmap),
            ],
            out_specs=pl.BlockSpec(memory_space=pl.ANY),
            scratch_shapes=[
                pltpu.VMEM((2, TME, H2X_W), f32),
                pltpu.VMEM((2, TME, D_MODEL), f32),
                pltpu.SemaphoreType.DMA((2,)),
                pltpu.SemaphoreType.DMA((2,)),
            ]),
        out_shape=jax.ShapeDtypeStruct((lay.m + TME, D_MODEL), f32),
        compiler_params=pltpu.CompilerParams(dimension_semantics=("arbitrary", "arbitrary"),
                                             vmem_limit_bytes=VMEM_LIMIT),
        name="moe",
    )(te, n_used, tstart, nvalid, order_p, h2, wg_b, wu_b, wd_b)


def _final_kernel(mod_ref, x_ref, y_ref, mods_ref, g_ref, o_ref):
    del mod_ref
    x = x_ref[...] + mods_ref[MOD_G2:MOD_G2 + 1, :] * y_ref[...]
    o_ref[...] = _rms(x) * g_ref[...]


def final_norm_apply(lay, x, ytok, mods, g):
    row_map = lambda i, mod: (i, 0)
    return pl.pallas_call(
        _final_kernel,
        grid_spec=pltpu.PrefetchScalarGridSpec(
            num_scalar_prefetch=1, grid=(lay.n_blk,),
            in_specs=[
                pl.BlockSpec((TB, D_MODEL), row_map),
                pl.BlockSpec((TB, D_MODEL), row_map),
                pl.BlockSpec((None, 6, D_MODEL), lambda i, mod: (mod[i], 0, 0)),
                pl.BlockSpec((1, D_MODEL), lambda i, mod: (0, 0)),
            ],
            out_specs=pl.BlockSpec((TB, D_MODEL), row_map)),
        out_shape=jax.ShapeDtypeStruct((lay.m, D_MODEL), f32),
        compiler_params=pltpu.CompilerParams(dimension_semantics=("arbitrary",)),
        name="final_norm",
    )(jnp.asarray(lay.mod), x, ytok, mods, g)


def _reorder_w_in(w):
    g_qkv, g_z, g_a, g_b, s_z, s_xbc, s_dt, m_q, m_kv = jnp.split(w, np.cumsum(IN_SIZES)[:-1], axis=1)
    ckv, kpe = m_kv[:, :KV_RANK], m_kv[:, KV_RANK:]
    z = lambda n: jnp.zeros((w.shape[0], n), w.dtype)
    small = jnp.concatenate([g_a, g_b, s_dt, z(SM_KPE - SM_DT - 2 * SSM_HEADS), kpe, z(LANES - SM_KPE - MLA_ROPE)], 1)
    return jnp.concatenate([g_qkv, g_z, s_z, m_q, ckv, s_xbc, small], 1).astype(bf16)


def _reorder_w_uq(w):
    r = w.reshape(Q_RANK, MLA_HEADS, MLA_QK)
    return jnp.pad(r, ((0, 0), (0, 0), (0, LANES - MLA_QK))).reshape(Q_RANK, Q_HEADS_W).astype(bf16)


def _reorder_w_ukv(w):
    r = w.reshape(KV_RANK, MLA_HEADS, MLA_NOPE + MLA_VDIM)
    k = jnp.pad(r[:, :, :MLA_NOPE], ((0, 0), (0, 0), (0, LANES - MLA_NOPE))).reshape(KV_RANK, MLA_HEADS * LANES)
    v = jnp.pad(r[:, :, MLA_NOPE:], ((0, 0), (0, 0), (0, LANES - MLA_VDIM))).reshape(KV_RANK, MLA_HEADS * LANES)
    return jnp.concatenate([k, v], 1).astype(bf16)


def _lane_row(vals, lane0):
    n = vals.shape[-1]
    return jnp.pad(vals, ((0, 0), (lane0, LANES - lane0 - n)))


def _par_rows(row0, row1):
    z = jnp.zeros_like(row0)
    return jnp.stack([row0, row1] + [z] * (SUBLANES - 2), axis=1)


def kernel(x_prompt, x_sample, cache_ckv, cache_kpe, state_gdn, state_ssm, c, c_ctx, w_ada, b_ada, norm1, norm2, w_in, gdn_conv, gdn_A_log, gdn_dt_bias, gdn_norm, ssm_conv, ssm_conv_bias, ssm_A_log, ssm_dt_bias, ssm_D, ssm_norm, mla_q_norm, mla_w_uq, mla_kv_norm, mla_w_ukv, w_out, router_group, router_group_bias, router_expert, router_expert_bias, moe_w_gate, moe_w_up, moe_w_down, final_norm):
    n_ctx, t_ctx, _ = x_prompt.shape
    n_lat, t_lat, _ = x_sample.shape
    past = cache_ckv.shape[2]
    depth = w_in.shape[0]
    assert n_lat < SUBLANES and t_ctx % TB == 0 and t_lat % TB == 0 and past % TB == 0
    lay = Layout(n_ctx, t_ctx, n_lat, t_lat)
    m_ctx = lay.m_ctx

    x = jnp.concatenate([x_prompt.reshape(m_ctx, D_MODEL), x_sample.reshape(lay.m_lat, D_MODEL)], 0)
    cvec = jnp.concatenate([c_ctx[None], c, jnp.zeros((SUBLANES - 1 - n_lat, D_MODEL), f32)], 0)
    tables = rope_tables(lay)

    b_ada3 = b_ada[:, None, :]
    norm1_3, norm2_3 = norm1[:, None, :], norm2[:, None, :]
    gdn_par = _par_rows(_lane_row(gdn_A_log.reshape(depth, -1), SM_GA), _lane_row(gdn_dt_bias.reshape(depth, -1), SM_GA))
    ssm_par = _par_rows(_lane_row(ssm_A_log.reshape(depth, -1), SM_DT), _lane_row(ssm_dt_bias.reshape(depth, -1), SM_DT))
    d_lanes = jnp.pad(jnp.repeat(ssm_D, SSM_HEADDIM, axis=-1), ((0, 0), (0, SSM_XBC - SSM_INNER)))
    ssm_cpar = _par_rows(ssm_conv_bias, d_lanes)
    w_rt = jnp.pad(jnp.concatenate([router_group, router_expert], -1),
                   ((0, 0), (0, 0), (0, LANES - N_EGROUPS - N_EXPERTS)))
    b_rt = jnp.pad(jnp.concatenate([router_group_bias, router_expert_bias], -1),
                   ((0, 0), (0, LANES - N_EGROUPS - N_EXPERTS)))[:, None, :]
    kpe_cache = jnp.pad(cache_kpe, ((0, 0), (0, 0), (0, 0), (SM_KPE, LANES - SM_KPE - MLA_ROPE)))

    w_out_b = w_out.astype(bf16)
    wg_b, wu_b, wd_b = moe_w_gate.astype(bf16), moe_w_up.astype(bf16), moe_w_down.astype(bf16)

    ckvs, kpes, gdns, ssms = [], [], [], []
    ytok = mods_prev = None
    for l in range(depth):
        mods = ada_mod(cvec, w_ada, b_ada3, l)
        res = None if l == 0 else (ytok, mods_prev)
        proj, x = in_proj(lay, x, mods, norm1_3, _reorder_w_in(w_in[l]), l, res)

        zg = jnp.zeros((n_ctx, GDN_HEADS, GDN_DK, GDN_DV), f32)
        o_f, sg_f = gdn_scan(lay, proj, gdn_conv, gdn_par, jnp.concatenate([zg, state_gdn[:, l, 0]], 0), l, False)
        o_b, sg_b = gdn_scan(lay, proj, gdn_conv, gdn_par, jnp.concatenate([zg, state_gdn[:, l, 1]], 0), l, True)

        zs = jnp.zeros((n_ctx, SSM_INNER, LANES), f32)
        y_f, hs_f = ssd_scan(lay, proj, ssm_conv, ssm_cpar, ssm_par,
                             jnp.concatenate([zs, ssm_state_to_lanes(state_ssm[:, l, 0])], 0), l, False)
        y_b, hs_b = ssd_scan(lay, proj, ssm_conv, ssm_cpar, ssm_par,
                             jnp.concatenate([zs, ssm_state_to_lanes(state_ssm[:, l, 1])], 0), l, True)

        q, ckvn, kpe = mla_prep(lay, proj, mla_q_norm[:, None, :], mla_kv_norm[:, None, :],
                                _reorder_w_uq(mla_w_uq[l]), tables, l)
        w_ukv_r = _reorder_w_ukv(mla_w_ukv[l])
        k_ctx, v_ctx = kv_up(ckvn[:m_ctx], kpe[:m_ctx], w_ukv_r)
        ckv_lat = jnp.concatenate([cache_ckv[:, l], ckvn[m_ctx:].reshape(n_lat, t_lat, KV_RANK)], 1)
        kpe_lat = jnp.concatenate([kpe_cache[:, l], kpe[m_ctx:].reshape(n_lat, t_lat, LANES)], 1)
        k_lat, v_lat = kv_up(ckv_lat.reshape(-1, KV_RANK), kpe_lat.reshape(-1, LANES), w_ukv_r)
        o_mla = jnp.concatenate([
            attention(q, k_ctx, v_ctx, 0, n_ctx, t_ctx, t_ctx),
            attention(q, k_lat, v_lat, m_ctx // TB, n_lat, t_lat, past + t_lat)], 0)

        x, h2, route = out_proj(lay, o_f, o_b, proj, y_f, y_b, o_mla, x, mods, gdn_norm[:, None, :],
                                ssm_norm[:, None, :], norm2_3, w_out_b, w_rt, b_rt, l)
        ytok = moe_apply(lay, h2, route, wg_b, wu_b, wd_b, l)
        mods_prev = mods

        ckvs.append(ckvn[:m_ctx].reshape(n_ctx, t_ctx, KV_RANK))
        kpes.append(kpe[:m_ctx, SM_KPE:SM_KPE + MLA_ROPE].reshape(n_ctx, t_ctx, MLA_ROPE))
        gdns.append(jnp.stack([sg_f[:n_ctx], sg_b[:n_ctx]], 1))
        ssms.append(jnp.stack([ssm_state_from_lanes(hs_f[:n_ctx]), ssm_state_from_lanes(hs_b[:n_ctx])], 1))

    y = final_norm_apply(lay, x, ytok, mods_prev, final_norm[None, :])
    return (y[:m_ctx].reshape(n_ctx, t_ctx, D_MODEL), y[m_ctx:].reshape(n_lat, t_lat, D_MODEL),
            jnp.stack(ckvs, 1), jnp.stack(kpes, 1), jnp.stack(gdns, 1), jnp.stack(ssms, 1))
```

```python
import functools
import math

import numpy as np
import jax
import jax.numpy as jnp
from jax import lax
from jax.experimental import pallas as pl
from jax.experimental.pallas import tpu as pltpu

f32 = jnp.float32
bf16 = jnp.bfloat16

D_MODEL = 2048
DEPTH = 2
GRID_W = 64
EPS = 1e-6
CONV_W = 5
CHUNK = 64

GDN_HEADS = 4
GDN_DK = 128
GDN_DV = 128
GDN_W = GDN_HEADS * GDN_DV
GDN_QKV = 2 * GDN_HEADS * GDN_DK + GDN_HEADS * GDN_DV

SSM_HEADS = 16
SSM_HEADDIM = 64
SSM_GROUPS = 2
SSM_DSTATE = 64
SSM_INNER = SSM_HEADS * SSM_HEADDIM
SSM_XBC = SSM_INNER + 2 * SSM_GROUPS * SSM_DSTATE

MLA_HEADS = 8
MLA_NOPE = 64
MLA_ROPE = 32
MLA_VDIM = 64
Q_RANK = 512
KV_RANK = 256
MLA_W = MLA_HEADS * MLA_VDIM
ROPE_THETA = 10000.0

MIX_W = GDN_W + SSM_INNER + MLA_W
IN_SIZES = (GDN_QKV, GDN_W, 2 * GDN_HEADS, 2 * GDN_HEADS, SSM_INNER, SSM_XBC, 2 * SSM_HEADS, Q_RANK, KV_RANK + MLA_ROPE)

N_EGROUPS = 4
E_PER_GROUP = 4
N_EXPERTS = N_EGROUPS * E_PER_GROUP
EXPERT_FF = 512

LANES = 128
SUBLANES = 8

TB = 256

COL_QKV = 0
COL_GZ = 1536
COL_SZ = 2048
COL_MQ = 3072
COL_CKV = 3584
COL_XBC = 3840
COL_SMALL = 5120
PROJ_W = 5248
SM_GA, SM_GB, SM_DT, SM_KPE = 0, 8, 16, 64


class Layout:
    def __init__(self, n_ctx, t_ctx, n_lat, t_lat):
        self.n_ctx, self.t_ctx, self.n_lat, self.t_lat = n_ctx, t_ctx, n_lat, t_lat
        self.m_ctx = n_ctx * t_ctx
        self.m_lat = n_lat * t_lat
        self.m = self.m_ctx + self.m_lat
        self.n_seq = n_ctx + n_lat
        bc, bl = t_ctx // TB, t_lat // TB
        seq, first, last, mod = [], [], [], []
        for s in range(n_ctx):
            for j in range(bc):
                seq.append(s); first.append(int(j == 0)); last.append(int(j == bc - 1)); mod.append(0)
        for s in range(n_lat):
            for j in range(bl):
                seq.append(n_ctx + s); first.append(int(j == 0)); last.append(int(j == bl - 1)); mod.append(1 + s)
        self.n_blk = len(seq)
        self.seq = np.array(seq, np.int32)
        self.first = np.array(first, np.int32)
        self.last = np.array(last, np.int32)
        self.mod = np.array(mod + mod[-1:], np.int32)
        fwd = np.arange(self.n_blk, dtype=np.int32)
        bwd = []
        i = 0
        while i < self.n_blk:
            j = i
            while self.last[j] == 0:
                j += 1
            bwd.extend(range(j, i - 1, -1))
            i = j + 1
        self.order = {False: fwd, True: np.array(bwd, np.int32)}

    def scan_tables(self, rev):
        order = self.order[rev]
        pad = lambda a: jnp.asarray(np.concatenate([a, a[-1:]]))
        return pad(order), pad(self.seq[order]), pad(self.first[order]), pad(self.last[order])


def _mm(a, b):
    return jnp.dot(a.astype(bf16), b.astype(bf16), preferred_element_type=f32)


def _mm_nt(a, b):
    return lax.dot_general(a.astype(bf16), b.astype(bf16), (((1,), (1,)), ((), ())), preferred_element_type=f32)


def _mm_tn(a, b):
    n = a.shape[1]
    eye = jnp.where(lax.broadcasted_iota(jnp.int32, (n, n), 0) == lax.broadcasted_iota(jnp.int32, (n, n), 1),
                    1.0, 0.0).astype(bf16)
    at = lax.dot_general(eye, a.astype(bf16), (((1,), (1,)), ((), ())), preferred_element_type=f32)
    return jnp.dot(at.astype(bf16), b.astype(bf16), preferred_element_type=f32)


def _split3(x):
    hi = x.astype(bf16)
    r = x - hi.astype(f32)
    mid = r.astype(bf16)
    lo = (r - mid.astype(f32)).astype(bf16)
    return hi, mid, lo


def _mm01(m01, x):
    hi, mid, lo = _split3(x)
    d = lambda t: jnp.dot(m01, t, preferred_element_type=f32)
    return d(hi) + d(mid) + d(lo)


def _transpose_exact(eye, x):
    hi, mid, lo = _split3(x)
    d = lambda t: lax.dot_general(eye, t, (((1,), (1,)), ((), ())), preferred_element_type=f32)
    return d(hi) + d(mid) + d(lo)


def _mm3(a, b):
    ah = a.astype(bf16)
    al = (a - ah.astype(f32)).astype(bf16)
    bh = b.astype(bf16)
    bl = (b - bh.astype(f32)).astype(bf16)
    d = lambda x, y: jnp.dot(x, y, preferred_element_type=f32)
    return d(ah, bh) + d(ah, bl) + d(al, bh)


def _sigmoid(x):
    return 1.0 / (1.0 + jnp.exp(-x))


def _silu(x):
    return x * _sigmoid(x)


def _softplus(x):
    return jnp.maximum(x, 0.0) + jnp.log(1.0 + jnp.exp(-jnp.abs(x)))


def _chunk_masks(rev):
    i = lax.broadcasted_iota(jnp.int32, (CHUNK, CHUNK), 0)
    j = lax.broadcasted_iota(jnp.int32, (CHUNK, CHUNK), 1)
    if rev:
        incl, strict = j >= i, j > i
    else:
        incl, strict = j <= i, j < i
    ll = jnp.where(incl, 1.0, 0.0).astype(bf16)
    lls = jnp.where(incl, 0.0, 1.0).astype(bf16)
    uu = jnp.where(strict, 1.0, 0.0)
    return incl, strict, ll, lls, uu


def _fill_ext(ext_ref, main_ref, prev_ref, next_ref, sfirst, slast):
    ext_ref[0:SUBLANES, :] = jnp.where(sfirst == 1, 0.0, prev_ref[...])
    ext_ref[SUBLANES:SUBLANES + TB, :] = main_ref[...]
    ext_ref[SUBLANES + TB:2 * SUBLANES + TB, :] = jnp.where(slast == 1, 0.0, next_ref[...])


def _conv_tile(ext_ref, w_ref, r0, col0, width, bias=None):
    acc = None
    for t in range(CONV_W):
        rows = ext_ref[pl.ds(SUBLANES + r0 - CONV_W // 2 + t, CHUNK), col0:col0 + width]
        term = rows * w_ref[t:t + 1, col0:col0 + width]
        acc = term if acc is None else acc + term
    if bias is not None:
        acc = acc + bias
    return _silu(acc)


INV_BLOCK = 16


def _block_masks():
    i = lax.broadcasted_iota(jnp.int32, (CHUNK, CHUNK), 0)
    j = lax.broadcasted_iota(jnp.int32, (CHUNK, CHUNK), 1)
    blk16 = (i // INV_BLOCK) == (j // INV_BLOCK)
    blk32 = (i // (2 * INV_BLOCK)) == (j // (2 * INV_BLOCK))
    return blk16, blk32


def _unit_tri_inverse_many(lmats, eye, blk16, blk32):
    ps = list(lmats)
    ld = {p: jnp.where(blk16, lmats[p], 0.0) for p in ps}
    t = {p: eye - ld[p] for p in ps}
    pw = {p: _mm3(ld[p], ld[p]) for p in ps}
    for it in range(3):
        t = {p: t[p] + _mm3(t[p], pw[p]) for p in ps}
        if it < 2:
            pw = {p: _mm3(pw[p], pw[p]) for p in ps}
    in32 = jnp.logical_and(blk32, jnp.logical_not(blk16))
    m = {p: _mm3(t[p], jnp.where(in32, lmats[p], 0.0)) for p in ps}
    t = {p: t[p] - _mm3(m[p], t[p]) for p in ps}
    m = {p: _mm3(t[p], jnp.where(blk32, 0.0, lmats[p])) for p in ps}
    return {p: t[p] - _mm3(m[p], t[p]) for p in ps}


def _gdn_kernel(order_ref, seq_ref, first_ref, last_ref,
                main_ref, prev_ref, next_ref, small_ref, convw_ref, par_ref, s0_ref,
                o_ref, sfin_ref, s_ref, ext_ref, *, rev):
    del order_ref, seq_ref
    step = pl.program_id(0)
    sfirst = first_ref[step]
    slast = last_ref[step]
    pfirst, plast = (slast, sfirst) if rev else (sfirst, slast)
    d = 1 if rev else 0

    @pl.when(pfirst == 1)
    def _():
        s_ref[...] = s0_ref[...]

    _fill_ext(ext_ref, main_ref, prev_ref, next_ref, sfirst, slast)

    incl, strict, ll, lls, uu = _chunk_masks(rev)
    eye = jnp.where(lax.broadcasted_iota(jnp.int32, (CHUNK, CHUNK), 0)
                    == lax.broadcasted_iota(jnp.int32, (CHUNK, CHUNK), 1), 1.0, 0.0)
    blk16, blk32 = _block_masks()
    neg_a = -jnp.exp(par_ref[0:1, :])
    dt_bias = par_ref[1:2, :]
    n_chunks = TB // CHUNK

    chunk_order = [n_chunks - 1 - cc if rev else cc for cc in range(n_chunks)]
    heads = range(GDN_HEADS)
    eye_k = jnp.where(lax.broadcasted_iota(jnp.int32, (GDN_DK, GDN_DK), 0)
                      == lax.broadcasted_iota(jnp.int32, (GDN_DK, GDN_DK), 1), 1.0, 0.0).astype(bf16)

    gates = {}
    for c in chunk_order:
        sm = small_ref[c * CHUNK:(c + 1) * CHUNK, :]
        la_blk = neg_a * _softplus(sm + dt_bias)
        gates[c] = (la_blk, _sigmoid(sm))
    g_blk = {c: _mm01(ll, gates[c][0]) for c in chunk_order}
    gr_blk = {c: _mm01(lls, gates[c][0]) for c in chunk_order}
    gl_blk = {c: jnp.sum(gates[c][0], axis=0, keepdims=True) for c in chunk_order}

    probs = [(c, h) for c in chunk_order for h in heads]
    ia = lambda h: SM_GA + d * GDN_HEADS + h
    ib = lambda h: SM_GB + d * GDN_HEADS + h
    col = lambda blk, lane: blk[:, lane:lane + 1]
    beta = {p: col(gates[p[0]][1], ib(p[1])) for p in probs}
    q, k, v = {}, {}, {}
    for p in probs:
        c, h = p
        r0 = c * CHUNK
        qq = _conv_tile(ext_ref, convw_ref, r0, h * GDN_DK, GDN_DK)
        kx = _conv_tile(ext_ref, convw_ref, r0, GDN_HEADS * GDN_DK + h * GDN_DK, GDN_DK)
        v[p] = _conv_tile(ext_ref, convw_ref, r0, 2 * GDN_HEADS * GDN_DK + h * GDN_DV, GDN_DV)
        q[p] = qq * lax.rsqrt(jnp.sum(qq * qq, axis=-1, keepdims=True) + EPS) * (GDN_DK ** -0.5)
        k[p] = kx * lax.rsqrt(jnp.sum(kx * kx, axis=-1, keepdims=True) + EPS)
    g_t = {c: _transpose_exact(eye_k, g_blk[c]) for c in chunk_order}
    dmat = {p: jnp.where(incl, col(g_blk[p[0]], ia(p[1])) - g_t[p[0]][ia(p[1]):ia(p[1]) + 1, :], 0.0)
            for p in probs}
    kk = {p: _mm_nt(k[p], k[p]) for p in probs}
    qk = {p: _mm_nt(q[p], k[p]) for p in probs}
    decay = {p: jnp.where(incl, jnp.exp(dmat[p]), 0.0) for p in probs}
    lmat = {p: jnp.where(strict, kk[p] * beta[p] * decay[p], 0.0) for p in probs}
    qk = {p: jnp.where(incl, qk[p] * decay[p], 0.0) for p in probs}
    tinv = _unit_tri_inverse_many(lmat, eye, blk16, blk32)
    uw = {}
    for p in probs:
        e_g = jnp.exp(col(g_blk[p[0]], ia(p[1])))
        rhs = jnp.concatenate([v[p] * beta[p], k[p] * beta[p] * e_g], axis=1)
        uw[p] = _mm3(tinv[p], rhs)
        q[p] = q[p] * e_g
    kdt = {}
    for p in probs:
        k_dec = (k[p] * jnp.exp(col(gr_blk[p[0]], ia(p[1])))).astype(bf16)
        kdt[p] = lax.dot_general(eye_k, k_dec, (((1,), (1,)), ((), ())), preferred_element_type=f32)

    for c in chunk_order:
        ps = [(c, h) for h in heads]
        s_old = {p: s_ref[p[1]] for p in ps}
        ws = {p: _mm(uw[p][:, GDN_DV:], s_old[p]) for p in ps}
        qs = {p: _mm(q[p], s_old[p]) for p in ps}
        v_new = {p: uw[p][:, :GDN_DV] - ws[p] for p in ps}
        o = {p: qs[p] + _mm(qk[p], v_new[p]) for p in ps}
        for p in ps:
            h = p[1]
            s_ref[h] = s_old[p] * jnp.exp(col(gl_blk[c], ia(h))) + _mm(kdt[p], v_new[p])
            o_ref[c * CHUNK:(c + 1) * CHUNK, h * GDN_DV:(h + 1) * GDN_DV] = o[p]

    @pl.when(plast == 1)
    def _():
        sfin_ref[...] = s_ref[...]


def _halo_maps(n_rows8):
    prev_map = lambda i, order, *_: (jnp.maximum(order[i] * (TB // SUBLANES) - 1, 0), 0)
    next_map = lambda i, order, *_: (jnp.minimum((order[i] + 1) * (TB // SUBLANES), n_rows8 - 1), 0)
    return prev_map, next_map


def gdn_scan(lay, proj, convw, par, s0, layer, rev):
    prev_map, next_map = _halo_maps(lay.m // SUBLANES)
    blk_map = lambda i, order, *_: (order[i], 0)
    seq_map = lambda i, order, seq, *_: (seq[i], 0, 0, 0)
    grid_spec = pltpu.PrefetchScalarGridSpec(
        num_scalar_prefetch=4,
        grid=(lay.n_blk,),
        in_specs=[
            pl.BlockSpec((TB, GDN_QKV), blk_map),
            pl.BlockSpec((SUBLANES, GDN_QKV), prev_map),
            pl.BlockSpec((SUBLANES, GDN_QKV), next_map),
            pl.BlockSpec((TB, LANES), lambda i, order, *_: (order[i], COL_SMALL // LANES)),
            pl.BlockSpec((None, CONV_W, GDN_QKV), lambda i, *_: (layer, 0, 0)),
            pl.BlockSpec((None, SUBLANES, LANES), lambda i, *_: (layer, 0, 0)),
            pl.BlockSpec((None, GDN_HEADS, GDN_DK, GDN_DV), seq_map),
        ],
        out_specs=[
            pl.BlockSpec((TB, GDN_W), blk_map),
            pl.BlockSpec((None, GDN_HEADS, GDN_DK, GDN_DV), seq_map),
        ],
        scratch_shapes=[
            pltpu.VMEM((GDN_HEADS, GDN_DK, GDN_DV), f32),
            pltpu.VMEM((TB + 2 * SUBLANES, GDN_QKV), f32),
        ],
    )
    return pl.pallas_call(
        functools.partial(_gdn_kernel, rev=rev),
        grid_spec=grid_spec,
        out_shape=[
            jax.ShapeDtypeStruct((lay.m, GDN_W), f32),
            jax.ShapeDtypeStruct((lay.n_seq, GDN_HEADS, GDN_DK, GDN_DV), f32),
        ],
        compiler_params=pltpu.CompilerParams(dimension_semantics=("arbitrary",)),
        name="gdn_bwd" if rev else "gdn_fwd",
    )(*lay.scan_tables(rev), proj, proj, proj, proj, convw, par, s0)


SSM_PAIRS = SSM_HEADS // 2
PAIRS_PER_GROUP = SSM_PAIRS // SSM_GROUPS
COL_B = SSM_INNER
COL_C = SSM_INNER + SSM_GROUPS * SSM_DSTATE


def _ssd_kernel(order_ref, seq_ref, first_ref, last_ref,
                main_ref, prev_ref, next_ref, small_ref, convw_ref, cpar_ref, par_ref, h0_ref,
                y_ref, hfin_ref, h_ref, ext_ref, *, rev):
    del order_ref, seq_ref
    step = pl.program_id(0)
    sfirst = first_ref[step]
    slast = last_ref[step]
    pfirst, plast = (slast, sfirst) if rev else (sfirst, slast)
    d = 1 if rev else 0

    @pl.when(pfirst == 1)
    def _():
        h_ref[...] = h0_ref[...]

    _fill_ext(ext_ref, main_ref, prev_ref, next_ref, sfirst, slast)

    incl, _, ll, lls, uu = _chunk_masks(rev)
    lane_lo = lax.broadcasted_iota(jnp.int32, (CHUNK, LANES), 1) < SSM_DSTATE
    row_lo = lax.broadcasted_iota(jnp.int32, (LANES, LANES), 0) < SSM_HEADDIM
    eye_p = jnp.where(lax.broadcasted_iota(jnp.int32, (LANES, LANES), 0)
                      == lax.broadcasted_iota(jnp.int32, (LANES, LANES), 1), 1.0, 0.0).astype(bf16)
    neg_a = -jnp.exp(par_ref[0:1, :])
    dt_bias = par_ref[1:2, :]
    n_chunks = TB // CHUNK

    for cc in range(n_chunks):
        c = n_chunks - 1 - cc if rev else cc
        r0 = c * CHUNK
        sm = small_ref[r0:r0 + CHUNK, :]
        dt_blk = _softplus(sm + dt_bias)
        dta_blk = dt_blk * neg_a
        acum_blk = _mm01(ll, dta_blk)
        ar_blk = _mm01(lls, dta_blk)
        al_blk = jnp.sum(dta_blk, axis=0, keepdims=True)
        b_pair = _conv_tile(ext_ref, convw_ref, r0, COL_B, LANES, cpar_ref[0:1, COL_B:COL_B + LANES])
        c_pair = _conv_tile(ext_ref, convw_ref, r0, COL_C, LANES, cpar_ref[0:1, COL_C:COL_C + LANES])
        c_g, b_g, cb = [], [], []
        for g in range(SSM_GROUPS):
            gmask = lane_lo if g == 0 else jnp.logical_not(lane_lo)
            c_g.append(jnp.where(gmask, c_pair, 0.0))
            b_g.append(jnp.where(gmask, b_pair, 0.0))
        cb = [_mm_nt(c_g[g], b_pair) for g in range(SSM_GROUPS)]
        pairs = range(SSM_PAIRS)
        grp = lambda p: p // PAIRS_PER_GROUP
        lane_of = lambda hh: SM_DT + d * SSM_HEADS + hh
        colv = lambda blk, hh: blk[:, lane_of(hh):lane_of(hh) + 1]
        both = lambda blk, p: jnp.where(lane_lo, colv(blk, 2 * p), colv(blk, 2 * p + 1))
        xs = [_conv_tile(ext_ref, convw_ref, r0, p * LANES, LANES, cpar_ref[0:1, p * LANES:(p + 1) * LANES])
              for p in pairs]
        acum_t = _transpose_exact(eye_p, acum_blk)
        dmat = [jnp.where(incl, colv(acum_blk, hh) - acum_t[lane_of(hh):lane_of(hh) + 1, :], 0.0)
                for hh in range(SSM_HEADS)]
        m_h = [cb[grp(hh // 2)] * jnp.where(incl, jnp.exp(dmat[hh]), 0.0) for hh in range(SSM_HEADS)]
        xdt = [xs[p] * both(dt_blk, p) for p in pairs]
        y = [jnp.where(lane_lo, _mm(m_h[2 * p], xdt[p]), _mm(m_h[2 * p + 1], xdt[p])) for p in pairs]
        xdt_t = [lax.dot_general(eye_p, xdt[p].astype(bf16), (((1,), (1,)), ((), ())),
                                 preferred_element_type=f32).astype(bf16) for p in pairs]
        st_new = [jnp.where(row_lo,
                            jnp.dot(xdt_t[p], (b_g[grp(p)] * jnp.exp(colv(ar_blk, 2 * p))).astype(bf16),
                                    preferred_element_type=f32),
                            jnp.dot(xdt_t[p], (b_g[grp(p)] * jnp.exp(colv(ar_blk, 2 * p + 1))).astype(bf16),
                                    preferred_element_type=f32)) for p in pairs]
        h_pair = [h_ref[p * LANES:(p + 1) * LANES, :] for p in pairs]
        y_off = [_mm_nt(c_g[grp(p)], h_pair[p]) for p in pairs]
        for p in pairs:
            col = p * LANES
            yp = y[p] + y_off[p] * jnp.exp(both(acum_blk, p))
            if not rev:
                yp = yp + xs[p] * cpar_ref[1:2, col:col + LANES]
            dec = jnp.where(row_lo, jnp.exp(colv(al_blk, 2 * p)), jnp.exp(colv(al_blk, 2 * p + 1)))
            h_ref[col:col + LANES, :] = h_pair[p] * dec + st_new[p]
            y_ref[r0:r0 + CHUNK, col:col + LANES] = yp

    @pl.when(plast == 1)
    def _():
        hfin_ref[...] = h_ref[...]


def ssd_scan(lay, proj, convw, cpar, par, h0, layer, rev):
    prev_map, next_map = _halo_maps(lay.m // SUBLANES)
    cblk = COL_XBC // SSM_XBC
    blk_map = lambda i, order, *_: (order[i], 0)
    seq_map = lambda i, order, seq, *_: (seq[i], 0, 0)
    grid_spec = pltpu.PrefetchScalarGridSpec(
        num_scalar_prefetch=4,
        grid=(lay.n_blk,),
        in_specs=[
            pl.BlockSpec((TB, SSM_XBC), lambda i, order, *_: (order[i], cblk)),
            pl.BlockSpec((SUBLANES, SSM_XBC), lambda i, *a: (prev_map(i, *a)[0], cblk)),
            pl.BlockSpec((SUBLANES, SSM_XBC), lambda i, *a: (next_map(i, *a)[0], cblk)),
            pl.BlockSpec((TB, LANES), lambda i, order, *_: (order[i], COL_SMALL // LANES)),
            pl.BlockSpec((None, CONV_W, SSM_XBC), lambda i, *_: (layer, 0, 0)),
            pl.BlockSpec((None, SUBLANES, SSM_XBC), lambda i, *_: (layer, 0, 0)),
            pl.BlockSpec((None, SUBLANES, LANES), lambda i, *_: (layer, 0, 0)),
            pl.BlockSpec((None, SSM_INNER, LANES), seq_map),
        ],
        out_specs=[
            pl.BlockSpec((TB, SSM_INNER), blk_map),
            pl.BlockSpec((None, SSM_INNER, LANES), seq_map),
        ],
        scratch_shapes=[
            pltpu.VMEM((SSM_INNER, LANES), f32),
            pltpu.VMEM((TB + 2 * SUBLANES, SSM_XBC), f32),
        ],
    )
    return pl.pallas_call(
        functools.partial(_ssd_kernel, rev=rev),
        grid_spec=grid_spec,
        out_shape=[
            jax.ShapeDtypeStruct((lay.m, SSM_INNER), f32),
            jax.ShapeDtypeStruct((lay.n_seq, SSM_INNER, LANES), f32),
        ],
        compiler_params=pltpu.CompilerParams(dimension_semantics=("arbitrary",)),
        name="ssd_bwd" if rev else "ssd_fwd",
    )(*lay.scan_tables(rev), proj, proj, proj, proj, convw, cpar, par, h0)


def ssm_state_to_lanes(h):
    n = h.shape[0]
    hg = h.reshape(n, SSM_GROUPS, SSM_HEADS // SSM_GROUPS * SSM_HEADDIM, SSM_DSTATE)
    z = jnp.zeros_like(hg[:, 0])
    return jnp.concatenate([jnp.concatenate([hg[:, 0], z], -1), jnp.concatenate([z, hg[:, 1]], -1)], 1)


def ssm_state_from_lanes(hl):
    n = hl.shape[0]
    r = hl.reshape(n, SSM_GROUPS, SSM_INNER // SSM_GROUPS, SSM_GROUPS, SSM_DSTATE)
    h = jnp.stack([r[:, g, :, g, :] for g in range(SSM_GROUPS)], 1)
    return h.reshape(n, SSM_HEADS, SSM_HEADDIM, SSM_DSTATE)


ADA_TN = 1536
VMEM_LIMIT = 56 * 1024 * 1024


def _ada_kernel(c_ref, w_ref, b_ref, o_ref):
    s = _silu(c_ref[...])
    o_ref[...] = jnp.dot(s.astype(bf16), w_ref[...].astype(bf16), preferred_element_type=f32) + b_ref[...]


def ada_mod(cvec, w_ada, b_ada, layer):
    n = 6 * D_MODEL
    out = pl.pallas_call(
        _ada_kernel,
        grid=(n // ADA_TN,),
        in_specs=[
            pl.BlockSpec((SUBLANES, D_MODEL), lambda j: (0, 0)),
            pl.BlockSpec((None, D_MODEL, ADA_TN), lambda j: (layer, 0, j)),
            pl.BlockSpec((None, 1, ADA_TN), lambda j: (layer, 0, j)),
        ],
        out_specs=pl.BlockSpec((SUBLANES, ADA_TN), lambda j: (0, j)),
        out_shape=jax.ShapeDtypeStruct((SUBLANES, n), f32),
        compiler_params=pltpu.CompilerParams(dimension_semantics=("arbitrary",), vmem_limit_bytes=VMEM_LIMIT),
        name="ada_mod",
    )(cvec, w_ada, b_ada)
    return out.reshape(SUBLANES, 6, D_MODEL)


MOD_SH1, MOD_SC1, MOD_G1, MOD_SH2, MOD_SC2, MOD_G2 = range(6)


def _rms(x):
    return x * lax.rsqrt(jnp.mean(x * x, axis=-1, keepdims=True) + EPS)


PROJ_CHUNK = 512


def _pick_rows(i, n_first, refs):
    if len(refs) == 1:
        return refs[0][...]
    return jnp.where(i < n_first, refs[0][...], refs[1][...])


def _row_specs(lay, x, width):
    if not isinstance(x, tuple):
        return [pl.BlockSpec((TB, width), lambda i, mod: (jnp.minimum(i, lay.n_blk - 1), 0))], [x]
    nbc, nbl = lay.m_ctx // TB, lay.m_lat // TB
    return ([pl.BlockSpec((TB, width), lambda i, mod: (jnp.minimum(i, nbc - 1), 0)),
             pl.BlockSpec((TB, width), lambda i, mod: (jnp.clip(i - nbc, 0, nbl - 1), 0))], list(x))


def _in_proj_kernel(mod_ref, *refs, has_res, n_x, n_first):
    del mod_ref
    i = pl.program_id(0)
    x_refs, refs = refs[:n_x], refs[n_x:]
    if has_res:
        y_ref, pmods_ref, mods_ref, n1_ref, w_ref, proj_ref, xo_ref, h_scr = refs
        x = _pick_rows(i, n_first, x_refs) + pmods_ref[MOD_G2:MOD_G2 + 1, :] * y_ref[...]
        xo_ref[...] = x
    else:
        mods_ref, n1_ref, w_ref, proj_ref, h_scr = refs
        x = _pick_rows(i, n_first, x_refs)

    @pl.when(i == 0)
    def _():
        h_scr[1] = jnp.zeros((TB, D_MODEL), bf16)

    hb = h_scr[(i + 1) % 2]
    for c0 in range(0, PROJ_W, PROJ_CHUNK):
        wd = min(PROJ_CHUNK, PROJ_W - c0)
        proj_ref[:, c0:c0 + wd] = jnp.dot(hb, w_ref[:, c0:c0 + wd], preferred_element_type=f32)
    h = _rms(x) * n1_ref[...]
    h = h * (1.0 + mods_ref[MOD_SC1:MOD_SC1 + 1, :]) + mods_ref[MOD_SH1:MOD_SH1 + 1, :]
    h_scr[i % 2] = h.astype(bf16)


def in_proj(lay, x, mods, norm1, w_in_r, layer, res=None):
    last = lay.n_blk - 1
    row_map = lambda i, mod: (jnp.minimum(i, last), 0)
    mod_map = lambda i, mod: (mod[jnp.minimum(i, last)], 0, 0)
    in_specs, args = _row_specs(lay, x, D_MODEL)
    n_x = len(args)
    if res is not None:
        in_specs += [pl.BlockSpec((TB, D_MODEL), row_map), pl.BlockSpec((None, 6, D_MODEL), mod_map)]
        args += [res[0], res[1]]
    in_specs += [
        pl.BlockSpec((None, 6, D_MODEL), mod_map),
        pl.BlockSpec((None, 1, D_MODEL), lambda i, mod: (layer, 0, 0)),
        pl.BlockSpec((D_MODEL, PROJ_W), lambda i, mod: (0, 0), pipeline_mode=pl.Buffered(1)),
    ]
    args += [mods, norm1, w_in_r]
    out_specs = [pl.BlockSpec((TB, PROJ_W), lambda i, mod: (jnp.maximum(i - 1, 0), 0))]
    out_shape = [jax.ShapeDtypeStruct((lay.m, PROJ_W), f32)]
    if res is not None:
        out_specs.append(pl.BlockSpec((TB, D_MODEL), row_map))
        out_shape.append(jax.ShapeDtypeStruct((lay.m, D_MODEL), f32))
    outs = pl.pallas_call(
        functools.partial(_in_proj_kernel, has_res=res is not None, n_x=n_x, n_first=lay.m_ctx // TB),
        grid_spec=pltpu.PrefetchScalarGridSpec(
            num_scalar_prefetch=1, grid=(lay.n_blk + 1,), in_specs=in_specs, out_specs=out_specs,
            scratch_shapes=[pltpu.VMEM((2, TB, D_MODEL), bf16)]),
        out_shape=out_shape,
        compiler_params=pltpu.CompilerParams(dimension_semantics=("arbitrary",), vmem_limit_bytes=VMEM_LIMIT),
        name="in_proj",
    )(jnp.asarray(lay.mod), *args)
    return (outs[0], outs[1]) if res is not None else (outs[0], x)


MLA_QK = MLA_NOPE + MLA_ROPE
Q_HEADS_W = MLA_HEADS * LANES
ROPE_HALF = MLA_ROPE // 2


def _rope_lanes(x, c, s1, s2):
    return x * c + pltpu.roll(x, ROPE_HALF, 1) * s1 + pltpu.roll(x, LANES - ROPE_HALF, 1) * s2


def _mla_prep_kernel(pos_ref, mq_ref, ckv_ref, small_ref, qn_ref, kvn_ref, wuq_ref, rc_ref, rs1_ref, rs2_ref,
                     q_ref, ckvn_ref, kpe_ref):
    del pos_ref
    c, s1, s2 = rc_ref[...], rs1_ref[...], rs2_ref[...]
    cq = (_rms(mq_ref[...]) * qn_ref[...]).astype(bf16)
    qf = jnp.dot(cq, wuq_ref[...], preferred_element_type=f32)
    scale = MLA_QK ** -0.5 * math.log2(math.e)
    for h in range(MLA_HEADS):
        xh = qf[:, h * LANES:(h + 1) * LANES]
        q_ref[:, h * LANES:(h + 1) * LANES] = (_rope_lanes(xh, c, s1, s2) * scale).astype(bf16)
    ckvn_ref[...] = _rms(ckv_ref[...]) * kvn_ref[...]
    lane = lax.broadcasted_iota(jnp.int32, (TB, LANES), 1)
    in_rope = jnp.logical_and(lane >= SM_KPE, lane < SM_KPE + MLA_ROPE)
    kpe_ref[...] = jnp.where(in_rope, _rope_lanes(small_ref[...], c, s1, s2), 0.0)


def rope_tables(lay):
    t = lay.t_lat
    rows = t // GRID_W
    row = jnp.repeat(jnp.arange(rows, dtype=f32), GRID_W)
    col = jnp.tile(jnp.arange(GRID_W, dtype=f32), rows)
    nf = MLA_ROPE // 4
    inv = jnp.power(ROPE_THETA, -jnp.arange(nf, dtype=f32) / nf)
    ang = jnp.concatenate([row[:, None] * inv, col[:, None] * inv], axis=-1)
    cos, sin = jnp.cos(ang), jnp.sin(ang)
    z = lambda n: jnp.zeros((t, n), f32)
    c = jnp.concatenate([jnp.ones((t, MLA_NOPE), f32), cos, cos, z(LANES - MLA_QK)], -1)
    s1 = jnp.concatenate([z(MLA_NOPE + ROPE_HALF), sin, z(LANES - MLA_QK)], -1)
    s2 = jnp.concatenate([z(MLA_NOPE), -sin, z(ROPE_HALF + LANES - MLA_QK)], -1)
    ident = jnp.concatenate([jnp.ones((TB, MLA_QK), f32), jnp.zeros((TB, LANES - MLA_QK), f32)], -1)
    zero = jnp.zeros((TB, LANES), f32)
    return (jnp.concatenate([ident, c]), jnp.concatenate([zero, s1]), jnp.concatenate([zero, s2]))


def _pos_blocks(lay):
    pos = [0] * (lay.m_ctx // TB)
    for _ in range(lay.n_lat):
        pos += [1 + j for j in range(lay.t_lat // TB)]
    return np.array(pos + pos[-1:], np.int32)


def mla_prep(lay, proj, q_norm, kv_norm, w_uq_r, tables, layer):
    row_map = lambda i, pos: (i, 0)
    tab_map = lambda i, pos: (pos[i], 0)
    return pl.pallas_call(
        _mla_prep_kernel,
        grid_spec=pltpu.PrefetchScalarGridSpec(
            num_scalar_prefetch=1, grid=(lay.n_blk,),
            in_specs=[
                pl.BlockSpec((TB, Q_RANK), lambda i, pos: (i, COL_MQ // Q_RANK)),
                pl.BlockSpec((TB, KV_RANK), lambda i, pos: (i, COL_CKV // KV_RANK)),
                pl.BlockSpec((TB, LANES), lambda i, pos: (i, COL_SMALL // LANES)),
                pl.BlockSpec((None, 1, Q_RANK), lambda i, pos: (layer, 0, 0)),
                pl.BlockSpec((None, 1, KV_RANK), lambda i, pos: (layer, 0, 0)),
                pl.BlockSpec((Q_RANK, Q_HEADS_W), lambda i, pos: (0, 0)),
                pl.BlockSpec((TB, LANES), tab_map),
                pl.BlockSpec((TB, LANES), tab_map),
                pl.BlockSpec((TB, LANES), tab_map),
            ],
            out_specs=[
                pl.BlockSpec((TB, Q_HEADS_W), row_map),
                pl.BlockSpec((TB, KV_RANK), row_map),
                pl.BlockSpec((TB, LANES), row_map),
            ]),
        out_shape=[
            jax.ShapeDtypeStruct((lay.m, Q_HEADS_W), bf16),
            jax.ShapeDtypeStruct((lay.m, KV_RANK), f32),
            jax.ShapeDtypeStruct((lay.m, LANES), f32),
        ],
        compiler_params=pltpu.CompilerParams(dimension_semantics=("arbitrary",)),
        name="mla_prep",
    )(jnp.asarray(_pos_blocks(lay)), proj, proj, proj, q_norm, kv_norm, w_uq_r, *tables)


V_PAIRS = MLA_HEADS // 2
KV_UP_W = 2 * MLA_HEADS * LANES


def _kv_up_kernel(ckv_ref, kpe_ref, w_ref, k_ref, v_ref):
    kv = jnp.dot(ckv_ref[...].astype(bf16), w_ref[...], preferred_element_type=f32)
    kpe = kpe_ref[...]
    for h in range(MLA_HEADS):
        k_ref[h] = (kv[:, h * LANES:(h + 1) * LANES] + kpe).astype(bf16)
    v0 = MLA_HEADS * LANES
    lane_lo = lax.broadcasted_iota(jnp.int32, (TB, LANES), 1) < MLA_VDIM
    for h in range(MLA_HEADS):
        v_ref[h] = jnp.where(lane_lo, kv[:, v0 + h * LANES:v0 + (h + 1) * LANES], 1.0).astype(bf16)


def kv_up(ckv, kpe, w_ukv_r):
    r = ckv.shape[0]
    return pl.pallas_call(
        _kv_up_kernel,
        grid=(r // TB,),
        in_specs=[
            pl.BlockSpec((TB, KV_RANK), lambda i: (i, 0)),
            pl.BlockSpec((TB, LANES), lambda i: (i, 0)),
            pl.BlockSpec((KV_RANK, KV_UP_W), lambda i: (0, 0)),
        ],
        out_specs=[
            pl.BlockSpec((MLA_HEADS, TB, LANES), lambda i: (0, i, 0)),
            pl.BlockSpec((MLA_HEADS, TB, LANES), lambda i: (0, i, 0)),
        ],
        out_shape=[
            jax.ShapeDtypeStruct((MLA_HEADS, r, LANES), bf16),
            jax.ShapeDtypeStruct((MLA_HEADS, r, LANES), bf16),
        ],
        compiler_params=pltpu.CompilerParams(dimension_semantics=("arbitrary",)),
        name="kv_up",
    )(ckv, kpe, w_ukv_r)


ATT_KC = 512


def _attn_kernel(q_ref, k_ref, v_ref, o_ref, m_ref, acc_ref, *, n_keys):
    lane_lo = lax.broadcasted_iota(jnp.int32, (TB, LANES), 1) < MLA_VDIM
    kc = min(ATT_KC, n_keys)
    hs = range(MLA_HEADS)
    m_ref[...] = jnp.full((MLA_HEADS, TB, LANES), -jnp.inf, f32)
    acc_ref[...] = jnp.zeros((MLA_HEADS, TB, LANES), f32)

    def body(c, carry):
        c0 = pl.multiple_of(c * kc, kc)
        score = lambda h: lax.dot_general(q_ref[:, h * LANES:(h + 1) * LANES], k_ref[h, pl.ds(c0, kc), :],
                                          (((1,), (1,)), ((), ())), preferred_element_type=f32)
        s_next = score(0)
        for h in hs:
            s = s_next
            if h + 1 < MLA_HEADS:
                s_next = score(h + 1)
            m_old = m_ref[h]
            m_new = jnp.maximum(m_old, jnp.max(s, axis=-1, keepdims=True))
            alpha = jnp.exp2(m_old - m_new)
            pe = jnp.exp2(s - m_new[:, 0:1])
            pv = jnp.dot(pe.astype(bf16), v_ref[h, pl.ds(c0, kc), :], preferred_element_type=f32)
            acc_ref[h] = alpha * acc_ref[h] + pv
            m_ref[h] = m_new
        return carry

    lax.fori_loop(0, n_keys // kc, body, 0)
    half = LANES // 2
    for p in range(V_PAIRS):
        a_even = acc_ref[2 * p]
        a_odd = pltpu.roll(acc_ref[2 * p + 1], half, 1)
        o_even = a_even * pltpu.roll(1.0 / a_even, half, 1)
        o_odd = a_odd * pltpu.roll(1.0 / a_odd, half, 1)
        o_ref[:, p * LANES:(p + 1) * LANES] = jnp.where(lane_lo, o_even, o_odd)


def attention(q, k, v, q_blk0, n_seq, t_q, n_keys):
    qb = t_q // TB
    return pl.pallas_call(
        functools.partial(_attn_kernel, n_keys=n_keys),
        grid=(n_seq, qb),
        in_specs=[
            pl.BlockSpec((TB, Q_HEADS_W), lambda b, j: (q_blk0 + b * qb + j, 0)),
            pl.BlockSpec((MLA_HEADS, n_keys, LANES), lambda b, j: (0, b, 0)),
            pl.BlockSpec((MLA_HEADS, n_keys, LANES), lambda b, j: (0, b, 0)),
        ],
        out_specs=pl.BlockSpec((TB, MLA_W), lambda b, j: (b * qb + j, 0)),
        out_shape=jax.ShapeDtypeStruct((n_seq * t_q, MLA_W), f32),
        scratch_shapes=[pltpu.VMEM((MLA_HEADS, TB, LANES), f32)] * 2,
        compiler_params=pltpu.CompilerParams(dimension_semantics=("arbitrary", "arbitrary"),
                                             vmem_limit_bytes=VMEM_LIMIT),
        name="mla_attn",
    )(q, k, v)


N_PAIRS = E_PER_GROUP * (E_PER_GROUP - 1) // 2
N_BUCKETS = N_EGROUPS * N_PAIRS
RT_BUCKET, RT_WLO, RT_WHI = 0, 1, 2
H2X_W = D_MODEL + LANES


def _route(lg):
    lane = lax.broadcasted_iota(jnp.int32, lg.shape, 1)
    big = jnp.int32(LANES)
    rmax = lambda x: jnp.max(x, axis=-1, keepdims=True)
    rmin = lambda x: jnp.min(x, axis=-1, keepdims=True)
    rsum = lambda x: jnp.sum(x, axis=-1, keepdims=True)
    neg = -jnp.inf
    gmask = lane < N_EGROUPS
    gl = jnp.where(gmask, lg, neg)
    gmax = rmax(gl)
    gsum = rsum(jnp.where(gmask, jnp.exp(lg - gmax), 0.0))
    gsel = rmin(jnp.where(gl == gmax, lane, big))
    pg_sel = 1.0 / gsum
    e0 = N_EGROUPS + E_PER_GROUP * gsel
    emask = jnp.logical_and(lane >= e0, lane < e0 + E_PER_GROUP)
    el = jnp.where(emask, lg, neg)
    emax = rmax(el)
    ee = jnp.where(emask, jnp.exp(lg - emax), 0.0)
    pe = ee / rsum(ee)
    pe_m = jnp.where(emask, pe, -1.0)
    v1 = rmax(pe_m)
    i1 = rmin(jnp.where(pe_m == v1, lane, big))
    pe_m2 = jnp.where(lane == i1, -1.0, pe_m)
    v2 = rmax(pe_m2)
    i2 = rmin(jnp.where(pe_m2 == v2, lane, big))
    w1 = pg_sel * v1 / (v1 + v2)
    w2 = pg_sel * v2 / (v1 + v2)
    a1, a2 = i1 - e0, i2 - e0
    lo, hi = jnp.minimum(a1, a2), jnp.maximum(a1, a2)
    pair = (lo * (2 * E_PER_GROUP - 1 - lo)) // 2 + (hi - lo - 1)
    bucket = (gsel * N_PAIRS + pair).astype(f32)
    w_lo = jnp.where(a1 < a2, w1, w2)
    w_hi = jnp.where(a1 < a2, w2, w1)
    return jnp.where(lane == RT_BUCKET, bucket,
                     jnp.where(lane == RT_WLO, w_lo, jnp.where(lane == RT_WHI, w_hi, 0.0)))


def _out_proj_kernel(mod_ref, *refs, n_x, n_first):
    del mod_ref
    x_refs = refs[:n_x]
    (of_ref, ob_ref, gz_ref, yf_ref, yb_ref, sz_ref, om_ref, mods_ref,
     gn_ref, sn_ref, n2_ref, wout_ref, wr_ref, br_ref, x1_ref, h2_ref, rt_ref) = refs[n_x:]
    acc = None
    for h in range(GDN_HEADS):
        sl = slice(h * GDN_DV, (h + 1) * GDN_DV)
        o = _rms(of_ref[:, sl] + ob_ref[:, sl]) * gn_ref[...] * _silu(gz_ref[:, sl])
        t = jnp.dot(o.astype(bf16), wout_ref[sl, :], preferred_element_type=f32)
        acc = t if acc is None else acc + t
    y = (yf_ref[...] + yb_ref[...]) * _silu(sz_ref[...])
    ys = _rms(y) * sn_ref[...]
    acc = acc + jnp.dot(ys.astype(bf16), wout_ref[GDN_W:GDN_W + SSM_INNER, :], preferred_element_type=f32)
    acc = acc + jnp.dot(om_ref[...].astype(bf16), wout_ref[GDN_W + SSM_INNER:MIX_W, :],
                        preferred_element_type=f32)
    x1 = _pick_rows(pl.program_id(0), n_first, x_refs) + mods_ref[MOD_G1:MOD_G1 + 1, :] * acc
    x1_ref[...] = x1
    h2 = _rms(x1) * n2_ref[...]
    h2 = h2 * (1.0 + mods_ref[MOD_SC2:MOD_SC2 + 1, :]) + mods_ref[MOD_SH2:MOD_SH2 + 1, :]
    rt = _route(_mm3(h2, wr_ref[...]) + br_ref[...])
    h2_ref[:, 0:D_MODEL] = h2
    h2_ref[:, D_MODEL:H2X_W] = rt
    rt_ref[...] = rt


def out_proj(lay, o_f, o_b, proj, y_f, y_b, o_mla, x, mods, gdn_norm, ssm_norm, norm2, w_out_b, w_rt, b_rt, layer):
    row_map = lambda i, mod: (i, 0)
    lyr = lambda i, mod: (layer, 0, 0)
    x_specs, x_args = _row_specs(lay, x, D_MODEL)
    return pl.pallas_call(
        functools.partial(_out_proj_kernel, n_x=len(x_args), n_first=lay.m_ctx // TB),
        grid_spec=pltpu.PrefetchScalarGridSpec(
            num_scalar_prefetch=1, grid=(lay.n_blk,),
            in_specs=x_specs + [
                pl.BlockSpec((TB, GDN_W), row_map),
                pl.BlockSpec((TB, GDN_W), row_map),
                pl.BlockSpec((TB, GDN_W), lambda i, mod: (i, COL_GZ // GDN_W)),
                pl.BlockSpec((TB, SSM_INNER), row_map),
                pl.BlockSpec((TB, SSM_INNER), row_map),
                pl.BlockSpec((TB, SSM_INNER), lambda i, mod: (i, COL_SZ // SSM_INNER)),
                pl.BlockSpec((TB, MLA_W), row_map),
                pl.BlockSpec((None, 6, D_MODEL), lambda i, mod: (mod[i], 0, 0)),
                pl.BlockSpec((None, 1, GDN_DV), lyr),
                pl.BlockSpec((None, 1, SSM_INNER), lyr),
                pl.BlockSpec((None, 1, D_MODEL), lyr),
                pl.BlockSpec((None, MIX_W, D_MODEL), lyr, pipeline_mode=pl.Buffered(1)),
                pl.BlockSpec((None, D_MODEL, LANES), lyr),
                pl.BlockSpec((None, 1, LANES), lyr),
            ],
            out_specs=[
                pl.BlockSpec((TB, D_MODEL), row_map),
                pl.BlockSpec((TB, H2X_W), row_map),
                pl.BlockSpec((TB, LANES), row_map),
            ]),
        out_shape=[
            jax.ShapeDtypeStruct((lay.m, D_MODEL), f32),
            jax.ShapeDtypeStruct((lay.m, H2X_W), f32),
            jax.ShapeDtypeStruct((lay.m, LANES), f32),
        ],
        compiler_params=pltpu.CompilerParams(dimension_semantics=("arbitrary",), vmem_limit_bytes=VMEM_LIMIT),
        name="out_proj",
    )(jnp.asarray(lay.mod), *x_args, o_f, o_b, proj, y_f, y_b, proj, o_mla, mods, gdn_norm, ssm_norm, norm2,
      w_out_b, w_rt, b_rt)


TME = 256
PAIR_LO = (0, 0, 0, 1, 1, 2)
PAIR_HI = (1, 2, 3, 2, 3, 3)


DMA_UNROLL = 8


def _moe_kernel(te_ref, nu_ref, ts_ref, nv_ref, order_ref,
                h2_hbm, wg_ref, wu_ref, wd_ref, y_hbm, xbuf, ybuf, gsem, ssem):
    del te_ref
    t = pl.program_id(0)
    j = pl.program_id(1)
    n_used = nu_ref[0]
    used = t < n_used
    slot = t % 2
    n_tok = y_hbm.shape[0] - TME

    def gather_rows(tile, sl):
        base = ts_ref[tile]

        def issue(r, c):
            pltpu.make_async_copy(h2_hbm.at[pl.ds(order_ref[base + r], 1), :],
                                  xbuf.at[sl, pl.ds(r, 1), :], gsem.at[sl]).start()
            return c
        lax.fori_loop(0, TME, issue, 0, unroll=DMA_UNROLL)

    def scatter_rows(tile, sl):
        base = ts_ref[tile]
        nvalid = nv_ref[tile]

        def issue(r, c):
            row = jnp.where(r < nvalid, order_ref[base + r], n_tok + r)
            pltpu.make_async_copy(ybuf.at[sl, pl.ds(r, 1), :],
                                  y_hbm.at[pl.ds(row, 1), :], ssem.at[sl]).start()
            return c
        lax.fori_loop(0, TME, issue, 0, unroll=DMA_UNROLL)

    def wait_gather(sl):
        pltpu.make_async_copy(h2_hbm.at[pl.ds(0, TME), :], xbuf.at[sl], gsem.at[sl]).wait()

    def wait_scatter(sl):
        pltpu.make_async_copy(ybuf.at[sl], y_hbm.at[pl.ds(0, TME), :], ssem.at[sl]).wait()

    @pl.when(jnp.logical_and(used, jnp.logical_and(t == 0, j == 0)))
    def _():
        gather_rows(0, 0)
        ybuf[1] = jnp.zeros((TME, D_MODEL), f32)
        spare = pltpu.make_async_copy(ybuf.at[1], y_hbm.at[pl.ds(n_tok, TME), :], ssem.at[1])
        spare.start()
        spare.wait()

    @pl.when(jnp.logical_and(used, j == 0))
    def _():
        wait_gather(slot)

    @pl.when(jnp.logical_and(t + 1 < n_used, j == 1))
    def _():
        gather_rows(t + 1, 1 - slot)

    @pl.when(used)
    def _():
        xb = xbuf[slot, :, 0:D_MODEL].astype(bf16)
        hg = jnp.dot(xb, wg_ref[...], preferred_element_type=f32)
        hu = jnp.dot(xb, wu_ref[...], preferred_element_type=f32)
        hdn = (_silu(hg) * hu).astype(bf16)
        yo = jnp.dot(hdn, wd_ref[...], preferred_element_type=f32)
        lane = lax.broadcasted_iota(jnp.int32, (TME, LANES), 1)
        w_lane = jnp.where(j == slot, RT_WLO, RT_WHI)
        wsel = jnp.sum(jnp.where(lane == w_lane, xbuf[slot, :, D_MODEL:H2X_W], 0.0), axis=-1, keepdims=True)

        @pl.when(j == 0)
        def _():
            ybuf[slot] = wsel * yo

        @pl.when(j == 1)
        def _():
            ybuf[slot] = ybuf[slot] + wsel * yo

    @pl.when(jnp.logical_and(used, j == 1))
    def _():
        @pl.when(t > 0)
        def _():
            wait_scatter(1 - slot)

        scatter_rows(t, slot)

        @pl.when(t == n_used - 1)
        def _():
            wait_scatter(slot)


def moe_dispatch(lay, route):
    m = lay.m
    n_tiles = m // TME + N_BUCKETS
    bucket = route[:, RT_BUCKET].astype(jnp.int32)
    order = jnp.argsort(bucket).astype(jnp.int32)
    counts = jnp.sum(bucket[:, None] == jnp.arange(N_BUCKETS, dtype=jnp.int32)[None, :], axis=0, dtype=jnp.int32)
    ptiles = (counts + TME - 1) // TME
    pend = jnp.cumsum(ptiles)
    pstart = pend - ptiles
    start = jnp.cumsum(counts) - counts
    n_used = pend[-1]
    tid = jnp.arange(n_tiles, dtype=jnp.int32)
    tid_c = jnp.minimum(tid, n_used - 1)
    tb = jnp.minimum(jnp.sum(pend[None, :] <= tid_c[:, None], axis=1, dtype=jnp.int32), N_BUCKETS - 1)
    sel = tb[:, None] == jnp.arange(N_BUCKETS, dtype=jnp.int32)[None, :]
    pick = lambda v: jnp.sum(jnp.where(sel, v[None, :], 0), axis=1, dtype=jnp.int32)
    off = (tid_c - pick(pstart)) * TME
    tstart = pick(start) + off
    nvalid = jnp.where(tid < n_used, jnp.clip(pick(counts) - off, 0, TME), 0).astype(jnp.int32)
    order_p = jnp.concatenate([order, jnp.zeros((TME,), jnp.int32)])
    grp, pr = tb // N_PAIRS, tb % N_PAIRS
    e_lo = grp * E_PER_GROUP + jnp.asarray(PAIR_LO, jnp.int32)[pr]
    e_hi = grp * E_PER_GROUP + jnp.asarray(PAIR_HI, jnp.int32)[pr]
    swap = (tid % 2) == 1
    te = jnp.stack([jnp.where(swap, e_hi, e_lo), jnp.where(swap, e_lo, e_hi)], -1).reshape(-1)
    te = jnp.concatenate([te, te[-2:]])
    return (te.astype(jnp.int32), n_used.reshape(1).astype(jnp.int32), tstart.astype(jnp.int32), nvalid,
            order_p, n_tiles)


def moe_apply(lay, h2, route, wg_b, wu_b, wd_b, layer):
    te, n_used, tstart, nvalid, order_p, n_tiles = moe_dispatch(lay, route)
    wmap = lambda t, j, te, *_: (layer, te[2 * t + j], 0, 0)
    return pl.pallas_call(
        _moe_kernel,
        grid_spec=pltpu.PrefetchScalarGridSpec(
            num_scalar_prefetch=5, grid=(n_tiles, 2),
            in_specs=[
                pl.BlockSpec(memory_space=pl.ANY),
                pl.BlockSpec((None, None, D_MODEL, EXPERT_FF), wmap),
                pl.BlockSpec((None, None, D_MODEL, EXPERT_FF), wmap),
                pl.BlockSpec((None, None, EXPERT_FF, D_MODEL), wmap),
            ],
            out_specs=pl.BlockSpec(memory_space=pl.ANY),
            scratch_shapes=[
                pltpu.VMEM((2, TME, H2X_W), f32),
                pltpu.VMEM((2, TME, D_MODEL), f32),
                pltpu.SemaphoreType.DMA((2,)),
                pltpu.SemaphoreType.DMA((2,)),
            ]),
        out_shape=jax.ShapeDtypeStruct((lay.m + TME, D_MODEL), f32),
        compiler_params=pltpu.CompilerParams(dimension_semantics=("arbitrary", "arbitrary"),
                                             vmem_limit_bytes=VMEM_LIMIT),
        name="moe",
    )(te, n_used, tstart, nvalid, order_p, h2, wg_b, wu_b, wd_b)


def _final_kernel(mod_ref, x_ref, y_ref, mods_ref, g_ref, oc_ref, ol_ref, *, n_first):
    del mod_ref
    i = pl.program_id(0)
    x = x_ref[...] + mods_ref[MOD_G2:MOD_G2 + 1, :] * y_ref[...]
    out = _rms(x) * g_ref[...]

    @pl.when(i < n_first)
    def _():
        oc_ref[...] = out

    @pl.when(i >= n_first)
    def _():
        ol_ref[...] = out


def final_norm_apply(lay, x, ytok, mods, g):
    row_map = lambda i, mod: (i, 0)
    nbc = lay.m_ctx // TB
    return pl.pallas_call(
        functools.partial(_final_kernel, n_first=nbc),
        grid_spec=pltpu.PrefetchScalarGridSpec(
            num_scalar_prefetch=1, grid=(lay.n_blk,),
            in_specs=[
                pl.BlockSpec((TB, D_MODEL), row_map),
                pl.BlockSpec((TB, D_MODEL), row_map),
                pl.BlockSpec((None, 6, D_MODEL), lambda i, mod: (mod[i], 0, 0)),
                pl.BlockSpec((1, D_MODEL), lambda i, mod: (0, 0)),
            ],
            out_specs=[pl.BlockSpec((TB, D_MODEL), lambda i, mod: (jnp.minimum(i, nbc - 1), 0)),
                       pl.BlockSpec((TB, D_MODEL), lambda i, mod: (jnp.maximum(i - nbc, 0), 0))]),
        out_shape=[jax.ShapeDtypeStruct((lay.m_ctx, D_MODEL), f32),
                   jax.ShapeDtypeStruct((lay.m_lat, D_MODEL), f32)],
        compiler_params=pltpu.CompilerParams(dimension_semantics=("arbitrary",)),
        name="final_norm",
    )(jnp.asarray(lay.mod), x, ytok, mods, g)


def _reorder_w_in(w):
    g_qkv, g_z, g_a, g_b, s_z, s_xbc, s_dt, m_q, m_kv = jnp.split(w, np.cumsum(IN_SIZES)[:-1], axis=1)
    ckv, kpe = m_kv[:, :KV_RANK], m_kv[:, KV_RANK:]
    z = lambda n: jnp.zeros((w.shape[0], n), w.dtype)
    small = jnp.concatenate([g_a, g_b, s_dt, z(SM_KPE - SM_DT - 2 * SSM_HEADS), kpe, z(LANES - SM_KPE - MLA_ROPE)], 1)
    return jnp.concatenate([g_qkv, g_z, s_z, m_q, ckv, s_xbc, small], 1).astype(bf16)


def _reorder_w_uq(w):
    r = w.reshape(Q_RANK, MLA_HEADS, MLA_QK)
    return jnp.pad(r, ((0, 0), (0, 0), (0, LANES - MLA_QK))).reshape(Q_RANK, Q_HEADS_W).astype(bf16)


def _reorder_w_ukv(w):
    r = w.reshape(KV_RANK, MLA_HEADS, MLA_NOPE + MLA_VDIM)
    k = jnp.pad(r[:, :, :MLA_NOPE], ((0, 0), (0, 0), (0, LANES - MLA_NOPE))).reshape(KV_RANK, MLA_HEADS * LANES)
    v = jnp.pad(r[:, :, MLA_NOPE:], ((0, 0), (0, 0), (0, LANES - MLA_VDIM))).reshape(KV_RANK, MLA_HEADS * LANES)
    return jnp.concatenate([k, v], 1).astype(bf16)


def _lane_row(vals, lane0):
    n = vals.shape[-1]
    return jnp.pad(vals, ((0, 0), (lane0, LANES - lane0 - n)))


def _par_rows(row0, row1):
    z = jnp.zeros_like(row0)
    return jnp.stack([row0, row1] + [z] * (SUBLANES - 2), axis=1)


def kernel(x_prompt, x_sample, cache_ckv, cache_kpe, state_gdn, state_ssm, c, c_ctx, w_ada, b_ada, norm1, norm2, w_in, gdn_conv, gdn_A_log, gdn_dt_bias, gdn_norm, ssm_conv, ssm_conv_bias, ssm_A_log, ssm_dt_bias, ssm_D, ssm_norm, mla_q_norm, mla_w_uq, mla_kv_norm, mla_w_ukv, w_out, router_group, router_group_bias, router_expert, router_expert_bias, moe_w_gate, moe_w_up, moe_w_down, final_norm):
    n_ctx, t_ctx, _ = x_prompt.shape
    n_lat, t_lat, _ = x_sample.shape
    past = cache_ckv.shape[2]
    depth = w_in.shape[0]
    assert n_lat < SUBLANES and t_ctx % TB == 0 and t_lat % TB == 0 and past % TB == 0
    lay = Layout(n_ctx, t_ctx, n_lat, t_lat)
    m_ctx = lay.m_ctx

    x = (x_prompt.reshape(m_ctx, D_MODEL), x_sample.reshape(lay.m_lat, D_MODEL))
    cvec = jnp.concatenate([c_ctx[None], c, jnp.zeros((SUBLANES - 1 - n_lat, D_MODEL), f32)], 0)
    tables = rope_tables(lay)

    b_ada3 = b_ada[:, None, :]
    norm1_3, norm2_3 = norm1[:, None, :], norm2[:, None, :]
    gdn_par = _par_rows(_lane_row(gdn_A_log.reshape(depth, -1), SM_GA), _lane_row(gdn_dt_bias.reshape(depth, -1), SM_GA))
    ssm_par = _par_rows(_lane_row(ssm_A_log.reshape(depth, -1), SM_DT), _lane_row(ssm_dt_bias.reshape(depth, -1), SM_DT))
    d_lanes = jnp.pad(jnp.repeat(ssm_D, SSM_HEADDIM, axis=-1), ((0, 0), (0, SSM_XBC - SSM_INNER)))
    ssm_cpar = _par_rows(ssm_conv_bias, d_lanes)
    w_rt = jnp.pad(jnp.concatenate([router_group, router_expert], -1),
                   ((0, 0), (0, 0), (0, LANES - N_EGROUPS - N_EXPERTS)))
    b_rt = jnp.pad(jnp.concatenate([router_group_bias, router_expert_bias], -1),
                   ((0, 0), (0, LANES - N_EGROUPS - N_EXPERTS)))[:, None, :]
    kpe_cache = jnp.pad(cache_kpe, ((0, 0), (0, 0), (0, 0), (SM_KPE, LANES - SM_KPE - MLA_ROPE)))

    w_out_b = w_out.astype(bf16)
    wg_b, wu_b, wd_b = moe_w_gate.astype(bf16), moe_w_up.astype(bf16), moe_w_down.astype(bf16)

    ckvs, kpes, gdns, ssms = [], [], [], []
    ytok = mods_prev = None
    for l in range(depth):
        mods = ada_mod(cvec, w_ada, b_ada3, l)
        res = None if l == 0 else (ytok, mods_prev)
        proj, x = in_proj(lay, x, mods, norm1_3, _reorder_w_in(w_in[l]), l, res)

        zg = jnp.zeros((n_ctx, GDN_HEADS, GDN_DK, GDN_DV), f32)
        o_f, sg_f = gdn_scan(lay, proj, gdn_conv, gdn_par, jnp.concatenate([zg, state_gdn[:, l, 0]], 0), l, False)
        o_b, sg_b = gdn_scan(lay, proj, gdn_conv, gdn_par, jnp.concatenate([zg, state_gdn[:, l, 1]], 0), l, True)

        zs = jnp.zeros((n_ctx, SSM_INNER, LANES), f32)
        y_f, hs_f = ssd_scan(lay, proj, ssm_conv, ssm_cpar, ssm_par,
                             jnp.concatenate([zs, ssm_state_to_lanes(state_ssm[:, l, 0])], 0), l, False)
        y_b, hs_b = ssd_scan(lay, proj, ssm_conv, ssm_cpar, ssm_par,
                             jnp.concatenate([zs, ssm_state_to_lanes(state_ssm[:, l, 1])], 0), l, True)

        q, ckvn, kpe = mla_prep(lay, proj, mla_q_norm[:, None, :], mla_kv_norm[:, None, :],
                                _reorder_w_uq(mla_w_uq[l]), tables, l)
        w_ukv_r = _reorder_w_ukv(mla_w_ukv[l])
        k_ctx, v_ctx = kv_up(ckvn[:m_ctx], kpe[:m_ctx], w_ukv_r)
        ckv_lat = jnp.concatenate([cache_ckv[:, l], ckvn[m_ctx:].reshape(n_lat, t_lat, KV_RANK)], 1)
        kpe_lat = jnp.concatenate([kpe_cache[:, l], kpe[m_ctx:].reshape(n_lat, t_lat, LANES)], 1)
        k_lat, v_lat = kv_up(ckv_lat.reshape(-1, KV_RANK), kpe_lat.reshape(-1, LANES), w_ukv_r)
        o_mla = jnp.concatenate([
            attention(q, k_ctx, v_ctx, 0, n_ctx, t_ctx, t_ctx),
            attention(q, k_lat, v_lat, m_ctx // TB, n_lat, t_lat, past + t_lat)], 0)

        x, h2, route = out_proj(lay, o_f, o_b, proj, y_f, y_b, o_mla, x, mods, gdn_norm[:, None, :],
                                ssm_norm[:, None, :], norm2_3, w_out_b, w_rt, b_rt, l)
        ytok = moe_apply(lay, h2, route, wg_b, wu_b, wd_b, l)
        mods_prev = mods

        ckvs.append(ckvn[:m_ctx].reshape(n_ctx, t_ctx, KV_RANK))
        kpes.append(kpe[:m_ctx, SM_KPE:SM_KPE + MLA_ROPE].reshape(n_ctx, t_ctx, MLA_ROPE))
        gdns.append(jnp.stack([sg_f[:n_ctx], sg_b[:n_ctx]], 1))
        ssms.append(jnp.stack([ssm_state_from_lanes(hs_f[:n_ctx]), ssm_state_from_lanes(hs_b[:n_ctx])], 1))

    y_ctx, y_lat = final_norm_apply(lay, x, ytok, mods_prev, final_norm[None, :])
    return (y_ctx.reshape(n_ctx, t_ctx, D_MODEL), y_lat.reshape(n_lat, t_lat, D_MODEL),
            jnp.stack(ckvs, 1), jnp.stack(kpes, 1), jnp.stack(gdns, 1), jnp.stack(ssms, 1))
```

```python
import functools
import math

import numpy as np
import jax
import jax.numpy as jnp
from jax import lax
from jax.experimental import pallas as pl
from jax.experimental.pallas import tpu as pltpu

f32 = jnp.float32
bf16 = jnp.bfloat16

D_MODEL = 2048
DEPTH = 2
GRID_W = 64
EPS = 1e-6
CONV_W = 5
CHUNK = 64

GDN_HEADS = 4
GDN_DK = 128
GDN_DV = 128
GDN_W = GDN_HEADS * GDN_DV
GDN_QKV = 2 * GDN_HEADS * GDN_DK + GDN_HEADS * GDN_DV

SSM_HEADS = 16
SSM_HEADDIM = 64
SSM_GROUPS = 2
SSM_DSTATE = 64
SSM_INNER = SSM_HEADS * SSM_HEADDIM
SSM_XBC = SSM_INNER + 2 * SSM_GROUPS * SSM_DSTATE

MLA_HEADS = 8
MLA_NOPE = 64
MLA_ROPE = 32
MLA_VDIM = 64
Q_RANK = 512
KV_RANK = 256
MLA_W = MLA_HEADS * MLA_VDIM
ROPE_THETA = 10000.0

MIX_W = GDN_W + SSM_INNER + MLA_W
IN_SIZES = (GDN_QKV, GDN_W, 2 * GDN_HEADS, 2 * GDN_HEADS, SSM_INNER, SSM_XBC, 2 * SSM_HEADS, Q_RANK, KV_RANK + MLA_ROPE)

N_EGROUPS = 4
E_PER_GROUP = 4
N_EXPERTS = N_EGROUPS * E_PER_GROUP
EXPERT_FF = 512

LANES = 128
SUBLANES = 8

TB = 256

COL_QKV = 0
COL_GZ = 1536
COL_SZ = 2048
COL_MQ = 3072
COL_CKV = 3584
COL_XBC = 3840
COL_SMALL = 5120
PROJ_W = 5248
SM_GA, SM_GB, SM_DT, SM_KPE = 0, 8, 16, 64


class Layout:
    def __init__(self, n_ctx, t_ctx, n_lat, t_lat):
        self.n_ctx, self.t_ctx, self.n_lat, self.t_lat = n_ctx, t_ctx, n_lat, t_lat
        self.m_ctx = n_ctx * t_ctx
        self.m_lat = n_lat * t_lat
        self.m = self.m_ctx + self.m_lat
        self.n_seq = n_ctx + n_lat
        bc, bl = t_ctx // TB, t_lat // TB
        seq, first, last, mod = [], [], [], []
        for s in range(n_ctx):
            for j in range(bc):
                seq.append(s); first.append(int(j == 0)); last.append(int(j == bc - 1)); mod.append(0)
        for s in range(n_lat):
            for j in range(bl):
                seq.append(n_ctx + s); first.append(int(j == 0)); last.append(int(j == bl - 1)); mod.append(1 + s)
        self.n_blk = len(seq)
        self.seq = np.array(seq, np.int32)
        self.first = np.array(first, np.int32)
        self.last = np.array(last, np.int32)
        self.mod = np.array(mod + mod[-1:], np.int32)
        fwd = np.arange(self.n_blk, dtype=np.int32)
        bwd = []
        i = 0
        while i < self.n_blk:
            j = i
            while self.last[j] == 0:
                j += 1
            bwd.extend(range(j, i - 1, -1))
            i = j + 1
        self.order = {False: fwd, True: np.array(bwd, np.int32)}

    def scan_tables(self, rev):
        order = self.order[rev]
        pad = lambda a: jnp.asarray(np.concatenate([a, a[-1:]]))
        return pad(order), pad(self.seq[order]), pad(self.first[order]), pad(self.last[order])


def _mm(a, b):
    return jnp.dot(a.astype(bf16), b.astype(bf16), preferred_element_type=f32)


def _mm_nt(a, b):
    return lax.dot_general(a.astype(bf16), b.astype(bf16), (((1,), (1,)), ((), ())), preferred_element_type=f32)


def _mm_tn(a, b):
    n = a.shape[1]
    eye = jnp.where(lax.broadcasted_iota(jnp.int32, (n, n), 0) == lax.broadcasted_iota(jnp.int32, (n, n), 1),
                    1.0, 0.0).astype(bf16)
    at = lax.dot_general(eye, a.astype(bf16), (((1,), (1,)), ((), ())), preferred_element_type=f32)
    return jnp.dot(at.astype(bf16), b.astype(bf16), preferred_element_type=f32)


def _split3(x):
    hi = x.astype(bf16)
    r = x - hi.astype(f32)
    mid = r.astype(bf16)
    lo = (r - mid.astype(f32)).astype(bf16)
    return hi, mid, lo


def _mm01(m01, x):
    hi, mid, lo = _split3(x)
    d = lambda t: jnp.dot(m01, t, preferred_element_type=f32)
    return d(hi) + d(mid) + d(lo)


def _transpose_exact(eye, x):
    hi, mid, lo = _split3(x)
    d = lambda t: lax.dot_general(eye, t, (((1,), (1,)), ((), ())), preferred_element_type=f32)
    return d(hi) + d(mid) + d(lo)


def _mm3(a, b):
    ah = a.astype(bf16)
    al = (a - ah.astype(f32)).astype(bf16)
    bh = b.astype(bf16)
    bl = (b - bh.astype(f32)).astype(bf16)
    d = lambda x, y: jnp.dot(x, y, preferred_element_type=f32)
    return d(ah, bh) + d(ah, bl) + d(al, bh)


def _sigmoid(x):
    return 1.0 / (1.0 + jnp.exp(-x))


def _silu(x):
    return x * _sigmoid(x)


def _softplus(x):
    return jnp.maximum(x, 0.0) + jnp.log(1.0 + jnp.exp(-jnp.abs(x)))


def _chunk_masks(rev):
    i = lax.broadcasted_iota(jnp.int32, (CHUNK, CHUNK), 0)
    j = lax.broadcasted_iota(jnp.int32, (CHUNK, CHUNK), 1)
    if rev:
        incl, strict = j >= i, j > i
    else:
        incl, strict = j <= i, j < i
    ll = jnp.where(incl, 1.0, 0.0).astype(bf16)
    lls = jnp.where(incl, 0.0, 1.0).astype(bf16)
    uu = jnp.where(strict, 1.0, 0.0)
    return incl, strict, ll, lls, uu


def _fill_ext(ext_ref, main_ref, prev_ref, next_ref, sfirst, slast):
    ext_ref[0:SUBLANES, :] = jnp.where(sfirst == 1, 0.0, prev_ref[...])
    ext_ref[SUBLANES:SUBLANES + TB, :] = main_ref[...]
    ext_ref[SUBLANES + TB:2 * SUBLANES + TB, :] = jnp.where(slast == 1, 0.0, next_ref[...])


def _conv_tile(ext_ref, w_ref, r0, col0, width, bias=None):
    acc = None
    for t in range(CONV_W):
        rows = ext_ref[pl.ds(SUBLANES + r0 - CONV_W // 2 + t, CHUNK), col0:col0 + width]
        term = rows * w_ref[t:t + 1, col0:col0 + width]
        acc = term if acc is None else acc + term
    if bias is not None:
        acc = acc + bias
    return _silu(acc)


INV_BLOCK = 16


def _block_masks():
    i = lax.broadcasted_iota(jnp.int32, (CHUNK, CHUNK), 0)
    j = lax.broadcasted_iota(jnp.int32, (CHUNK, CHUNK), 1)
    blk16 = (i // INV_BLOCK) == (j // INV_BLOCK)
    blk32 = (i // (2 * INV_BLOCK)) == (j // (2 * INV_BLOCK))
    return blk16, blk32


def _unit_tri_inverse_many(lmats, eye, blk16, blk32):
    ps = list(lmats)
    ld = {p: jnp.where(blk16, lmats[p], 0.0) for p in ps}
    t = {p: eye - ld[p] for p in ps}
    pw = {p: _mm3(ld[p], ld[p]) for p in ps}
    for it in range(3):
        t = {p: t[p] + _mm3(t[p], pw[p]) for p in ps}
        if it < 2:
            pw = {p: _mm3(pw[p], pw[p]) for p in ps}
    in32 = jnp.logical_and(blk32, jnp.logical_not(blk16))
    m = {p: _mm3(t[p], jnp.where(in32, lmats[p], 0.0)) for p in ps}
    t = {p: t[p] - _mm3(m[p], t[p]) for p in ps}
    m = {p: _mm3(t[p], jnp.where(blk32, 0.0, lmats[p])) for p in ps}
    return {p: t[p] - _mm3(m[p], t[p]) for p in ps}


def _gdn_kernel(order_ref, seq_ref, first_ref, last_ref,
                main_ref, prev_ref, next_ref, small_ref, convw_ref, par_ref, s0_ref,
                o_ref, sfin_ref, s_ref, ext_ref, *, rev):
    del order_ref, seq_ref
    step = pl.program_id(0)
    sfirst = first_ref[step]
    slast = last_ref[step]
    pfirst, plast = (slast, sfirst) if rev else (sfirst, slast)
    d = 1 if rev else 0

    @pl.when(pfirst == 1)
    def _():
        s_ref[...] = s0_ref[...]

    _fill_ext(ext_ref, main_ref, prev_ref, next_ref, sfirst, slast)

    incl, strict, ll, lls, uu = _chunk_masks(rev)
    eye = jnp.where(lax.broadcasted_iota(jnp.int32, (CHUNK, CHUNK), 0)
                    == lax.broadcasted_iota(jnp.int32, (CHUNK, CHUNK), 1), 1.0, 0.0)
    blk16, blk32 = _block_masks()
    neg_a = -jnp.exp(par_ref[0:1, :])
    dt_bias = par_ref[1:2, :]
    n_chunks = TB // CHUNK

    chunk_order = [n_chunks - 1 - cc if rev else cc for cc in range(n_chunks)]
    heads = range(GDN_HEADS)
    eye_k = jnp.where(lax.broadcasted_iota(jnp.int32, (GDN_DK, GDN_DK), 0)
                      == lax.broadcasted_iota(jnp.int32, (GDN_DK, GDN_DK), 1), 1.0, 0.0).astype(bf16)

    gates = {}
    for c in chunk_order:
        sm = small_ref[c * CHUNK:(c + 1) * CHUNK, :]
        la_blk = neg_a * _softplus(sm + dt_bias)
        gates[c] = (la_blk, _sigmoid(sm))
    g_blk = {c: _mm01(ll, gates[c][0]) for c in chunk_order}
    gr_blk = {c: _mm01(lls, gates[c][0]) for c in chunk_order}
    gl_blk = {c: jnp.sum(gates[c][0], axis=0, keepdims=True) for c in chunk_order}

    probs = [(c, h) for c in chunk_order for h in heads]
    ia = lambda h: SM_GA + d * GDN_HEADS + h
    ib = lambda h: SM_GB + d * GDN_HEADS + h
    col = lambda blk, lane: blk[:, lane:lane + 1]
    beta = {p: col(gates[p[0]][1], ib(p[1])) for p in probs}
    q, k, v = {}, {}, {}
    for p in probs:
        c, h = p
        r0 = c * CHUNK
        qq = _conv_tile(ext_ref, convw_ref, r0, h * GDN_DK, GDN_DK)
        kx = _conv_tile(ext_ref, convw_ref, r0, GDN_HEADS * GDN_DK + h * GDN_DK, GDN_DK)
        v[p] = _conv_tile(ext_ref, convw_ref, r0, 2 * GDN_HEADS * GDN_DK + h * GDN_DV, GDN_DV)
        q[p] = qq * lax.rsqrt(jnp.sum(qq * qq, axis=-1, keepdims=True) + EPS) * (GDN_DK ** -0.5)
        k[p] = kx * lax.rsqrt(jnp.sum(kx * kx, axis=-1, keepdims=True) + EPS)
    g_t = {c: _transpose_exact(eye_k, g_blk[c]) for c in chunk_order}
    dmat = {p: jnp.where(incl, col(g_blk[p[0]], ia(p[1])) - g_t[p[0]][ia(p[1]):ia(p[1]) + 1, :], 0.0)
            for p in probs}
    kk = {p: _mm_nt(k[p], k[p]) for p in probs}
    qk = {p: _mm_nt(q[p], k[p]) for p in probs}
    decay = {p: jnp.where(incl, jnp.exp(dmat[p]), 0.0) for p in probs}
    lmat = {p: jnp.where(strict, kk[p] * beta[p] * decay[p], 0.0) for p in probs}
    qk = {p: jnp.where(incl, qk[p] * decay[p], 0.0) for p in probs}
    tinv = _unit_tri_inverse_many(lmat, eye, blk16, blk32)
    uw = {}
    for p in probs:
        e_g = jnp.exp(col(g_blk[p[0]], ia(p[1])))
        rhs = jnp.concatenate([v[p] * beta[p], k[p] * beta[p] * e_g], axis=1)
        uw[p] = _mm3(tinv[p], rhs)
        q[p] = q[p] * e_g
    kdt = {}
    for p in probs:
        k_dec = (k[p] * jnp.exp(col(gr_blk[p[0]], ia(p[1])))).astype(bf16)
        kdt[p] = lax.dot_general(eye_k, k_dec, (((1,), (1,)), ((), ())), preferred_element_type=f32)

    for c in chunk_order:
        ps = [(c, h) for h in heads]
        s_old = {p: s_ref[p[1]] for p in ps}
        ws = {p: _mm(uw[p][:, GDN_DV:], s_old[p]) for p in ps}
        qs = {p: _mm(q[p], s_old[p]) for p in ps}
        v_new = {p: uw[p][:, :GDN_DV] - ws[p] for p in ps}
        o = {p: qs[p] + _mm(qk[p], v_new[p]) for p in ps}
        for p in ps:
            h = p[1]
            s_ref[h] = s_old[p] * jnp.exp(col(gl_blk[c], ia(h))) + _mm(kdt[p], v_new[p])
            o_ref[c * CHUNK:(c + 1) * CHUNK, h * GDN_DV:(h + 1) * GDN_DV] = o[p]

    @pl.when(plast == 1)
    def _():
        sfin_ref[...] = s_ref[...]


def _halo_maps(n_rows8):
    prev_map = lambda i, order, *_: (jnp.maximum(order[i] * (TB // SUBLANES) - 1, 0), 0)
    next_map = lambda i, order, *_: (jnp.minimum((order[i] + 1) * (TB // SUBLANES), n_rows8 - 1), 0)
    return prev_map, next_map


def gdn_scan(lay, proj, convw, par, s0, layer, rev):
    prev_map, next_map = _halo_maps(lay.m // SUBLANES)
    blk_map = lambda i, order, *_: (order[i], 0)
    seq_map = lambda i, order, seq, *_: (seq[i], 0, 0, 0)
    grid_spec = pltpu.PrefetchScalarGridSpec(
        num_scalar_prefetch=4,
        grid=(lay.n_blk,),
        in_specs=[
            pl.BlockSpec((TB, GDN_QKV), blk_map),
            pl.BlockSpec((SUBLANES, GDN_QKV), prev_map),
            pl.BlockSpec((SUBLANES, GDN_QKV), next_map),
            pl.BlockSpec((TB, LANES), lambda i, order, *_: (order[i], COL_SMALL // LANES)),
            pl.BlockSpec((None, CONV_W, GDN_QKV), lambda i, *_: (layer, 0, 0)),
            pl.BlockSpec((None, SUBLANES, LANES), lambda i, *_: (layer, 0, 0)),
            pl.BlockSpec((None, GDN_HEADS, GDN_DK, GDN_DV), seq_map),
        ],
        out_specs=[
            pl.BlockSpec((TB, GDN_W), blk_map),
            pl.BlockSpec((None, GDN_HEADS, GDN_DK, GDN_DV), seq_map),
        ],
        scratch_shapes=[
            pltpu.VMEM((GDN_HEADS, GDN_DK, GDN_DV), f32),
            pltpu.VMEM((TB + 2 * SUBLANES, GDN_QKV), f32),
        ],
    )
    return pl.pallas_call(
        functools.partial(_gdn_kernel, rev=rev),
        grid_spec=grid_spec,
        out_shape=[
            jax.ShapeDtypeStruct((lay.m, GDN_W), f32),
            jax.ShapeDtypeStruct((lay.n_seq, GDN_HEADS, GDN_DK, GDN_DV), f32),
        ],
        compiler_params=pltpu.CompilerParams(dimension_semantics=("arbitrary",)),
        name="gdn_bwd" if rev else "gdn_fwd",
    )(*lay.scan_tables(rev), proj, proj, proj, proj, convw, par, s0)


SSM_PAIRS = SSM_HEADS // 2
PAIRS_PER_GROUP = SSM_PAIRS // SSM_GROUPS
COL_B = SSM_INNER
COL_C = SSM_INNER + SSM_GROUPS * SSM_DSTATE


def _ssd_kernel(order_ref, seq_ref, first_ref, last_ref,
                main_ref, prev_ref, next_ref, small_ref, convw_ref, cpar_ref, par_ref, h0_ref,
                y_ref, hfin_ref, h_ref, ext_ref, *, rev):
    del order_ref, seq_ref
    step = pl.program_id(0)
    sfirst = first_ref[step]
    slast = last_ref[step]
    pfirst, plast = (slast, sfirst) if rev else (sfirst, slast)
    d = 1 if rev else 0

    @pl.when(pfirst == 1)
    def _():
        h_ref[...] = h0_ref[...]

    _fill_ext(ext_ref, main_ref, prev_ref, next_ref, sfirst, slast)

    incl, _, ll, lls, uu = _chunk_masks(rev)
    lane_lo = lax.broadcasted_iota(jnp.int32, (CHUNK, LANES), 1) < SSM_DSTATE
    row_lo = lax.broadcasted_iota(jnp.int32, (LANES, LANES), 0) < SSM_HEADDIM
    eye_p = jnp.where(lax.broadcasted_iota(jnp.int32, (LANES, LANES), 0)
                      == lax.broadcasted_iota(jnp.int32, (LANES, LANES), 1), 1.0, 0.0).astype(bf16)
    neg_a = -jnp.exp(par_ref[0:1, :])
    dt_bias = par_ref[1:2, :]
    n_chunks = TB // CHUNK

    for cc in range(n_chunks):
        c = n_chunks - 1 - cc if rev else cc
        r0 = c * CHUNK
        sm = small_ref[r0:r0 + CHUNK, :]
        dt_blk = _softplus(sm + dt_bias)
        dta_blk = dt_blk * neg_a
        acum_blk = _mm01(ll, dta_blk)
        ar_blk = _mm01(lls, dta_blk)
        al_blk = jnp.sum(dta_blk, axis=0, keepdims=True)
        b_pair = _conv_tile(ext_ref, convw_ref, r0, COL_B, LANES, cpar_ref[0:1, COL_B:COL_B + LANES])
        c_pair = _conv_tile(ext_ref, convw_ref, r0, COL_C, LANES, cpar_ref[0:1, COL_C:COL_C + LANES])
        c_g, b_g, cb = [], [], []
        for g in range(SSM_GROUPS):
            gmask = lane_lo if g == 0 else jnp.logical_not(lane_lo)
            c_g.append(jnp.where(gmask, c_pair, 0.0))
            b_g.append(jnp.where(gmask, b_pair, 0.0))
        cb = [_mm_nt(c_g[g], b_pair) for g in range(SSM_GROUPS)]
        pairs = range(SSM_PAIRS)
        grp = lambda p: p // PAIRS_PER_GROUP
        lane_of = lambda hh: SM_DT + d * SSM_HEADS + hh
        colv = lambda blk, hh: blk[:, lane_of(hh):lane_of(hh) + 1]
        both = lambda blk, p: jnp.where(lane_lo, colv(blk, 2 * p), colv(blk, 2 * p + 1))
        xs = [_conv_tile(ext_ref, convw_ref, r0, p * LANES, LANES, cpar_ref[0:1, p * LANES:(p + 1) * LANES])
              for p in pairs]
        acum_t = _transpose_exact(eye_p, acum_blk)
        dmat = [jnp.where(incl, colv(acum_blk, hh) - acum_t[lane_of(hh):lane_of(hh) + 1, :], 0.0)
                for hh in range(SSM_HEADS)]
        m_h = [cb[grp(hh // 2)] * jnp.where(incl, jnp.exp(dmat[hh]), 0.0) for hh in range(SSM_HEADS)]
        xdt = [xs[p] * both(dt_blk, p) for p in pairs]
        y = [jnp.where(lane_lo, _mm(m_h[2 * p], xdt[p]), _mm(m_h[2 * p + 1], xdt[p])) for p in pairs]
        xdt_t = [lax.dot_general(eye_p, xdt[p].astype(bf16), (((1,), (1,)), ((), ())),
                                 preferred_element_type=f32).astype(bf16) for p in pairs]
        st_new = [jnp.where(row_lo,
                            jnp.dot(xdt_t[p], (b_g[grp(p)] * jnp.exp(colv(ar_blk, 2 * p))).astype(bf16),
                                    preferred_element_type=f32),
                            jnp.dot(xdt_t[p], (b_g[grp(p)] * jnp.exp(colv(ar_blk, 2 * p + 1))).astype(bf16),
                                    preferred_element_type=f32)) for p in pairs]
        h_pair = [h_ref[p * LANES:(p + 1) * LANES, :] for p in pairs]
        y_off = [_mm_nt(c_g[grp(p)], h_pair[p]) for p in pairs]
        for p in pairs:
            col = p * LANES
            yp = y[p] + y_off[p] * jnp.exp(both(acum_blk, p))
            if not rev:
                yp = yp + xs[p] * cpar_ref[1:2, col:col + LANES]
            dec = jnp.where(row_lo, jnp.exp(colv(al_blk, 2 * p)), jnp.exp(colv(al_blk, 2 * p + 1)))
            h_ref[col:col + LANES, :] = h_pair[p] * dec + st_new[p]
            y_ref[r0:r0 + CHUNK, col:col + LANES] = yp

    @pl.when(plast == 1)
    def _():
        hfin_ref[...] = h_ref[...]


def ssd_scan(lay, proj, convw, cpar, par, h0, layer, rev):
    prev_map, next_map = _halo_maps(lay.m // SUBLANES)
    cblk = COL_XBC // SSM_XBC
    blk_map = lambda i, order, *_: (order[i], 0)
    seq_map = lambda i, order, seq, *_: (seq[i], 0, 0)
    grid_spec = pltpu.PrefetchScalarGridSpec(
        num_scalar_prefetch=4,
        grid=(lay.n_blk,),
        in_specs=[
            pl.BlockSpec((TB, SSM_XBC), lambda i, order, *_: (order[i], cblk)),
            pl.BlockSpec((SUBLANES, SSM_XBC), lambda i, *a: (prev_map(i, *a)[0], cblk)),
            pl.BlockSpec((SUBLANES, SSM_XBC), lambda i, *a: (next_map(i, *a)[0], cblk)),
            pl.BlockSpec((TB, LANES), lambda i, order, *_: (order[i], COL_SMALL // LANES)),
            pl.BlockSpec((None, CONV_W, SSM_XBC), lambda i, *_: (layer, 0, 0)),
            pl.BlockSpec((None, SUBLANES, SSM_XBC), lambda i, *_: (layer, 0, 0)),
            pl.BlockSpec((None, SUBLANES, LANES), lambda i, *_: (layer, 0, 0)),
            pl.BlockSpec((None, SSM_INNER, LANES), seq_map),
        ],
        out_specs=[
            pl.BlockSpec((TB, SSM_INNER), blk_map),
            pl.BlockSpec((None, SSM_INNER, LANES), seq_map),
        ],
        scratch_shapes=[
            pltpu.VMEM((SSM_INNER, LANES), f32),
            pltpu.VMEM((TB + 2 * SUBLANES, SSM_XBC), f32),
        ],
    )
    return pl.pallas_call(
        functools.partial(_ssd_kernel, rev=rev),
        grid_spec=grid_spec,
        out_shape=[
            jax.ShapeDtypeStruct((lay.m, SSM_INNER), f32),
            jax.ShapeDtypeStruct((lay.n_seq, SSM_INNER, LANES), f32),
        ],
        compiler_params=pltpu.CompilerParams(dimension_semantics=("arbitrary",)),
        name="ssd_bwd" if rev else "ssd_fwd",
    )(*lay.scan_tables(rev), proj, proj, proj, proj, convw, cpar, par, h0)


def ssm_state_to_lanes(h):
    n = h.shape[0]
    hg = h.reshape(n, SSM_GROUPS, SSM_HEADS // SSM_GROUPS * SSM_HEADDIM, SSM_DSTATE)
    z = jnp.zeros_like(hg[:, 0])
    return jnp.concatenate([jnp.concatenate([hg[:, 0], z], -1), jnp.concatenate([z, hg[:, 1]], -1)], 1)


def ssm_state_from_lanes(hl):
    n = hl.shape[0]
    r = hl.reshape(n, SSM_GROUPS, SSM_INNER // SSM_GROUPS, SSM_GROUPS, SSM_DSTATE)
    h = jnp.stack([r[:, g, :, g, :] for g in range(SSM_GROUPS)], 1)
    return h.reshape(n, SSM_HEADS, SSM_HEADDIM, SSM_DSTATE)


ADA_TN = 1536
VMEM_LIMIT = 56 * 1024 * 1024


def _ada_kernel(c_ref, w_ref, b_ref, o_ref):
    s = _silu(c_ref[...])
    o_ref[...] = jnp.dot(s.astype(bf16), w_ref[...].astype(bf16), preferred_element_type=f32) + b_ref[...]


def ada_mod(cvec, w_ada, b_ada, layer):
    n = 6 * D_MODEL
    out = pl.pallas_call(
        _ada_kernel,
        grid=(n // ADA_TN,),
        in_specs=[
            pl.BlockSpec((SUBLANES, D_MODEL), lambda j: (0, 0)),
            pl.BlockSpec((None, D_MODEL, ADA_TN), lambda j: (layer, 0, j)),
            pl.BlockSpec((None, 1, ADA_TN), lambda j: (layer, 0, j)),
        ],
        out_specs=pl.BlockSpec((SUBLANES, ADA_TN), lambda j: (0, j)),
        out_shape=jax.ShapeDtypeStruct((SUBLANES, n), f32),
        compiler_params=pltpu.CompilerParams(dimension_semantics=("arbitrary",), vmem_limit_bytes=VMEM_LIMIT),
        name="ada_mod",
    )(cvec, w_ada, b_ada)
    return out.reshape(SUBLANES, 6, D_MODEL)


MOD_SH1, MOD_SC1, MOD_G1, MOD_SH2, MOD_SC2, MOD_G2 = range(6)


def _rms(x):
    return x * lax.rsqrt(jnp.mean(x * x, axis=-1, keepdims=True) + EPS)


PROJ_CHUNK = 512


def _pick_rows(i, n_first, refs):
    if len(refs) == 1:
        return refs[0][...]
    return jnp.where(i < n_first, refs[0][...], refs[1][...])


def _row_specs(lay, x, width):
    if not isinstance(x, tuple):
        return [pl.BlockSpec((TB, width), lambda i, mod: (jnp.minimum(i, lay.n_blk - 1), 0))], [x]
    nbc, nbl = lay.m_ctx // TB, lay.m_lat // TB
    return ([pl.BlockSpec((TB, width), lambda i, mod: (jnp.minimum(i, nbc - 1), 0)),
             pl.BlockSpec((TB, width), lambda i, mod: (jnp.clip(i - nbc, 0, nbl - 1), 0))], list(x))


def _in_proj_kernel(mod_ref, *refs, has_res, n_x, n_first):
    del mod_ref
    i = pl.program_id(0)
    x_refs, refs = refs[:n_x], refs[n_x:]
    if has_res:
        y_ref, pmods_ref, mods_ref, n1_ref, w_ref, proj_ref, xo_ref = refs
        x = _pick_rows(i, n_first, x_refs) + pmods_ref[MOD_G2:MOD_G2 + 1, :] * y_ref[...]
        xo_ref[...] = x
    else:
        mods_ref, n1_ref, w_ref, proj_ref = refs
        x = _pick_rows(i, n_first, x_refs)
    h = _rms(x) * n1_ref[...]
    h = h * (1.0 + mods_ref[MOD_SC1:MOD_SC1 + 1, :]) + mods_ref[MOD_SH1:MOD_SH1 + 1, :]
    hb = h.astype(bf16)
    for c0 in range(0, PROJ_W, PROJ_CHUNK):
        wd = min(PROJ_CHUNK, PROJ_W - c0)
        proj_ref[:, c0:c0 + wd] = jnp.dot(hb, w_ref[:, c0:c0 + wd], preferred_element_type=f32)


def in_proj(lay, x, mods, norm1, w_in_r, layer, res=None):
    last = lay.n_blk - 1
    row_map = lambda i, mod: (jnp.minimum(i, last), 0)
    mod_map = lambda i, mod: (mod[jnp.minimum(i, last)], 0, 0)
    in_specs, args = _row_specs(lay, x, D_MODEL)
    n_x = len(args)
    if res is not None:
        in_specs += [pl.BlockSpec((TB, D_MODEL), row_map), pl.BlockSpec((None, 6, D_MODEL), mod_map)]
        args += [res[0], res[1]]
    in_specs += [
        pl.BlockSpec((None, 6, D_MODEL), mod_map),
        pl.BlockSpec((None, 1, D_MODEL), lambda i, mod: (layer, 0, 0)),
        pl.BlockSpec((D_MODEL, PROJ_W), lambda i, mod: (0, 0), pipeline_mode=pl.Buffered(1)),
    ]
    args += [mods, norm1, w_in_r]
    out_specs = [pl.BlockSpec((TB, PROJ_W), row_map)]
    out_shape = [jax.ShapeDtypeStruct((lay.m, PROJ_W), f32)]
    if res is not None:
        out_specs.append(pl.BlockSpec((TB, D_MODEL), row_map))
        out_shape.append(jax.ShapeDtypeStruct((lay.m, D_MODEL), f32))
    outs = pl.pallas_call(
        functools.partial(_in_proj_kernel, has_res=res is not None, n_x=n_x, n_first=lay.m_ctx // TB),
        grid_spec=pltpu.PrefetchScalarGridSpec(
            num_scalar_prefetch=1, grid=(lay.n_blk,), in_specs=in_specs, out_specs=out_specs),
        out_shape=out_shape,
        compiler_params=pltpu.CompilerParams(dimension_semantics=("arbitrary",), vmem_limit_bytes=VMEM_LIMIT),
        name="in_proj",
    )(jnp.asarray(lay.mod), *args)
    return (outs[0], outs[1]) if res is not None else (outs[0], x)


MLA_QK = MLA_NOPE + MLA_ROPE
Q_HEADS_W = MLA_HEADS * LANES
ROPE_HALF = MLA_ROPE // 2


def _rope_lanes(x, c, s1, s2):
    return x * c + pltpu.roll(x, ROPE_HALF, 1) * s1 + pltpu.roll(x, LANES - ROPE_HALF, 1) * s2


def _mla_prep_kernel(pos_ref, mq_ref, ckv_ref, small_ref, qn_ref, kvn_ref, wuq_ref, rc_ref, rs1_ref, rs2_ref,
                     q_ref, ckvn_ref, kpe_ref):
    del pos_ref
    c, s1, s2 = rc_ref[...], rs1_ref[...], rs2_ref[...]
    cq = (_rms(mq_ref[...]) * qn_ref[...]).astype(bf16)
    qf = jnp.dot(cq, wuq_ref[...], preferred_element_type=f32)
    scale = MLA_QK ** -0.5 * math.log2(math.e)
    for h in range(MLA_HEADS):
        xh = qf[:, h * LANES:(h + 1) * LANES]
        q_ref[:, h * LANES:(h + 1) * LANES] = (_rope_lanes(xh, c, s1, s2) * scale).astype(bf16)
    ckvn_ref[...] = _rms(ckv_ref[...]) * kvn_ref[...]
    lane = lax.broadcasted_iota(jnp.int32, (TB, LANES), 1)
    in_rope = jnp.logical_and(lane >= SM_KPE, lane < SM_KPE + MLA_ROPE)
    kpe_ref[...] = jnp.where(in_rope, _rope_lanes(small_ref[...], c, s1, s2), 0.0)


def rope_tables(lay):
    t = lay.t_lat
    rows = t // GRID_W
    row = jnp.repeat(jnp.arange(rows, dtype=f32), GRID_W)
    col = jnp.tile(jnp.arange(GRID_W, dtype=f32), rows)
    nf = MLA_ROPE // 4
    inv = jnp.power(ROPE_THETA, -jnp.arange(nf, dtype=f32) / nf)
    ang = jnp.concatenate([row[:, None] * inv, col[:, None] * inv], axis=-1)
    cos, sin = jnp.cos(ang), jnp.sin(ang)
    z = lambda n: jnp.zeros((t, n), f32)
    c = jnp.concatenate([jnp.ones((t, MLA_NOPE), f32), cos, cos, z(LANES - MLA_QK)], -1)
    s1 = jnp.concatenate([z(MLA_NOPE + ROPE_HALF), sin, z(LANES - MLA_QK)], -1)
    s2 = jnp.concatenate([z(MLA_NOPE), -sin, z(ROPE_HALF + LANES - MLA_QK)], -1)
    ident = jnp.concatenate([jnp.ones((TB, MLA_QK), f32), jnp.zeros((TB, LANES - MLA_QK), f32)], -1)
    zero = jnp.zeros((TB, LANES), f32)
    return (jnp.concatenate([ident, c]), jnp.concatenate([zero, s1]), jnp.concatenate([zero, s2]))


def _pos_blocks(lay):
    pos = [0] * (lay.m_ctx // TB)
    for _ in range(lay.n_lat):
        pos += [1 + j for j in range(lay.t_lat // TB)]
    return np.array(pos + pos[-1:], np.int32)


def mla_prep(lay, proj, q_norm, kv_norm, w_uq_r, tables, layer):
    row_map = lambda i, pos: (i, 0)
    tab_map = lambda i, pos: (pos[i], 0)
    return pl.pallas_call(
        _mla_prep_kernel,
        grid_spec=pltpu.PrefetchScalarGridSpec(
            num_scalar_prefetch=1, grid=(lay.n_blk,),
            in_specs=[
                pl.BlockSpec((TB, Q_RANK), lambda i, pos: (i, COL_MQ // Q_RANK)),
                pl.BlockSpec((TB, KV_RANK), lambda i, pos: (i, COL_CKV // KV_RANK)),
                pl.BlockSpec((TB, LANES), lambda i, pos: (i, COL_SMALL // LANES)),
                pl.BlockSpec((None, 1, Q_RANK), lambda i, pos: (layer, 0, 0)),
                pl.BlockSpec((None, 1, KV_RANK), lambda i, pos: (layer, 0, 0)),
                pl.BlockSpec((Q_RANK, Q_HEADS_W), lambda i, pos: (0, 0)),
                pl.BlockSpec((TB, LANES), tab_map),
                pl.BlockSpec((TB, LANES), tab_map),
                pl.BlockSpec((TB, LANES), tab_map),
            ],
            out_specs=[
                pl.BlockSpec((TB, Q_HEADS_W), row_map),
                pl.BlockSpec((TB, KV_RANK), row_map),
                pl.BlockSpec((TB, LANES), row_map),
            ]),
        out_shape=[
            jax.ShapeDtypeStruct((lay.m, Q_HEADS_W), bf16),
            jax.ShapeDtypeStruct((lay.m, KV_RANK), f32),
            jax.ShapeDtypeStruct((lay.m, LANES), f32),
        ],
        compiler_params=pltpu.CompilerParams(dimension_semantics=("arbitrary",)),
        name="mla_prep",
    )(jnp.asarray(_pos_blocks(lay)), proj, proj, proj, q_norm, kv_norm, w_uq_r, *tables)


V_PAIRS = MLA_HEADS // 2
KV_UP_W = 2 * MLA_HEADS * LANES


def _kv_up_kernel(ckv_ref, kpe_ref, w_ref, k_ref, v_ref):
    kv = jnp.dot(ckv_ref[...].astype(bf16), w_ref[...], preferred_element_type=f32)
    kpe = kpe_ref[...]
    for h in range(MLA_HEADS):
        k_ref[h] = (kv[:, h * LANES:(h + 1) * LANES] + kpe).astype(bf16)
    v0 = MLA_HEADS * LANES
    lane_lo = lax.broadcasted_iota(jnp.int32, (TB, LANES), 1) < MLA_VDIM
    for h in range(MLA_HEADS):
        v_ref[h] = jnp.where(lane_lo, kv[:, v0 + h * LANES:v0 + (h + 1) * LANES], 1.0).astype(bf16)


def kv_up(ckv, kpe, w_ukv_r):
    r = ckv.shape[0]
    return pl.pallas_call(
        _kv_up_kernel,
        grid=(r // TB,),
        in_specs=[
            pl.BlockSpec((TB, KV_RANK), lambda i: (i, 0)),
            pl.BlockSpec((TB, LANES), lambda i: (i, 0)),
            pl.BlockSpec((KV_RANK, KV_UP_W), lambda i: (0, 0)),
        ],
        out_specs=[
            pl.BlockSpec((MLA_HEADS, TB, LANES), lambda i: (0, i, 0)),
            pl.BlockSpec((MLA_HEADS, TB, LANES), lambda i: (0, i, 0)),
        ],
        out_shape=[
            jax.ShapeDtypeStruct((MLA_HEADS, r, LANES), bf16),
            jax.ShapeDtypeStruct((MLA_HEADS, r, LANES), bf16),
        ],
        compiler_params=pltpu.CompilerParams(dimension_semantics=("arbitrary",)),
        name="kv_up",
    )(ckv, kpe, w_ukv_r)


ATT_KC = 2304


def _attn_kernel(q_ref, k_ref, v_ref, o_ref, m_ref, acc_ref, *, n_keys):
    lane_lo = lax.broadcasted_iota(jnp.int32, (TB, LANES), 1) < MLA_VDIM
    kc = min(ATT_KC, n_keys)
    hs = range(MLA_HEADS)
    m_ref[...] = jnp.full((MLA_HEADS, TB, LANES), -jnp.inf, f32)
    acc_ref[...] = jnp.zeros((MLA_HEADS, TB, LANES), f32)

    def body(c, carry):
        c0 = pl.multiple_of(c * kc, kc)
        score = lambda h: lax.dot_general(q_ref[:, h * LANES:(h + 1) * LANES], k_ref[h, pl.ds(c0, kc), :],
                                          (((1,), (1,)), ((), ())), preferred_element_type=f32)
        def softmax_part(h, s):
            m_old = m_ref[h]
            m_new = jnp.maximum(m_old, jnp.max(s, axis=-1, keepdims=True))
            m_ref[h] = m_new
            return jnp.exp2(m_old - m_new), jnp.exp2(s - m_new[:, 0:1]).astype(bf16)

        def value_part(h, alpha, pe):
            pv = jnp.dot(pe, v_ref[h, pl.ds(c0, kc), :], preferred_element_type=f32)
            acc_ref[h] = alpha * acc_ref[h] + pv

        scores, probs = {0: score(0), 1: score(1)}, {}
        probs[0] = softmax_part(0, scores.pop(0))
        for h in hs:
            if h + 2 < MLA_HEADS:
                scores[h + 2] = score(h + 2)
            if h + 1 < MLA_HEADS:
                probs[h + 1] = softmax_part(h + 1, scores.pop(h + 1))
            value_part(h, *probs.pop(h))
        return carry

    lax.fori_loop(0, n_keys // kc, body, 0)
    half = LANES // 2
    for p in range(V_PAIRS):
        a_even = acc_ref[2 * p]
        a_odd = pltpu.roll(acc_ref[2 * p + 1], half, 1)
        o_even = a_even * pltpu.roll(1.0 / a_even, half, 1)
        o_odd = a_odd * pltpu.roll(1.0 / a_odd, half, 1)
        o_ref[:, p * LANES:(p + 1) * LANES] = jnp.where(lane_lo, o_even, o_odd)


def attention(q, k, v, q_blk0, n_seq, t_q, n_keys):
    qb = t_q // TB
    return pl.pallas_call(
        functools.partial(_attn_kernel, n_keys=n_keys),
        grid=(n_seq, qb),
        in_specs=[
            pl.BlockSpec((TB, Q_HEADS_W), lambda b, j: (q_blk0 + b * qb + j, 0)),
            pl.BlockSpec((MLA_HEADS, n_keys, LANES), lambda b, j: (0, b, 0)),
            pl.BlockSpec((MLA_HEADS, n_keys, LANES), lambda b, j: (0, b, 0)),
        ],
        out_specs=pl.BlockSpec((TB, MLA_W), lambda b, j: (b * qb + j, 0)),
        out_shape=jax.ShapeDtypeStruct((n_seq * t_q, MLA_W), f32),
        scratch_shapes=[pltpu.VMEM((MLA_HEADS, TB, LANES), f32)] * 2,
        compiler_params=pltpu.CompilerParams(dimension_semantics=("arbitrary", "arbitrary"),
                                             vmem_limit_bytes=VMEM_LIMIT),
        name="mla_attn",
    )(q, k, v)


N_PAIRS = E_PER_GROUP * (E_PER_GROUP - 1) // 2
N_BUCKETS = N_EGROUPS * N_PAIRS
RT_BUCKET, RT_WLO, RT_WHI = 0, 1, 2
H2X_W = D_MODEL + LANES


def _route(lg):
    lane = lax.broadcasted_iota(jnp.int32, lg.shape, 1)
    big = jnp.int32(LANES)
    rmax = lambda x: jnp.max(x, axis=-1, keepdims=True)
    rmin = lambda x: jnp.min(x, axis=-1, keepdims=True)
    rsum = lambda x: jnp.sum(x, axis=-1, keepdims=True)
    neg = -jnp.inf
    gmask = lane < N_EGROUPS
    gl = jnp.where(gmask, lg, neg)
    gmax = rmax(gl)
    gsum = rsum(jnp.where(gmask, jnp.exp(lg - gmax), 0.0))
    gsel = rmin(jnp.where(gl == gmax, lane, big))
    pg_sel = 1.0 / gsum
    e0 = N_EGROUPS + E_PER_GROUP * gsel
    emask = jnp.logical_and(lane >= e0, lane < e0 + E_PER_GROUP)
    el = jnp.where(emask, lg, neg)
    emax = rmax(el)
    ee = jnp.where(emask, jnp.exp(lg - emax), 0.0)
    pe = ee / rsum(ee)
    pe_m = jnp.where(emask, pe, -1.0)
    v1 = rmax(pe_m)
    i1 = rmin(jnp.where(pe_m == v1, lane, big))
    pe_m2 = jnp.where(lane == i1, -1.0, pe_m)
    v2 = rmax(pe_m2)
    i2 = rmin(jnp.where(pe_m2 == v2, lane, big))
    w1 = pg_sel * v1 / (v1 + v2)
    w2 = pg_sel * v2 / (v1 + v2)
    a1, a2 = i1 - e0, i2 - e0
    lo, hi = jnp.minimum(a1, a2), jnp.maximum(a1, a2)
    pair = (lo * (2 * E_PER_GROUP - 1 - lo)) // 2 + (hi - lo - 1)
    bucket = (gsel * N_PAIRS + pair).astype(f32)
    w_lo = jnp.where(a1 < a2, w1, w2)
    w_hi = jnp.where(a1 < a2, w2, w1)
    return jnp.where(lane == RT_BUCKET, bucket,
                     jnp.where(lane == RT_WLO, w_lo, jnp.where(lane == RT_WHI, w_hi, 0.0)))


def _out_proj_kernel(mod_ref, *refs, n_x, n_first):
    del mod_ref
    x_refs = refs[:n_x]
    (of_ref, ob_ref, gz_ref, yf_ref, yb_ref, sz_ref, om_ref, mods_ref,
     gn_ref, sn_ref, n2_ref, wout_ref, wr_ref, br_ref, x1_ref, h2_ref, rt_ref) = refs[n_x:]
    acc = None
    for h in range(GDN_HEADS):
        sl = slice(h * GDN_DV, (h + 1) * GDN_DV)
        o = _rms(of_ref[:, sl] + ob_ref[:, sl]) * gn_ref[...] * _silu(gz_ref[:, sl])
        t = jnp.dot(o.astype(bf16), wout_ref[sl, :], preferred_element_type=f32)
        acc = t if acc is None else acc + t
    y = (yf_ref[...] + yb_ref[...]) * _silu(sz_ref[...])
    ys = _rms(y) * sn_ref[...]
    acc = acc + jnp.dot(ys.astype(bf16), wout_ref[GDN_W:GDN_W + SSM_INNER, :], preferred_element_type=f32)
    acc = acc + jnp.dot(om_ref[...].astype(bf16), wout_ref[GDN_W + SSM_INNER:MIX_W, :],
                        preferred_element_type=f32)
    x1 = _pick_rows(pl.program_id(0), n_first, x_refs) + mods_ref[MOD_G1:MOD_G1 + 1, :] * acc
    x1_ref[...] = x1
    h2 = _rms(x1) * n2_ref[...]
    h2 = h2 * (1.0 + mods_ref[MOD_SC2:MOD_SC2 + 1, :]) + mods_ref[MOD_SH2:MOD_SH2 + 1, :]
    rt = _route(_mm3(h2, wr_ref[...]) + br_ref[...])
    h2_ref[:, 0:D_MODEL] = h2
    h2_ref[:, D_MODEL:H2X_W] = rt
    rt_ref[...] = rt


def out_proj(lay, o_f, o_b, proj, y_f, y_b, o_mla, x, mods, gdn_norm, ssm_norm, norm2, w_out_b, w_rt, b_rt, layer):
    row_map = lambda i, mod: (i, 0)
    lyr = lambda i, mod: (layer, 0, 0)
    x_specs, x_args = _row_specs(lay, x, D_MODEL)
    return pl.pallas_call(
        functools.partial(_out_proj_kernel, n_x=len(x_args), n_first=lay.m_ctx // TB),
        grid_spec=pltpu.PrefetchScalarGridSpec(
            num_scalar_prefetch=1, grid=(lay.n_blk,),
            in_specs=x_specs + [
                pl.BlockSpec((TB, GDN_W), row_map),
                pl.BlockSpec((TB, GDN_W), row_map),
                pl.BlockSpec((TB, GDN_W), lambda i, mod: (i, COL_GZ // GDN_W)),
                pl.BlockSpec((TB, SSM_INNER), row_map),
                pl.BlockSpec((TB, SSM_INNER), row_map),
                pl.BlockSpec((TB, SSM_INNER), lambda i, mod: (i, COL_SZ // SSM_INNER)),
                pl.BlockSpec((TB, MLA_W), row_map),
                pl.BlockSpec((None, 6, D_MODEL), lambda i, mod: (mod[i], 0, 0)),
                pl.BlockSpec((None, 1, GDN_DV), lyr),
                pl.BlockSpec((None, 1, SSM_INNER), lyr),
                pl.BlockSpec((None, 1, D_MODEL), lyr),
                pl.BlockSpec((None, MIX_W, D_MODEL), lyr, pipeline_mode=pl.Buffered(1)),
                pl.BlockSpec((None, D_MODEL, LANES), lyr),
                pl.BlockSpec((None, 1, LANES), lyr),
            ],
            out_specs=[
                pl.BlockSpec((TB, D_MODEL), row_map),
                pl.BlockSpec((TB, H2X_W), row_map),
                pl.BlockSpec((TB, LANES), row_map),
            ]),
        out_shape=[
            jax.ShapeDtypeStruct((lay.m, D_MODEL), f32),
            jax.ShapeDtypeStruct((lay.m, H2X_W), f32),
            jax.ShapeDtypeStruct((lay.m, LANES), f32),
        ],
        compiler_params=pltpu.CompilerParams(dimension_semantics=("arbitrary",), vmem_limit_bytes=VMEM_LIMIT),
        name="out_proj",
    )(jnp.asarray(lay.mod), *x_args, o_f, o_b, proj, y_f, y_b, proj, o_mla, mods, gdn_norm, ssm_norm, norm2,
      w_out_b, w_rt, b_rt)


TME = 256
PAIR_LO = (0, 0, 0, 1, 1, 2)
PAIR_HI = (1, 2, 3, 2, 3, 3)


DMA_UNROLL = 8


def _moe_kernel(te_ref, nu_ref, ts_ref, nv_ref, order_ref,
                h2_hbm, wg_ref, wu_ref, wd_ref, y_hbm, xbuf, ybuf, gsem, ssem):
    del te_ref
    t = pl.program_id(0)
    j = pl.program_id(1)
    n_used = nu_ref[0]
    used = t < n_used
    slot = t % 2
    n_tok = y_hbm.shape[0] - TME

    def gather_rows(tile, sl):
        base = ts_ref[tile]

        def issue(r, c):
            pltpu.make_async_copy(h2_hbm.at[pl.ds(order_ref[base + r], 1), :],
                                  xbuf.at[sl, pl.ds(r, 1), :], gsem.at[sl]).start()
            return c
        lax.fori_loop(0, TME, issue, 0, unroll=DMA_UNROLL)

    def scatter_rows(tile, sl):
        base = ts_ref[tile]
        nvalid = nv_ref[tile]

        def issue(r, c):
            row = jnp.where(r < nvalid, order_ref[base + r], n_tok + r)
            pltpu.make_async_copy(ybuf.at[sl, pl.ds(r, 1), :],
                                  y_hbm.at[pl.ds(row, 1), :], ssem.at[sl]).start()
            return c
        lax.fori_loop(0, TME, issue, 0, unroll=DMA_UNROLL)

    def wait_gather(sl):
        pltpu.make_async_copy(h2_hbm.at[pl.ds(0, TME), :], xbuf.at[sl], gsem.at[sl]).wait()

    def wait_scatter(sl):
        pltpu.make_async_copy(ybuf.at[sl], y_hbm.at[pl.ds(0, TME), :], ssem.at[sl]).wait()

    @pl.when(jnp.logical_and(used, jnp.logical_and(t == 0, j == 0)))
    def _():
        gather_rows(0, 0)
        ybuf[1] = jnp.zeros((TME, D_MODEL), f32)
        spare = pltpu.make_async_copy(ybuf.at[1], y_hbm.at[pl.ds(n_tok, TME), :], ssem.at[1])
        spare.start()
        spare.wait()

    @pl.when(jnp.logical_and(used, j == 0))
    def _():
        wait_gather(slot)

    @pl.when(jnp.logical_and(t + 1 < n_used, j == 1))
    def _():
        gather_rows(t + 1, 1 - slot)

    @pl.when(used)
    def _():
        xb = xbuf[slot, :, 0:D_MODEL].astype(bf16)
        hg = jnp.dot(xb, wg_ref[...], preferred_element_type=f32)
        hu = jnp.dot(xb, wu_ref[...], preferred_element_type=f32)
        hdn = (_silu(hg) * hu).astype(bf16)
        yo = jnp.dot(hdn, wd_ref[...], preferred_element_type=f32)
        lane = lax.broadcasted_iota(jnp.int32, (TME, LANES), 1)
        w_lane = jnp.where(j == slot, RT_WLO, RT_WHI)
        wsel = jnp.sum(jnp.where(lane == w_lane, xbuf[slot, :, D_MODEL:H2X_W], 0.0), axis=-1, keepdims=True)

        @pl.when(j == 0)
        def _():
            ybuf[slot] = wsel * yo

        @pl.when(j == 1)
        def _():
            ybuf[slot] = ybuf[slot] + wsel * yo

    @pl.when(jnp.logical_and(used, j == 1))
    def _():
        @pl.when(t > 0)
        def _():
            wait_scatter(1 - slot)

        scatter_rows(t, slot)

        @pl.when(t == n_used - 1)
        def _():
            wait_scatter(slot)


def moe_dispatch(lay, route):
    m = lay.m
    n_tiles = m // TME + N_BUCKETS
    bucket = route[:, RT_BUCKET].astype(jnp.int32)
    order = jnp.argsort(bucket).astype(jnp.int32)
    counts = jnp.sum(bucket[:, None] == jnp.arange(N_BUCKETS, dtype=jnp.int32)[None, :], axis=0, dtype=jnp.int32)
    ptiles = (counts + TME - 1) // TME
    pend = jnp.cumsum(ptiles)
    pstart = pend - ptiles
    start = jnp.cumsum(counts) - counts
    n_used = pend[-1]
    tid = jnp.arange(n_tiles, dtype=jnp.int32)
    tid_c = jnp.minimum(tid, n_used - 1)
    tb = jnp.minimum(jnp.sum(pend[None, :] <= tid_c[:, None], axis=1, dtype=jnp.int32), N_BUCKETS - 1)
    sel = tb[:, None] == jnp.arange(N_BUCKETS, dtype=jnp.int32)[None, :]
    pick = lambda v: jnp.sum(jnp.where(sel, v[None, :], 0), axis=1, dtype=jnp.int32)
    off = (tid_c - pick(pstart)) * TME
    tstart = pick(start) + off
    nvalid = jnp.where(tid < n_used, jnp.clip(pick(counts) - off, 0, TME), 0).astype(jnp.int32)
    order_p = jnp.concatenate([order, jnp.zeros((TME,), jnp.int32)])
    grp, pr = tb // N_PAIRS, tb % N_PAIRS
    e_lo = grp * E_PER_GROUP + jnp.asarray(PAIR_LO, jnp.int32)[pr]
    e_hi = grp * E_PER_GROUP + jnp.asarray(PAIR_HI, jnp.int32)[pr]
    swap = (tid % 2) == 1
    te = jnp.stack([jnp.where(swap, e_hi, e_lo), jnp.where(swap, e_lo, e_hi)], -1).reshape(-1)
    te = jnp.concatenate([te, te[-2:]])
    return (te.astype(jnp.int32), n_used.reshape(1).astype(jnp.int32), tstart.astype(jnp.int32), nvalid,
            order_p, n_tiles)


def moe_apply(lay, h2, route, wg_b, wu_b, wd_b, layer):
    te, n_used, tstart, nvalid, order_p, n_tiles = moe_dispatch(lay, route)
    wmap = lambda t, j, te, *_: (layer, te[2 * t + j], 0, 0)
    return pl.pallas_call(
        _moe_kernel,
        grid_spec=pltpu.PrefetchScalarGridSpec(
            num_scalar_prefetch=5, grid=(n_tiles, 2),
            in_specs=[
                pl.BlockSpec(memory_space=pl.ANY),
                pl.BlockSpec((None, None, D_MODEL, EXPERT_FF), wmap),
                pl.BlockSpec((None, None, D_MODEL, EXPERT_FF), wmap),
                pl.BlockSpec((None, None, EXPERT_FF, D_MODEL), wmap),
            ],
            out_specs=pl.BlockSpec(memory_space=pl.ANY),
            scratch_shapes=[
                pltpu.VMEM((2, TME, H2X_W), f32),
                pltpu.VMEM((2, TME, D_MODEL), f32),
                pltpu.SemaphoreType.DMA((2,)),
                pltpu.SemaphoreType.DMA((2,)),
            ]),
        out_shape=jax.ShapeDtypeStruct((lay.m + TME, D_MODEL), f32),
        compiler_params=pltpu.CompilerParams(dimension_semantics=("arbitrary", "arbitrary"),
                                             vmem_limit_bytes=VMEM_LIMIT),
        name="moe",
    )(te, n_used, tstart, nvalid, order_p, h2, wg_b, wu_b, wd_b)


def _final_kernel(mod_ref, x_ref, y_ref, mods_ref, g_ref, oc_ref, ol_ref, *, n_first):
    del mod_ref
    i = pl.program_id(0)
    x = x_ref[...] + mods_ref[MOD_G2:MOD_G2 + 1, :] * y_ref[...]
    out = _rms(x) * g_ref[...]

    @pl.when(i < n_first)
    def _():
        oc_ref[...] = out

    @pl.when(i >= n_first)
    def _():
        ol_ref[...] = out


def final_norm_apply(lay, x, ytok, mods, g):
    row_map = lambda i, mod: (i, 0)
    nbc = lay.m_ctx // TB
    return pl.pallas_call(
        functools.partial(_final_kernel, n_first=nbc),
        grid_spec=pltpu.PrefetchScalarGridSpec(
            num_scalar_prefetch=1, grid=(lay.n_blk,),
            in_specs=[
                pl.BlockSpec((TB, D_MODEL), row_map),
                pl.BlockSpec((TB, D_MODEL), row_map),
                pl.BlockSpec((None, 6, D_MODEL), lambda i, mod: (mod[i], 0, 0)),
                pl.BlockSpec((1, D_MODEL), lambda i, mod: (0, 0)),
            ],
            out_specs=[pl.BlockSpec((TB, D_MODEL), lambda i, mod: (jnp.minimum(i, nbc - 1), 0)),
                       pl.BlockSpec((TB, D_MODEL), lambda i, mod: (jnp.maximum(i - nbc, 0), 0))]),
        out_shape=[jax.ShapeDtypeStruct((lay.m_ctx, D_MODEL), f32),
                   jax.ShapeDtypeStruct((lay.m_lat, D_MODEL), f32)],
        compiler_params=pltpu.CompilerParams(dimension_semantics=("arbitrary",)),
        name="final_norm",
    )(jnp.asarray(lay.mod), x, ytok, mods, g)


def _reorder_w_in(w):
    g_qkv, g_z, g_a, g_b, s_z, s_xbc, s_dt, m_q, m_kv = jnp.split(w, np.cumsum(IN_SIZES)[:-1], axis=1)
    ckv, kpe = m_kv[:, :KV_RANK], m_kv[:, KV_RANK:]
    z = lambda n: jnp.zeros((w.shape[0], n), w.dtype)
    small = jnp.concatenate([g_a, g_b, s_dt, z(SM_KPE - SM_DT - 2 * SSM_HEADS), kpe, z(LANES - SM_KPE - MLA_ROPE)], 1)
    return jnp.concatenate([g_qkv, g_z, s_z, m_q, ckv, s_xbc, small], 1).astype(bf16)


def _reorder_w_uq(w):
    r = w.reshape(Q_RANK, MLA_HEADS, MLA_QK)
    return jnp.pad(r, ((0, 0), (0, 0), (0, LANES - MLA_QK))).reshape(Q_RANK, Q_HEADS_W).astype(bf16)


def _reorder_w_ukv(w):
    r = w.reshape(KV_RANK, MLA_HEADS, MLA_NOPE + MLA_VDIM)
    k = jnp.pad(r[:, :, :MLA_NOPE], ((0, 0), (0, 0), (0, LANES - MLA_NOPE))).reshape(KV_RANK, MLA_HEADS * LANES)
    v = jnp.pad(r[:, :, MLA_NOPE:], ((0, 0), (0, 0), (0, LANES - MLA_VDIM))).reshape(KV_RANK, MLA_HEADS * LANES)
    return jnp.concatenate([k, v], 1).astype(bf16)


def _lane_row(vals, lane0):
    n = vals.shape[-1]
    return jnp.pad(vals, ((0, 0), (lane0, LANES - lane0 - n)))


def _par_rows(row0, row1):
    z = jnp.zeros_like(row0)
    return jnp.stack([row0, row1] + [z] * (SUBLANES - 2), axis=1)


def kernel(x_prompt, x_sample, cache_ckv, cache_kpe, state_gdn, state_ssm, c, c_ctx, w_ada, b_ada, norm1, norm2, w_in, gdn_conv, gdn_A_log, gdn_dt_bias, gdn_norm, ssm_conv, ssm_conv_bias, ssm_A_log, ssm_dt_bias, ssm_D, ssm_norm, mla_q_norm, mla_w_uq, mla_kv_norm, mla_w_ukv, w_out, router_group, router_group_bias, router_expert, router_expert_bias, moe_w_gate, moe_w_up, moe_w_down, final_norm):
    n_ctx, t_ctx, _ = x_prompt.shape
    n_lat, t_lat, _ = x_sample.shape
    past = cache_ckv.shape[2]
    depth = w_in.shape[0]
    assert n_lat < SUBLANES and t_ctx % TB == 0 and t_lat % TB == 0 and past % TB == 0
    lay = Layout(n_ctx, t_ctx, n_lat, t_lat)
    m_ctx = lay.m_ctx

    x = (x_prompt.reshape(m_ctx, D_MODEL), x_sample.reshape(lay.m_lat, D_MODEL))
    cvec = jnp.concatenate([c_ctx[None], c, jnp.zeros((SUBLANES - 1 - n_lat, D_MODEL), f32)], 0)
    tables = rope_tables(lay)

    b_ada3 = b_ada[:, None, :]
    norm1_3, norm2_3 = norm1[:, None, :], norm2[:, None, :]
    gdn_par = _par_rows(_lane_row(gdn_A_log.reshape(depth, -1), SM_GA), _lane_row(gdn_dt_bias.reshape(depth, -1), SM_GA))
    ssm_par = _par_rows(_lane_row(ssm_A_log.reshape(depth, -1), SM_DT), _lane_row(ssm_dt_bias.reshape(depth, -1), SM_DT))
    d_lanes = jnp.pad(jnp.repeat(ssm_D, SSM_HEADDIM, axis=-1), ((0, 0), (0, SSM_XBC - SSM_INNER)))
    ssm_cpar = _par_rows(ssm_conv_bias, d_lanes)
    w_rt = jnp.pad(jnp.concatenate([router_group, router_expert], -1),
                   ((0, 0), (0, 0), (0, LANES - N_EGROUPS - N_EXPERTS)))
    b_rt = jnp.pad(jnp.concatenate([router_group_bias, router_expert_bias], -1),
                   ((0, 0), (0, LANES - N_EGROUPS - N_EXPERTS)))[:, None, :]
    kpe_cache = jnp.pad(cache_kpe, ((0, 0), (0, 0), (0, 0), (SM_KPE, LANES - SM_KPE - MLA_ROPE)))

    w_out_b = w_out.astype(bf16)
    wg_b, wu_b, wd_b = moe_w_gate.astype(bf16), moe_w_up.astype(bf16), moe_w_down.astype(bf16)

    ckvs, kpes, gdns, ssms = [], [], [], []
    ytok = mods_prev = None
    for l in range(depth):
        mods = ada_mod(cvec, w_ada, b_ada3, l)
        res = None if l == 0 else (ytok, mods_prev)
        proj, x = in_proj(lay, x, mods, norm1_3, _reorder_w_in(w_in[l]), l, res)

        zg = jnp.zeros((n_ctx, GDN_HEADS, GDN_DK, GDN_DV), f32)
        o_f, sg_f = gdn_scan(lay, proj, gdn_conv, gdn_par, jnp.concatenate([zg, state_gdn[:, l, 0]], 0), l, False)
        o_b, sg_b = gdn_scan(lay, proj, gdn_conv, gdn_par, jnp.concatenate([zg, state_gdn[:, l, 1]], 0), l, True)

        zs = jnp.zeros((n_ctx, SSM_INNER, LANES), f32)
        y_f, hs_f = ssd_scan(lay, proj, ssm_conv, ssm_cpar, ssm_par,
                             jnp.concatenate([zs, ssm_state_to_lanes(state_ssm[:, l, 0])], 0), l, False)
        y_b, hs_b = ssd_scan(lay, proj, ssm_conv, ssm_cpar, ssm_par,
                             jnp.concatenate([zs, ssm_state_to_lanes(state_ssm[:, l, 1])], 0), l, True)

        q, ckvn, kpe = mla_prep(lay, proj, mla_q_norm[:, None, :], mla_kv_norm[:, None, :],
                                _reorder_w_uq(mla_w_uq[l]), tables, l)
        w_ukv_r = _reorder_w_ukv(mla_w_ukv[l])
        k_ctx, v_ctx = kv_up(ckvn[:m_ctx], kpe[:m_ctx], w_ukv_r)
        ckv_lat = jnp.concatenate([cache_ckv[:, l], ckvn[m_ctx:].reshape(n_lat, t_lat, KV_RANK)], 1)
        kpe_lat = jnp.concatenate([kpe_cache[:, l], kpe[m_ctx:].reshape(n_lat, t_lat, LANES)], 1)
        k_lat, v_lat = kv_up(ckv_lat.reshape(-1, KV_RANK), kpe_lat.reshape(-1, LANES), w_ukv_r)
        o_mla = jnp.concatenate([
            attention(q, k_ctx, v_ctx, 0, n_ctx, t_ctx, t_ctx),
            attention(q, k_lat, v_lat, m_ctx // TB, n_lat, t_lat, past + t_lat)], 0)

        x, h2, route = out_proj(lay, o_f, o_b, proj, y_f, y_b, o_mla, x, mods, gdn_norm[:, None, :],
                                ssm_norm[:, None, :], norm2_3, w_out_b, w_rt, b_rt, l)
        ytok = moe_apply(lay, h2, route, wg_b, wu_b, wd_b, l)
        mods_prev = mods

        ckvs.append(ckvn[:m_ctx].reshape(n_ctx, t_ctx, KV_RANK))
        kpes.append(kpe[:m_ctx, SM_KPE:SM_KPE + MLA_ROPE].reshape(n_ctx, t_ctx, MLA_ROPE))
        gdns.append(jnp.stack([sg_f[:n_ctx], sg_b[:n_ctx]], 1))
        ssms.append(jnp.stack([ssm_state_from_lanes(hs_f[:n_ctx]), ssm_state_from_lanes(hs_b[:n_ctx])], 1))

    y_ctx, y_lat = final_norm_apply(lay, x, ytok, mods_prev, final_norm[None, :])
    return (y_ctx.reshape(n_ctx, t_ctx, D_MODEL), y_lat.reshape(n_lat, t_lat, D_MODEL),
            jnp.stack(ckvs, 1), jnp.stack(kpes, 1), jnp.stack(gdns, 1), jnp.stack(ssms, 1))
```

```python
import functools
import math

import numpy as np
import jax
import jax.numpy as jnp
from jax import lax
from jax.experimental import pallas as pl
from jax.experimental.pallas import tpu as pltpu

f32 = jnp.float32
bf16 = jnp.bfloat16

D_MODEL = 2048
DEPTH = 2
GRID_W = 64
EPS = 1e-6
CONV_W = 5
CHUNK = 64

GDN_HEADS = 4
GDN_DK = 128
GDN_DV = 128
GDN_W = GDN_HEADS * GDN_DV
GDN_QKV = 2 * GDN_HEADS * GDN_DK + GDN_HEADS * GDN_DV

SSM_HEADS = 16
SSM_HEADDIM = 64
SSM_GROUPS = 2
SSM_DSTATE = 64
SSM_INNER = SSM_HEADS * SSM_HEADDIM
SSM_XBC = SSM_INNER + 2 * SSM_GROUPS * SSM_DSTATE

MLA_HEADS = 8
MLA_NOPE = 64
MLA_ROPE = 32
MLA_VDIM = 64
Q_RANK = 512
KV_RANK = 256
MLA_W = MLA_HEADS * MLA_VDIM
ROPE_THETA = 10000.0

MIX_W = GDN_W + SSM_INNER + MLA_W
IN_SIZES = (GDN_QKV, GDN_W, 2 * GDN_HEADS, 2 * GDN_HEADS, SSM_INNER, SSM_XBC, 2 * SSM_HEADS, Q_RANK, KV_RANK + MLA_ROPE)

N_EGROUPS = 4
E_PER_GROUP = 4
N_EXPERTS = N_EGROUPS * E_PER_GROUP
EXPERT_FF = 512

LANES = 128
SUBLANES = 8

TB = 256

COL_QKV = 0
COL_GZ = 1536
COL_SZ = 2048
COL_MQ = 3072
COL_CKV = 3584
COL_XBC = 3840
COL_SMALL = 5120
PROJ_W = 5248
SM_GA, SM_GB, SM_DT, SM_KPE = 0, 8, 16, 64


class Layout:
    def __init__(self, n_ctx, t_ctx, n_lat, t_lat):
        self.n_ctx, self.t_ctx, self.n_lat, self.t_lat = n_ctx, t_ctx, n_lat, t_lat
        self.m_ctx = n_ctx * t_ctx
        self.m_lat = n_lat * t_lat
        self.m = self.m_ctx + self.m_lat
        self.n_seq = n_ctx + n_lat
        bc, bl = t_ctx // TB, t_lat // TB
        seq, first, last, mod = [], [], [], []
        for s in range(n_ctx):
            for j in range(bc):
                seq.append(s); first.append(int(j == 0)); last.append(int(j == bc - 1)); mod.append(0)
        for s in range(n_lat):
            for j in range(bl):
                seq.append(n_ctx + s); first.append(int(j == 0)); last.append(int(j == bl - 1)); mod.append(1 + s)
        self.n_blk = len(seq)
        self.seq = np.array(seq, np.int32)
        self.first = np.array(first, np.int32)
        self.last = np.array(last, np.int32)
        self.mod = np.array(mod + mod[-1:], np.int32)
        fwd = np.arange(self.n_blk, dtype=np.int32)
        bwd = []
        i = 0
        while i < self.n_blk:
            j = i
            while self.last[j] == 0:
                j += 1
            bwd.extend(range(j, i - 1, -1))
            i = j + 1
        self.order = {False: fwd, True: np.array(bwd, np.int32)}

    def scan_tables(self, rev):
        order = self.order[rev]
        pad = lambda a: jnp.asarray(np.concatenate([a, a[-1:]]))
        return pad(order), pad(self.seq[order]), pad(self.first[order]), pad(self.last[order])


def _mm(a, b):
    return jnp.dot(a.astype(bf16), b.astype(bf16), preferred_element_type=f32)


def _mm_nt(a, b):
    return lax.dot_general(a.astype(bf16), b.astype(bf16), (((1,), (1,)), ((), ())), preferred_element_type=f32)


def _mm_tn(a, b):
    n = a.shape[1]
    eye = jnp.where(lax.broadcasted_iota(jnp.int32, (n, n), 0) == lax.broadcasted_iota(jnp.int32, (n, n), 1),
                    1.0, 0.0).astype(bf16)
    at = lax.dot_general(eye, a.astype(bf16), (((1,), (1,)), ((), ())), preferred_element_type=f32)
    return jnp.dot(at.astype(bf16), b.astype(bf16), preferred_element_type=f32)


def _split3(x):
    hi = x.astype(bf16)
    r = x - hi.astype(f32)
    mid = r.astype(bf16)
    lo = (r - mid.astype(f32)).astype(bf16)
    return hi, mid, lo


def _mm01(m01, x):
    hi, mid, lo = _split3(x)
    d = lambda t: jnp.dot(m01, t, preferred_element_type=f32)
    return d(hi) + d(mid) + d(lo)


def _transpose_exact(eye, x):
    hi, mid, lo = _split3(x)
    d = lambda t: lax.dot_general(eye, t, (((1,), (1,)), ((), ())), preferred_element_type=f32)
    return d(hi) + d(mid) + d(lo)


def _mm3(a, b):
    ah = a.astype(bf16)
    al = (a - ah.astype(f32)).astype(bf16)
    bh = b.astype(bf16)
    bl = (b - bh.astype(f32)).astype(bf16)
    d = lambda x, y: jnp.dot(x, y, preferred_element_type=f32)
    return d(ah, bh) + d(ah, bl) + d(al, bh)


def _sigmoid(x):
    return 1.0 / (1.0 + jnp.exp(-x))


def _silu(x):
    return x * _sigmoid(x)


def _softplus(x):
    return jnp.maximum(x, 0.0) + jnp.log(1.0 + jnp.exp(-jnp.abs(x)))


def _chunk_masks(rev):
    i = lax.broadcasted_iota(jnp.int32, (CHUNK, CHUNK), 0)
    j = lax.broadcasted_iota(jnp.int32, (CHUNK, CHUNK), 1)
    if rev:
        incl, strict = j >= i, j > i
    else:
        incl, strict = j <= i, j < i
    ll = jnp.where(incl, 1.0, 0.0).astype(bf16)
    lls = jnp.where(incl, 0.0, 1.0).astype(bf16)
    uu = jnp.where(strict, 1.0, 0.0)
    return incl, strict, ll, lls, uu


def _fill_ext(ext_ref, main_ref, prev_ref, next_ref, sfirst, slast):
    ext_ref[0:SUBLANES, :] = jnp.where(sfirst == 1, 0.0, prev_ref[...])
    ext_ref[SUBLANES:SUBLANES + TB, :] = main_ref[...]
    ext_ref[SUBLANES + TB:2 * SUBLANES + TB, :] = jnp.where(slast == 1, 0.0, next_ref[...])


def _conv_tile(ext_ref, w_ref, r0, col0, width, bias=None):
    acc = None
    for t in range(CONV_W):
        rows = ext_ref[pl.ds(SUBLANES + r0 - CONV_W // 2 + t, CHUNK), col0:col0 + width]
        term = rows * w_ref[t:t + 1, col0:col0 + width]
        acc = term if acc is None else acc + term
    if bias is not None:
        acc = acc + bias
    return _silu(acc)


INV_BLOCK = 16


def _block_masks():
    i = lax.broadcasted_iota(jnp.int32, (CHUNK, CHUNK), 0)
    j = lax.broadcasted_iota(jnp.int32, (CHUNK, CHUNK), 1)
    blk16 = (i // INV_BLOCK) == (j // INV_BLOCK)
    blk32 = (i // (2 * INV_BLOCK)) == (j // (2 * INV_BLOCK))
    return blk16, blk32


def _unit_tri_inverse_many(lmats, eye, blk16, blk32):
    ps = list(lmats)
    ld = {p: jnp.where(blk16, lmats[p], 0.0) for p in ps}
    t = {p: eye - ld[p] for p in ps}
    pw = {p: _mm3(ld[p], ld[p]) for p in ps}
    for it in range(3):
        t = {p: t[p] + _mm3(t[p], pw[p]) for p in ps}
        if it < 2:
            pw = {p: _mm3(pw[p], pw[p]) for p in ps}
    in32 = jnp.logical_and(blk32, jnp.logical_not(blk16))
    m = {p: _mm3(t[p], jnp.where(in32, lmats[p], 0.0)) for p in ps}
    t = {p: t[p] - _mm3(m[p], t[p]) for p in ps}
    m = {p: _mm3(t[p], jnp.where(blk32, 0.0, lmats[p])) for p in ps}
    return {p: t[p] - _mm3(m[p], t[p]) for p in ps}


def _gdn_kernel(order_ref, seq_ref, first_ref, last_ref,
                main_ref, prev_ref, next_ref, small_ref, convw_ref, par_ref, s0_ref,
                o_ref, sfin_ref, s_ref, ext_ref, *, rev):
    del order_ref, seq_ref
    step = pl.program_id(0)
    sfirst = first_ref[step]
    slast = last_ref[step]
    pfirst, plast = (slast, sfirst) if rev else (sfirst, slast)
    d = 1 if rev else 0

    @pl.when(pfirst == 1)
    def _():
        s_ref[...] = s0_ref[...]

    _fill_ext(ext_ref, main_ref, prev_ref, next_ref, sfirst, slast)

    incl, strict, ll, lls, uu = _chunk_masks(rev)
    eye = jnp.where(lax.broadcasted_iota(jnp.int32, (CHUNK, CHUNK), 0)
                    == lax.broadcasted_iota(jnp.int32, (CHUNK, CHUNK), 1), 1.0, 0.0)
    blk16, blk32 = _block_masks()
    neg_a = -jnp.exp(par_ref[0:1, :])
    dt_bias = par_ref[1:2, :]
    n_chunks = TB // CHUNK

    chunk_order = [n_chunks - 1 - cc if rev else cc for cc in range(n_chunks)]
    heads = range(GDN_HEADS)
    eye_k = jnp.where(lax.broadcasted_iota(jnp.int32, (GDN_DK, GDN_DK), 0)
                      == lax.broadcasted_iota(jnp.int32, (GDN_DK, GDN_DK), 1), 1.0, 0.0).astype(bf16)

    gates = {}
    for c in chunk_order:
        sm = small_ref[c * CHUNK:(c + 1) * CHUNK, :]
        la_blk = neg_a * _softplus(sm + dt_bias)
        gates[c] = (la_blk, _sigmoid(sm))
    g_blk = {c: _mm01(ll, gates[c][0]) for c in chunk_order}
    gr_blk = {c: _mm01(lls, gates[c][0]) for c in chunk_order}
    gl_blk = {c: jnp.sum(gates[c][0], axis=0, keepdims=True) for c in chunk_order}

    probs = [(c, h) for c in chunk_order for h in heads]
    ia = lambda h: SM_GA + d * GDN_HEADS + h
    ib = lambda h: SM_GB + d * GDN_HEADS + h
    col = lambda blk, lane: blk[:, lane:lane + 1]
    beta = {p: col(gates[p[0]][1], ib(p[1])) for p in probs}
    q, k, v = {}, {}, {}
    for p in probs:
        c, h = p
        r0 = c * CHUNK
        qq = _conv_tile(ext_ref, convw_ref, r0, h * GDN_DK, GDN_DK)
        kx = _conv_tile(ext_ref, convw_ref, r0, GDN_HEADS * GDN_DK + h * GDN_DK, GDN_DK)
        v[p] = _conv_tile(ext_ref, convw_ref, r0, 2 * GDN_HEADS * GDN_DK + h * GDN_DV, GDN_DV)
        q[p] = qq * lax.rsqrt(jnp.sum(qq * qq, axis=-1, keepdims=True) + EPS) * (GDN_DK ** -0.5)
        k[p] = kx * lax.rsqrt(jnp.sum(kx * kx, axis=-1, keepdims=True) + EPS)
    g_t = {c: _transpose_exact(eye_k, g_blk[c]) for c in chunk_order}
    dmat = {p: jnp.where(incl, col(g_blk[p[0]], ia(p[1])) - g_t[p[0]][ia(p[1]):ia(p[1]) + 1, :], 0.0)
            for p in probs}
    kk = {p: _mm_nt(k[p], k[p]) for p in probs}
    qk = {p: _mm_nt(q[p], k[p]) for p in probs}
    decay = {p: jnp.where(incl, jnp.exp(dmat[p]), 0.0) for p in probs}
    lmat = {p: jnp.where(strict, kk[p] * beta[p] * decay[p], 0.0) for p in probs}
    qk = {p: jnp.where(incl, qk[p] * decay[p], 0.0) for p in probs}
    tinv = _unit_tri_inverse_many(lmat, eye, blk16, blk32)
    uw = {}
    for p in probs:
        e_g = jnp.exp(col(g_blk[p[0]], ia(p[1])))
        rhs = jnp.concatenate([v[p] * beta[p], k[p] * beta[p] * e_g], axis=1)
        uw[p] = _mm3(tinv[p], rhs)
        q[p] = q[p] * e_g
    kdt = {}
    for p in probs:
        k_dec = (k[p] * jnp.exp(col(gr_blk[p[0]], ia(p[1])))).astype(bf16)
        kdt[p] = lax.dot_general(eye_k, k_dec, (((1,), (1,)), ((), ())), preferred_element_type=f32)

    for c in chunk_order:
        ps = [(c, h) for h in heads]
        s_old = {p: s_ref[p[1]] for p in ps}
        ws = {p: _mm(uw[p][:, GDN_DV:], s_old[p]) for p in ps}
        qs = {p: _mm(q[p], s_old[p]) for p in ps}
        v_new = {p: uw[p][:, :GDN_DV] - ws[p] for p in ps}
        o = {p: qs[p] + _mm(qk[p], v_new[p]) for p in ps}
        for p in ps:
            h = p[1]
            s_ref[h] = s_old[p] * jnp.exp(col(gl_blk[c], ia(h))) + _mm(kdt[p], v_new[p])
            o_ref[c * CHUNK:(c + 1) * CHUNK, h * GDN_DV:(h + 1) * GDN_DV] = o[p]

    @pl.when(plast == 1)
    def _():
        sfin_ref[...] = s_ref[...]


def _halo_maps(n_rows8):
    prev_map = lambda i, order, *_: (jnp.maximum(order[i] * (TB // SUBLANES) - 1, 0), 0)
    next_map = lambda i, order, *_: (jnp.minimum((order[i] + 1) * (TB // SUBLANES), n_rows8 - 1), 0)
    return prev_map, next_map


def gdn_scan(lay, proj, convw, par, s0, layer, rev):
    prev_map, next_map = _halo_maps(lay.m // SUBLANES)
    blk_map = lambda i, order, *_: (order[i], 0)
    seq_map = lambda i, order, seq, *_: (seq[i], 0, 0, 0)
    grid_spec = pltpu.PrefetchScalarGridSpec(
        num_scalar_prefetch=4,
        grid=(lay.n_blk,),
        in_specs=[
            pl.BlockSpec((TB, GDN_QKV), blk_map),
            pl.BlockSpec((SUBLANES, GDN_QKV), prev_map),
            pl.BlockSpec((SUBLANES, GDN_QKV), next_map),
            pl.BlockSpec((TB, LANES), lambda i, order, *_: (order[i], COL_SMALL // LANES)),
            pl.BlockSpec((None, CONV_W, GDN_QKV), lambda i, *_: (layer, 0, 0)),
            pl.BlockSpec((None, SUBLANES, LANES), lambda i, *_: (layer, 0, 0)),
            pl.BlockSpec((None, GDN_HEADS, GDN_DK, GDN_DV), seq_map),
        ],
        out_specs=[
            pl.BlockSpec((TB, GDN_W), blk_map),
            pl.BlockSpec((None, GDN_HEADS, GDN_DK, GDN_DV), seq_map),
        ],
        scratch_shapes=[
            pltpu.VMEM((GDN_HEADS, GDN_DK, GDN_DV), f32),
            pltpu.VMEM((TB + 2 * SUBLANES, GDN_QKV), f32),
        ],
    )
    return pl.pallas_call(
        functools.partial(_gdn_kernel, rev=rev),
        grid_spec=grid_spec,
        out_shape=[
            jax.ShapeDtypeStruct((lay.m, GDN_W), f32),
            jax.ShapeDtypeStruct((lay.n_seq, GDN_HEADS, GDN_DK, GDN_DV), f32),
        ],
        compiler_params=pltpu.CompilerParams(dimension_semantics=("arbitrary",)),
        name="gdn_bwd" if rev else "gdn_fwd",
    )(*lay.scan_tables(rev), proj, proj, proj, proj, convw, par, s0)


SSM_PAIRS = SSM_HEADS // 2
PAIRS_PER_GROUP = SSM_PAIRS // SSM_GROUPS
COL_B = SSM_INNER
COL_C = SSM_INNER + SSM_GROUPS * SSM_DSTATE


def _ssd_kernel(order_ref, seq_ref, first_ref, last_ref,
                main_ref, prev_ref, next_ref, small_ref, convw_ref, cpar_ref, par_ref, h0_ref,
                y_ref, hfin_ref, h_ref, ext_ref, *, rev):
    del order_ref, seq_ref
    step = pl.program_id(0)
    sfirst = first_ref[step]
    slast = last_ref[step]
    pfirst, plast = (slast, sfirst) if rev else (sfirst, slast)
    d = 1 if rev else 0

    @pl.when(pfirst == 1)
    def _():
        h_ref[...] = h0_ref[...]

    _fill_ext(ext_ref, main_ref, prev_ref, next_ref, sfirst, slast)

    incl, _, ll, lls, uu = _chunk_masks(rev)
    lane_lo = lax.broadcasted_iota(jnp.int32, (CHUNK, LANES), 1) < SSM_DSTATE
    row_lo = lax.broadcasted_iota(jnp.int32, (LANES, LANES), 0) < SSM_HEADDIM
    eye_p = jnp.where(lax.broadcasted_iota(jnp.int32, (LANES, LANES), 0)
                      == lax.broadcasted_iota(jnp.int32, (LANES, LANES), 1), 1.0, 0.0).astype(bf16)
    neg_a = -jnp.exp(par_ref[0:1, :])
    dt_bias = par_ref[1:2, :]
    n_chunks = TB // CHUNK

    for cc in range(n_chunks):
        c = n_chunks - 1 - cc if rev else cc
        r0 = c * CHUNK
        sm = small_ref[r0:r0 + CHUNK, :]
        dt_blk = _softplus(sm + dt_bias)
        dta_blk = dt_blk * neg_a
        acum_blk = _mm01(ll, dta_blk)
        ar_blk = _mm01(lls, dta_blk)
        al_blk = jnp.sum(dta_blk, axis=0, keepdims=True)
        b_pair = _conv_tile(ext_ref, convw_ref, r0, COL_B, LANES, cpar_ref[0:1, COL_B:COL_B + LANES])
        c_pair = _conv_tile(ext_ref, convw_ref, r0, COL_C, LANES, cpar_ref[0:1, COL_C:COL_C + LANES])
        c_g, b_g, cb = [], [], []
        for g in range(SSM_GROUPS):
            gmask = lane_lo if g == 0 else jnp.logical_not(lane_lo)
            c_g.append(jnp.where(gmask, c_pair, 0.0))
            b_g.append(jnp.where(gmask, b_pair, 0.0))
        cb = [_mm_nt(c_g[g], b_pair) for g in range(SSM_GROUPS)]
        pairs = range(SSM_PAIRS)
        grp = lambda p: p // PAIRS_PER_GROUP
        lane_of = lambda hh: SM_DT + d * SSM_HEADS + hh
        colv = lambda blk, hh: blk[:, lane_of(hh):lane_of(hh) + 1]
        both = lambda blk, p: jnp.where(lane_lo, colv(blk, 2 * p), colv(blk, 2 * p + 1))
        xs = [_conv_tile(ext_ref, convw_ref, r0, p * LANES, LANES, cpar_ref[0:1, p * LANES:(p + 1) * LANES])
              for p in pairs]
        acum_t = _transpose_exact(eye_p, acum_blk)
        dmat = [jnp.where(incl, colv(acum_blk, hh) - acum_t[lane_of(hh):lane_of(hh) + 1, :], 0.0)
                for hh in range(SSM_HEADS)]
        m_h = [cb[grp(hh // 2)] * jnp.where(incl, jnp.exp(dmat[hh]), 0.0) for hh in range(SSM_HEADS)]
        xdt = [xs[p] * both(dt_blk, p) for p in pairs]
        y = [jnp.where(lane_lo, _mm(m_h[2 * p], xdt[p]), _mm(m_h[2 * p + 1], xdt[p])) for p in pairs]
        xdt_t = [lax.dot_general(eye_p, xdt[p].astype(bf16), (((1,), (1,)), ((), ())),
                                 preferred_element_type=f32).astype(bf16) for p in pairs]
        st_new = [jnp.where(row_lo,
                            jnp.dot(xdt_t[p], (b_g[grp(p)] * jnp.exp(colv(ar_blk, 2 * p))).astype(bf16),
                                    preferred_element_type=f32),
                            jnp.dot(xdt_t[p], (b_g[grp(p)] * jnp.exp(colv(ar_blk, 2 * p + 1))).astype(bf16),
                                    preferred_element_type=f32)) for p in pairs]
        h_pair = [h_ref[p * LANES:(p + 1) * LANES, :] for p in pairs]
        y_off = [_mm_nt(c_g[grp(p)], h_pair[p]) for p in pairs]
        for p in pairs:
            col = p * LANES
            yp = y[p] + y_off[p] * jnp.exp(both(acum_blk, p))
            if not rev:
                yp = yp + xs[p] * cpar_ref[1:2, col:col + LANES]
            dec = jnp.where(row_lo, jnp.exp(colv(al_blk, 2 * p)), jnp.exp(colv(al_blk, 2 * p + 1)))
            h_ref[col:col + LANES, :] = h_pair[p] * dec + st_new[p]
            y_ref[r0:r0 + CHUNK, col:col + LANES] = yp

    @pl.when(plast == 1)
    def _():
        hfin_ref[...] = h_ref[...]


def ssd_scan(lay, proj, convw, cpar, par, h0, layer, rev):
    prev_map, next_map = _halo_maps(lay.m // SUBLANES)
    cblk = COL_XBC // SSM_XBC
    blk_map = lambda i, order, *_: (order[i], 0)
    seq_map = lambda i, order, seq, *_: (seq[i], 0, 0)
    grid_spec = pltpu.PrefetchScalarGridSpec(
        num_scalar_prefetch=4,
        grid=(lay.n_blk,),
        in_specs=[
            pl.BlockSpec((TB, SSM_XBC), lambda i, order, *_: (order[i], cblk)),
            pl.BlockSpec((SUBLANES, SSM_XBC), lambda i, *a: (prev_map(i, *a)[0], cblk)),
            pl.BlockSpec((SUBLANES, SSM_XBC), lambda i, *a: (next_map(i, *a)[0], cblk)),
            pl.BlockSpec((TB, LANES), lambda i, order, *_: (order[i], COL_SMALL // LANES)),
            pl.BlockSpec((None, CONV_W, SSM_XBC), lambda i, *_: (layer, 0, 0)),
            pl.BlockSpec((None, SUBLANES, SSM_XBC), lambda i, *_: (layer, 0, 0)),
            pl.BlockSpec((None, SUBLANES, LANES), lambda i, *_: (layer, 0, 0)),
            pl.BlockSpec((None, SSM_INNER, LANES), seq_map),
        ],
        out_specs=[
            pl.BlockSpec((TB, SSM_INNER), blk_map),
            pl.BlockSpec((None, SSM_INNER, LANES), seq_map),
        ],
        scratch_shapes=[
            pltpu.VMEM((SSM_INNER, LANES), f32),
            pltpu.VMEM((TB + 2 * SUBLANES, SSM_XBC), f32),
        ],
    )
    return pl.pallas_call(
        functools.partial(_ssd_kernel, rev=rev),
        grid_spec=grid_spec,
        out_shape=[
            jax.ShapeDtypeStruct((lay.m, SSM_INNER), f32),
            jax.ShapeDtypeStruct((lay.n_seq, SSM_INNER, LANES), f32),
        ],
        compiler_params=pltpu.CompilerParams(dimension_semantics=("arbitrary",)),
        name="ssd_bwd" if rev else "ssd_fwd",
    )(*lay.scan_tables(rev), proj, proj, proj, proj, convw, cpar, par, h0)


def ssm_state_to_lanes(h):
    n = h.shape[0]
    hg = h.reshape(n, SSM_GROUPS, SSM_HEADS // SSM_GROUPS * SSM_HEADDIM, SSM_DSTATE)
    z = jnp.zeros_like(hg[:, 0])
    return jnp.concatenate([jnp.concatenate([hg[:, 0], z], -1), jnp.concatenate([z, hg[:, 1]], -1)], 1)


def ssm_state_from_lanes(hl):
    n = hl.shape[0]
    r = hl.reshape(n, SSM_GROUPS, SSM_INNER // SSM_GROUPS, SSM_GROUPS, SSM_DSTATE)
    h = jnp.stack([r[:, g, :, g, :] for g in range(SSM_GROUPS)], 1)
    return h.reshape(n, SSM_HEADS, SSM_HEADDIM, SSM_DSTATE)


ADA_TN = 1536
VMEM_LIMIT = 56 * 1024 * 1024


def _ada_kernel(c_ref, w_ref, b_ref, o_ref):
    s = _silu(c_ref[...])
    o_ref[...] = jnp.dot(s.astype(bf16), w_ref[...].astype(bf16), preferred_element_type=f32) + b_ref[...]


def ada_mod(cvec, w_ada, b_ada, layer):
    n = 6 * D_MODEL
    out = pl.pallas_call(
        _ada_kernel,
        grid=(n // ADA_TN,),
        in_specs=[
            pl.BlockSpec((SUBLANES, D_MODEL), lambda j: (0, 0)),
            pl.BlockSpec((None, D_MODEL, ADA_TN), lambda j: (layer, 0, j)),
            pl.BlockSpec((None, 1, ADA_TN), lambda j: (layer, 0, j)),
        ],
        out_specs=pl.BlockSpec((SUBLANES, ADA_TN), lambda j: (0, j)),
        out_shape=jax.ShapeDtypeStruct((SUBLANES, n), f32),
        compiler_params=pltpu.CompilerParams(dimension_semantics=("arbitrary",), vmem_limit_bytes=VMEM_LIMIT),
        name="ada_mod",
    )(cvec, w_ada, b_ada)
    return out.reshape(SUBLANES, 6, D_MODEL)


MOD_SH1, MOD_SC1, MOD_G1, MOD_SH2, MOD_SC2, MOD_G2 = range(6)


def _rms(x):
    return x * lax.rsqrt(jnp.mean(x * x, axis=-1, keepdims=True) + EPS)


PROJ_CHUNK = 512


def _pick_rows(i, n_first, refs):
    if len(refs) == 1:
        return refs[0][...]
    return jnp.where(i < n_first, refs[0][...], refs[1][...])


def _row_specs(lay, x, width):
    if not isinstance(x, tuple):
        return [pl.BlockSpec((TB, width), lambda i, mod: (jnp.minimum(i, lay.n_blk - 1), 0))], [x]
    nbc, nbl = lay.m_ctx // TB, lay.m_lat // TB
    return ([pl.BlockSpec((TB, width), lambda i, mod: (jnp.minimum(i, nbc - 1), 0)),
             pl.BlockSpec((TB, width), lambda i, mod: (jnp.clip(i - nbc, 0, nbl - 1), 0))], list(x))


def _in_proj_kernel(mod_ref, *refs, has_res, n_x, n_first):
    del mod_ref
    i = pl.program_id(0)
    x_refs, refs = refs[:n_x], refs[n_x:]
    if has_res:
        y_ref, pmods_ref, mods_ref, n1_ref, w_ref, proj_ref, xo_ref = refs
        x = _pick_rows(i, n_first, x_refs) + pmods_ref[MOD_G2:MOD_G2 + 1, :] * y_ref[...]
        xo_ref[...] = x
    else:
        mods_ref, n1_ref, w_ref, proj_ref = refs
        x = _pick_rows(i, n_first, x_refs)
    h = _rms(x) * n1_ref[...]
    h = h * (1.0 + mods_ref[MOD_SC1:MOD_SC1 + 1, :]) + mods_ref[MOD_SH1:MOD_SH1 + 1, :]
    hb = h.astype(bf16)
    for c0 in range(0, PROJ_W, PROJ_CHUNK):
        wd = min(PROJ_CHUNK, PROJ_W - c0)
        proj_ref[:, c0:c0 + wd] = jnp.dot(hb, w_ref[:, c0:c0 + wd], preferred_element_type=f32)


def in_proj(lay, x, mods, norm1, w_in_r, layer, res=None):
    last = lay.n_blk - 1
    row_map = lambda i, mod: (jnp.minimum(i, last), 0)
    mod_map = lambda i, mod: (mod[jnp.minimum(i, last)], 0, 0)
    in_specs, args = _row_specs(lay, x, D_MODEL)
    n_x = len(args)
    if res is not None:
        in_specs += [pl.BlockSpec((TB, D_MODEL), row_map), pl.BlockSpec((None, 6, D_MODEL), mod_map)]
        args += [res[0], res[1]]
    in_specs += [
        pl.BlockSpec((None, 6, D_MODEL), mod_map),
        pl.BlockSpec((None, 1, D_MODEL), lambda i, mod: (layer, 0, 0)),
        pl.BlockSpec((D_MODEL, PROJ_W), lambda i, mod: (0, 0), pipeline_mode=pl.Buffered(1)),
    ]
    args += [mods, norm1, w_in_r]
    out_specs = [pl.BlockSpec((TB, PROJ_W), row_map)]
    out_shape = [jax.ShapeDtypeStruct((lay.m, PROJ_W), f32)]
    if res is not None:
        out_specs.append(pl.BlockSpec((TB, D_MODEL), row_map))
        out_shape.append(jax.ShapeDtypeStruct((lay.m, D_MODEL), f32))
    outs = pl.pallas_call(
        functools.partial(_in_proj_kernel, has_res=res is not None, n_x=n_x, n_first=lay.m_ctx // TB),
        grid_spec=pltpu.PrefetchScalarGridSpec(
            num_scalar_prefetch=1, grid=(lay.n_blk,), in_specs=in_specs, out_specs=out_specs),
        out_shape=out_shape,
        compiler_params=pltpu.CompilerParams(dimension_semantics=("arbitrary",), vmem_limit_bytes=VMEM_LIMIT),
        name="in_proj",
    )(jnp.asarray(lay.mod), *args)
    return (outs[0], outs[1]) if res is not None else (outs[0], x)


MLA_QK = MLA_NOPE + MLA_ROPE
Q_HEADS_W = MLA_HEADS * LANES
ROPE_HALF = MLA_ROPE // 2


def _rope_lanes(x, c, s1, s2):
    return x * c + pltpu.roll(x, ROPE_HALF, 1) * s1 + pltpu.roll(x, LANES - ROPE_HALF, 1) * s2


def _mla_prep_kernel(pos_ref, mq_ref, ckv_ref, small_ref, qn_ref, kvn_ref, wuq_ref, rc_ref, rs1_ref, rs2_ref,
                     q_ref, ckvn_ref, kpe_ref):
    del pos_ref
    c, s1, s2 = rc_ref[...], rs1_ref[...], rs2_ref[...]
    cq = (_rms(mq_ref[...]) * qn_ref[...]).astype(bf16)
    qf = jnp.dot(cq, wuq_ref[...], preferred_element_type=f32)
    scale = MLA_QK ** -0.5 * math.log2(math.e)
    for h in range(MLA_HEADS):
        xh = qf[:, h * LANES:(h + 1) * LANES]
        q_ref[:, h * LANES:(h + 1) * LANES] = (_rope_lanes(xh, c, s1, s2) * scale).astype(bf16)
    ckvn_ref[...] = _rms(ckv_ref[...]) * kvn_ref[...]
    lane = lax.broadcasted_iota(jnp.int32, (TB, LANES), 1)
    in_rope = jnp.logical_and(lane >= SM_KPE, lane < SM_KPE + MLA_ROPE)
    kpe_ref[...] = jnp.where(in_rope, _rope_lanes(small_ref[...], c, s1, s2), 0.0)


def rope_tables(lay):
    t = lay.t_lat
    rows = t // GRID_W
    row = jnp.repeat(jnp.arange(rows, dtype=f32), GRID_W)
    col = jnp.tile(jnp.arange(GRID_W, dtype=f32), rows)
    nf = MLA_ROPE // 4
    inv = jnp.power(ROPE_THETA, -jnp.arange(nf, dtype=f32) / nf)
    ang = jnp.concatenate([row[:, None] * inv, col[:, None] * inv], axis=-1)
    cos, sin = jnp.cos(ang), jnp.sin(ang)
    z = lambda n: jnp.zeros((t, n), f32)
    c = jnp.concatenate([jnp.ones((t, MLA_NOPE), f32), cos, cos, z(LANES - MLA_QK)], -1)
    s1 = jnp.concatenate([z(MLA_NOPE + ROPE_HALF), sin, z(LANES - MLA_QK)], -1)
    s2 = jnp.concatenate([z(MLA_NOPE), -sin, z(ROPE_HALF + LANES - MLA_QK)], -1)
    ident = jnp.concatenate([jnp.ones((TB, MLA_QK), f32), jnp.zeros((TB, LANES - MLA_QK), f32)], -1)
    zero = jnp.zeros((TB, LANES), f32)
    return (jnp.concatenate([ident, c]), jnp.concatenate([zero, s1]), jnp.concatenate([zero, s2]))


def _pos_blocks(lay):
    pos = [0] * (lay.m_ctx // TB)
    for _ in range(lay.n_lat):
        pos += [1 + j for j in range(lay.t_lat // TB)]
    return np.array(pos + pos[-1:], np.int32)


def mla_prep(lay, proj, q_norm, kv_norm, w_uq_r, tables, layer):
    row_map = lambda i, pos: (i, 0)
    tab_map = lambda i, pos: (pos[i], 0)
    return pl.pallas_call(
        _mla_prep_kernel,
        grid_spec=pltpu.PrefetchScalarGridSpec(
            num_scalar_prefetch=1, grid=(lay.n_blk,),
            in_specs=[
                pl.BlockSpec((TB, Q_RANK), lambda i, pos: (i, COL_MQ // Q_RANK)),
                pl.BlockSpec((TB, KV_RANK), lambda i, pos: (i, COL_CKV // KV_RANK)),
                pl.BlockSpec((TB, LANES), lambda i, pos: (i, COL_SMALL // LANES)),
                pl.BlockSpec((None, 1, Q_RANK), lambda i, pos: (layer, 0, 0)),
                pl.BlockSpec((None, 1, KV_RANK), lambda i, pos: (layer, 0, 0)),
                pl.BlockSpec((Q_RANK, Q_HEADS_W), lambda i, pos: (0, 0)),
                pl.BlockSpec((TB, LANES), tab_map),
                pl.BlockSpec((TB, LANES), tab_map),
                pl.BlockSpec((TB, LANES), tab_map),
            ],
            out_specs=[
                pl.BlockSpec((TB, Q_HEADS_W), row_map),
                pl.BlockSpec((TB, KV_RANK), row_map),
                pl.BlockSpec((TB, LANES), row_map),
            ]),
        out_shape=[
            jax.ShapeDtypeStruct((lay.m, Q_HEADS_W), bf16),
            jax.ShapeDtypeStruct((lay.m, KV_RANK), f32),
            jax.ShapeDtypeStruct((lay.m, LANES), f32),
        ],
        compiler_params=pltpu.CompilerParams(dimension_semantics=("arbitrary",)),
        name="mla_prep",
    )(jnp.asarray(_pos_blocks(lay)), proj, proj, proj, q_norm, kv_norm, w_uq_r, *tables)


V_PAIRS = MLA_HEADS // 2
KV_UP_W = 2 * MLA_HEADS * LANES


def _kv_up_kernel(ckv_ref, kpe_ref, w_ref, k_ref, v_ref):
    kv = jnp.dot(ckv_ref[...].astype(bf16), w_ref[...], preferred_element_type=f32)
    kpe = kpe_ref[...]
    for h in range(MLA_HEADS):
        k_ref[h] = (kv[:, h * LANES:(h + 1) * LANES] + kpe).astype(bf16)
    v0 = MLA_HEADS * LANES
    lane_lo = lax.broadcasted_iota(jnp.int32, (TB, LANES), 1) < MLA_VDIM
    for h in range(MLA_HEADS):
        v_ref[h] = jnp.where(lane_lo, kv[:, v0 + h * LANES:v0 + (h + 1) * LANES], 1.0).astype(bf16)


def kv_up(ckv, kpe, w_ukv_r):
    r = ckv.shape[0]
    return pl.pallas_call(
        _kv_up_kernel,
        grid=(r // TB,),
        in_specs=[
            pl.BlockSpec((TB, KV_RANK), lambda i: (i, 0)),
            pl.BlockSpec((TB, LANES), lambda i: (i, 0)),
            pl.BlockSpec((KV_RANK, KV_UP_W), lambda i: (0, 0)),
        ],
        out_specs=[
            pl.BlockSpec((MLA_HEADS, TB, LANES), lambda i: (0, i, 0)),
            pl.BlockSpec((MLA_HEADS, TB, LANES), lambda i: (0, i, 0)),
        ],
        out_shape=[
            jax.ShapeDtypeStruct((MLA_HEADS, r, LANES), bf16),
            jax.ShapeDtypeStruct((MLA_HEADS, r, LANES), bf16),
        ],
        compiler_params=pltpu.CompilerParams(dimension_semantics=("arbitrary",)),
        name="kv_up",
    )(ckv, kpe, w_ukv_r)


ATT_KC = 2304


def _attn_kernel(q_ref, k_ref, v_ref, o_ref, m_ref, acc_ref, *, n_keys):
    lane_lo = lax.broadcasted_iota(jnp.int32, (TB, LANES), 1) < MLA_VDIM
    kc = min(ATT_KC, n_keys)
    hs = range(MLA_HEADS)
    m_ref[...] = jnp.full((MLA_HEADS, TB, LANES), -jnp.inf, f32)
    acc_ref[...] = jnp.zeros((MLA_HEADS, TB, LANES), f32)

    def body(c, carry):
        c0 = pl.multiple_of(c * kc, kc)
        score = lambda h: lax.dot_general(q_ref[:, h * LANES:(h + 1) * LANES], k_ref[h, pl.ds(c0, kc), :],
                                          (((1,), (1,)), ((), ())), preferred_element_type=f32)
        def softmax_part(h, s):
            m_old = m_ref[h]
            m_new = jnp.maximum(m_old, jnp.max(s, axis=-1, keepdims=True))
            m_ref[h] = m_new
            return jnp.exp2(m_old - m_new), jnp.exp2(s - m_new[:, 0:1]).astype(bf16)

        def value_part(h, alpha, pe):
            pv = jnp.dot(pe, v_ref[h, pl.ds(c0, kc), :], preferred_element_type=f32)
            acc_ref[h] = alpha * acc_ref[h] + pv

        scores, probs = {0: score(0), 1: score(1)}, {}
        probs[0] = softmax_part(0, scores.pop(0))
        for h in hs:
            if h + 2 < MLA_HEADS:
                scores[h + 2] = score(h + 2)
            if h + 1 < MLA_HEADS:
                probs[h + 1] = softmax_part(h + 1, scores.pop(h + 1))
            value_part(h, *probs.pop(h))
        return carry

    lax.fori_loop(0, n_keys // kc, body, 0)
    half = LANES // 2
    for p in range(V_PAIRS):
        a_even = acc_ref[2 * p]
        a_odd = pltpu.roll(acc_ref[2 * p + 1], half, 1)
        o_even = a_even * pltpu.roll(1.0 / a_even, half, 1)
        o_odd = a_odd * pltpu.roll(1.0 / a_odd, half, 1)
        o_ref[:, p * LANES:(p + 1) * LANES] = jnp.where(lane_lo, o_even, o_odd)


def attention(q, k, v, q_blk0, n_seq, t_q, n_keys):
    qb = t_q // TB
    return pl.pallas_call(
        functools.partial(_attn_kernel, n_keys=n_keys),
        grid=(n_seq, qb),
        in_specs=[
            pl.BlockSpec((TB, Q_HEADS_W), lambda b, j: (q_blk0 + b * qb + j, 0)),
            pl.BlockSpec((MLA_HEADS, n_keys, LANES), lambda b, j: (0, b, 0)),
            pl.BlockSpec((MLA_HEADS, n_keys, LANES), lambda b, j: (0, b, 0)),
        ],
        out_specs=pl.BlockSpec((TB, MLA_W), lambda b, j: (b * qb + j, 0)),
        out_shape=jax.ShapeDtypeStruct((n_seq * t_q, MLA_W), f32),
        scratch_shapes=[pltpu.VMEM((MLA_HEADS, TB, LANES), f32)] * 2,
        compiler_params=pltpu.CompilerParams(dimension_semantics=("arbitrary", "arbitrary"),
                                             vmem_limit_bytes=VMEM_LIMIT),
        name="mla_attn",
    )(q, k, v)


N_PAIRS = E_PER_GROUP * (E_PER_GROUP - 1) // 2
N_BUCKETS = N_EGROUPS * N_PAIRS
RT_BUCKET, RT_WLO, RT_WHI = 0, 1, 2
H2X_W = D_MODEL + LANES


def _route(lg):
    lane = lax.broadcasted_iota(jnp.int32, lg.shape, 1)
    big = jnp.int32(LANES)
    rmax = lambda x: jnp.max(x, axis=-1, keepdims=True)
    rmin = lambda x: jnp.min(x, axis=-1, keepdims=True)
    rsum = lambda x: jnp.sum(x, axis=-1, keepdims=True)
    neg = -jnp.inf
    gmask = lane < N_EGROUPS
    gl = jnp.where(gmask, lg, neg)
    gmax = rmax(gl)
    gsum = rsum(jnp.where(gmask, jnp.exp(lg - gmax), 0.0))
    gsel = rmin(jnp.where(gl == gmax, lane, big))
    pg_sel = 1.0 / gsum
    e0 = N_EGROUPS + E_PER_GROUP * gsel
    emask = jnp.logical_and(lane >= e0, lane < e0 + E_PER_GROUP)
    el = jnp.where(emask, lg, neg)
    emax = rmax(el)
    ee = jnp.where(emask, jnp.exp(lg - emax), 0.0)
    pe = ee / rsum(ee)
    pe_m = jnp.where(emask, pe, -1.0)
    v1 = rmax(pe_m)
    i1 = rmin(jnp.where(pe_m == v1, lane, big))
    pe_m2 = jnp.where(lane == i1, -1.0, pe_m)
    v2 = rmax(pe_m2)
    i2 = rmin(jnp.where(pe_m2 == v2, lane, big))
    w1 = pg_sel * v1 / (v1 + v2)
    w2 = pg_sel * v2 / (v1 + v2)
    a1, a2 = i1 - e0, i2 - e0
    lo, hi = jnp.minimum(a1, a2), jnp.maximum(a1, a2)
    pair = (lo * (2 * E_PER_GROUP - 1 - lo)) // 2 + (hi - lo - 1)
    bucket = (gsel * N_PAIRS + pair).astype(f32)
    w_lo = jnp.where(a1 < a2, w1, w2)
    w_hi = jnp.where(a1 < a2, w2, w1)
    return jnp.where(lane == RT_BUCKET, bucket,
                     jnp.where(lane == RT_WLO, w_lo, jnp.where(lane == RT_WHI, w_hi, 0.0)))


def _out_proj_kernel(mod_ref, *refs, n_x, n_first):
    del mod_ref
    x_refs = refs[:n_x]
    (of_ref, ob_ref, gz_ref, yf_ref, yb_ref, sz_ref, om_ref, mods_ref,
     gn_ref, sn_ref, n2_ref, wout_ref, wr_ref, br_ref, x1_ref, h2_ref, rt_ref) = refs[n_x:]
    acc = None
    for h in range(GDN_HEADS):
        sl = slice(h * GDN_DV, (h + 1) * GDN_DV)
        o = _rms(of_ref[:, sl] + ob_ref[:, sl]) * gn_ref[...] * _silu(gz_ref[:, sl])
        t = jnp.dot(o.astype(bf16), wout_ref[sl, :], preferred_element_type=f32)
        acc = t if acc is None else acc + t
    y = (yf_ref[...] + yb_ref[...]) * _silu(sz_ref[...])
    ys = _rms(y) * sn_ref[...]
    acc = acc + jnp.dot(ys.astype(bf16), wout_ref[GDN_W:GDN_W + SSM_INNER, :], preferred_element_type=f32)
    acc = acc + jnp.dot(om_ref[...].astype(bf16), wout_ref[GDN_W + SSM_INNER:MIX_W, :],
                        preferred_element_type=f32)
    x1 = _pick_rows(pl.program_id(0), n_first, x_refs) + mods_ref[MOD_G1:MOD_G1 + 1, :] * acc
    x1_ref[...] = x1
    h2 = _rms(x1) * n2_ref[...]
    h2 = h2 * (1.0 + mods_ref[MOD_SC2:MOD_SC2 + 1, :]) + mods_ref[MOD_SH2:MOD_SH2 + 1, :]
    h2_hi = h2.astype(bf16)
    h2_lo = (h2 - h2_hi.astype(f32)).astype(bf16)
    r = jnp.dot(h2_hi, wr_ref[...], preferred_element_type=f32)
    logits = r[:, 0:LANES] + r[:, LANES:2 * LANES] + jnp.dot(h2_lo, wr_ref[:, 0:LANES], preferred_element_type=f32)
    rt = _route(logits + br_ref[...])
    h2_ref[:, 0:D_MODEL] = h2
    h2_ref[:, D_MODEL:H2X_W] = rt
    rt_ref[...] = rt


def out_proj(lay, o_f, o_b, proj, y_f, y_b, o_mla, x, mods, gdn_norm, ssm_norm, norm2, w_out_b, w_rt, b_rt, layer):
    row_map = lambda i, mod: (i, 0)
    lyr = lambda i, mod: (layer, 0, 0)
    x_specs, x_args = _row_specs(lay, x, D_MODEL)
    return pl.pallas_call(
        functools.partial(_out_proj_kernel, n_x=len(x_args), n_first=lay.m_ctx // TB),
        grid_spec=pltpu.PrefetchScalarGridSpec(
            num_scalar_prefetch=1, grid=(lay.n_blk,),
            in_specs=x_specs + [
                pl.BlockSpec((TB, GDN_W), row_map),
                pl.BlockSpec((TB, GDN_W), row_map),
                pl.BlockSpec((TB, GDN_W), lambda i, mod: (i, COL_GZ // GDN_W)),
                pl.BlockSpec((TB, SSM_INNER), row_map),
                pl.BlockSpec((TB, SSM_INNER), row_map),
                pl.BlockSpec((TB, SSM_INNER), lambda i, mod: (i, COL_SZ // SSM_INNER)),
                pl.BlockSpec((TB, MLA_W), row_map),
                pl.BlockSpec((None, 6, D_MODEL), lambda i, mod: (mod[i], 0, 0)),
                pl.BlockSpec((None, 1, GDN_DV), lyr),
                pl.BlockSpec((None, 1, SSM_INNER), lyr),
                pl.BlockSpec((None, 1, D_MODEL), lyr),
                pl.BlockSpec((None, MIX_W, D_MODEL), lyr, pipeline_mode=pl.Buffered(1)),
                pl.BlockSpec((None, D_MODEL, 2 * LANES), lyr),
                pl.BlockSpec((None, 1, LANES), lyr),
            ],
            out_specs=[
                pl.BlockSpec((TB, D_MODEL), row_map),
                pl.BlockSpec((TB, H2X_W), row_map),
                pl.BlockSpec((TB, LANES), row_map),
            ]),
        out_shape=[
            jax.ShapeDtypeStruct((lay.m, D_MODEL), f32),
            jax.ShapeDtypeStruct((lay.m, H2X_W), f32),
            jax.ShapeDtypeStruct((lay.m, LANES), f32),
        ],
        compiler_params=pltpu.CompilerParams(dimension_semantics=("arbitrary",), vmem_limit_bytes=VMEM_LIMIT),
        name="out_proj",
    )(jnp.asarray(lay.mod), *x_args, o_f, o_b, proj, y_f, y_b, proj, o_mla, mods, gdn_norm, ssm_norm, norm2,
      w_out_b, w_rt, b_rt)


TME = 256
PAIR_LO = (0, 0, 0, 1, 1, 2)
PAIR_HI = (1, 2, 3, 2, 3, 3)


DMA_UNROLL = 8


def _moe_kernel(te_ref, nu_ref, ts_ref, nv_ref, order_ref,
                h2_hbm, wg_ref, wu_ref, wd_ref, y_hbm, xbuf, ybuf, gsem, ssem):
    del te_ref
    t = pl.program_id(0)
    j = pl.program_id(1)
    n_used = nu_ref[0]
    used = t < n_used
    slot = t % 2
    n_tok = y_hbm.shape[0] - TME

    def gather_rows(tile, sl):
        base = ts_ref[tile]

        def issue(r, c):
            pltpu.make_async_copy(h2_hbm.at[pl.ds(order_ref[base + r], 1), :],
                                  xbuf.at[sl, pl.ds(r, 1), :], gsem.at[sl]).start()
            return c
        lax.fori_loop(0, TME, issue, 0, unroll=DMA_UNROLL)

    def scatter_rows(tile, sl):
        base = ts_ref[tile]
        nvalid = nv_ref[tile]

        def issue(r, c):
            row = jnp.where(r < nvalid, order_ref[base + r], n_tok + r)
            pltpu.make_async_copy(ybuf.at[sl, pl.ds(r, 1), :],
                                  y_hbm.at[pl.ds(row, 1), :], ssem.at[sl]).start()
            return c
        lax.fori_loop(0, TME, issue, 0, unroll=DMA_UNROLL)

    def wait_gather(sl):
        pltpu.make_async_copy(h2_hbm.at[pl.ds(0, TME), :], xbuf.at[sl], gsem.at[sl]).wait()

    def wait_scatter(sl):
        pltpu.make_async_copy(ybuf.at[sl], y_hbm.at[pl.ds(0, TME), :], ssem.at[sl]).wait()

    @pl.when(jnp.logical_and(used, jnp.logical_and(t == 0, j == 0)))
    def _():
        gather_rows(0, 0)
        ybuf[1] = jnp.zeros((TME, D_MODEL), f32)
        spare = pltpu.make_async_copy(ybuf.at[1], y_hbm.at[pl.ds(n_tok, TME), :], ssem.at[1])
        spare.start()
        spare.wait()

    @pl.when(jnp.logical_and(used, j == 0))
    def _():
        wait_gather(slot)

    @pl.when(jnp.logical_and(t + 1 < n_used, j == 1))
    def _():
        gather_rows(t + 1, 1 - slot)

    @pl.when(used)
    def _():
        xb = xbuf[slot, :, 0:D_MODEL].astype(bf16)
        hg = jnp.dot(xb, wg_ref[...], preferred_element_type=f32)
        hu = jnp.dot(xb, wu_ref[...], preferred_element_type=f32)
        hdn = (_silu(hg) * hu).astype(bf16)
        yo = jnp.dot(hdn, wd_ref[...], preferred_element_type=f32)
        lane = lax.broadcasted_iota(jnp.int32, (TME, LANES), 1)
        w_lane = jnp.where(j == slot, RT_WLO, RT_WHI)
        wsel = jnp.sum(jnp.where(lane == w_lane, xbuf[slot, :, D_MODEL:H2X_W], 0.0), axis=-1, keepdims=True)

        @pl.when(j == 0)
        def _():
            ybuf[slot] = wsel * yo

        @pl.when(j == 1)
        def _():
            ybuf[slot] = ybuf[slot] + wsel * yo

    @pl.when(jnp.logical_and(used, j == 1))
    def _():
        @pl.when(t > 0)
        def _():
            wait_scatter(1 - slot)

        scatter_rows(t, slot)

        @pl.when(t == n_used - 1)
        def _():
            wait_scatter(slot)


def moe_dispatch(lay, route):
    m = lay.m
    n_tiles = m // TME + N_BUCKETS
    bucket = route[:, RT_BUCKET].astype(jnp.int32)
    order = jnp.argsort(bucket).astype(jnp.int32)
    counts = jnp.sum(bucket[:, None] == jnp.arange(N_BUCKETS, dtype=jnp.int32)[None, :], axis=0, dtype=jnp.int32)
    ptiles = (counts + TME - 1) // TME
    pend = jnp.cumsum(ptiles)
    pstart = pend - ptiles
    start = jnp.cumsum(counts) - counts
    n_used = pend[-1]
    tid = jnp.arange(n_tiles, dtype=jnp.int32)
    tid_c = jnp.minimum(tid, n_used - 1)
    tb = jnp.minimum(jnp.sum(pend[None, :] <= tid_c[:, None], axis=1, dtype=jnp.int32), N_BUCKETS - 1)
    sel = tb[:, None] == jnp.arange(N_BUCKETS, dtype=jnp.int32)[None, :]
    pick = lambda v: jnp.sum(jnp.where(sel, v[None, :], 0), axis=1, dtype=jnp.int32)
    off = (tid_c - pick(pstart)) * TME
    tstart = pick(start) + off
    nvalid = jnp.where(tid < n_used, jnp.clip(pick(counts) - off, 0, TME), 0).astype(jnp.int32)
    order_p = jnp.concatenate([order, jnp.zeros((TME,), jnp.int32)])
    grp, pr = tb // N_PAIRS, tb % N_PAIRS
    e_lo = grp * E_PER_GROUP + jnp.asarray(PAIR_LO, jnp.int32)[pr]
    e_hi = grp * E_PER_GROUP + jnp.asarray(PAIR_HI, jnp.int32)[pr]
    swap = (tid % 2) == 1
    te = jnp.stack([jnp.where(swap, e_hi, e_lo), jnp.where(swap, e_lo, e_hi)], -1).reshape(-1)
    te = jnp.concatenate([te, te[-2:]])
    return (te.astype(jnp.int32), n_used.reshape(1).astype(jnp.int32), tstart.astype(jnp.int32), nvalid,
            order_p, n_tiles)


def moe_apply(lay, h2, route, wg_b, wu_b, wd_b, layer):
    te, n_used, tstart, nvalid, order_p, n_tiles = moe_dispatch(lay, route)
    wmap = lambda t, j, te, *_: (layer, te[2 * t + j], 0, 0)
    return pl.pallas_call(
        _moe_kernel,
        grid_spec=pltpu.PrefetchScalarGridSpec(
            num_scalar_prefetch=5, grid=(n_tiles, 2),
            in_specs=[
                pl.BlockSpec(memory_space=pl.ANY),
                pl.BlockSpec((None, None, D_MODEL, EXPERT_FF), wmap),
                pl.BlockSpec((None, None, D_MODEL, EXPERT_FF), wmap),
                pl.BlockSpec((None, None, EXPERT_FF, D_MODEL), wmap),
            ],
            out_specs=pl.BlockSpec(memory_space=pl.ANY),
            scratch_shapes=[
                pltpu.VMEM((2, TME, H2X_W), f32),
                pltpu.VMEM((2, TME, D_MODEL), f32),
                pltpu.SemaphoreType.DMA((2,)),
                pltpu.SemaphoreType.DMA((2,)),
            ]),
        out_shape=jax.ShapeDtypeStruct((lay.m + TME, D_MODEL), f32),
        compiler_params=pltpu.CompilerParams(dimension_semantics=("arbitrary", "arbitrary"),
                                             vmem_limit_bytes=VMEM_LIMIT),
        name="moe",
    )(te, n_used, tstart, nvalid, order_p, h2, wg_b, wu_b, wd_b)


def _final_kernel(mod_ref, x_ref, y_ref, mods_ref, g_ref, oc_ref, ol_ref, *, n_first):
    del mod_ref
    i = pl.program_id(0)
    x = x_ref[...] + mods_ref[MOD_G2:MOD_G2 + 1, :] * y_ref[...]
    out = _rms(x) * g_ref[...]

    @pl.when(i < n_first)
    def _():
        oc_ref[...] = out

    @pl.when(i >= n_first)
    def _():
        ol_ref[...] = out


def final_norm_apply(lay, x, ytok, mods, g):
    row_map = lambda i, mod: (i, 0)
    nbc = lay.m_ctx // TB
    return pl.pallas_call(
        functools.partial(_final_kernel, n_first=nbc),
        grid_spec=pltpu.PrefetchScalarGridSpec(
            num_scalar_prefetch=1, grid=(lay.n_blk,),
            in_specs=[
                pl.BlockSpec((TB, D_MODEL), row_map),
                pl.BlockSpec((TB, D_MODEL), row_map),
                pl.BlockSpec((None, 6, D_MODEL), lambda i, mod: (mod[i], 0, 0)),
                pl.BlockSpec((1, D_MODEL), lambda i, mod: (0, 0)),
            ],
            out_specs=[pl.BlockSpec((TB, D_MODEL), lambda i, mod: (jnp.minimum(i, nbc - 1), 0)),
                       pl.BlockSpec((TB, D_MODEL), lambda i, mod: (jnp.maximum(i - nbc, 0), 0))]),
        out_shape=[jax.ShapeDtypeStruct((lay.m_ctx, D_MODEL), f32),
                   jax.ShapeDtypeStruct((lay.m_lat, D_MODEL), f32)],
        compiler_params=pltpu.CompilerParams(dimension_semantics=("arbitrary",)),
        name="final_norm",
    )(jnp.asarray(lay.mod), x, ytok, mods, g)


def _reorder_w_in(w):
    g_qkv, g_z, g_a, g_b, s_z, s_xbc, s_dt, m_q, m_kv = jnp.split(w, np.cumsum(IN_SIZES)[:-1], axis=1)
    ckv, kpe = m_kv[:, :KV_RANK], m_kv[:, KV_RANK:]
    z = lambda n: jnp.zeros((w.shape[0], n), w.dtype)
    small = jnp.concatenate([g_a, g_b, s_dt, z(SM_KPE - SM_DT - 2 * SSM_HEADS), kpe, z(LANES - SM_KPE - MLA_ROPE)], 1)
    return jnp.concatenate([g_qkv, g_z, s_z, m_q, ckv, s_xbc, small], 1).astype(bf16)


def _reorder_w_uq(w):
    r = w.reshape(Q_RANK, MLA_HEADS, MLA_QK)
    return jnp.pad(r, ((0, 0), (0, 0), (0, LANES - MLA_QK))).reshape(Q_RANK, Q_HEADS_W).astype(bf16)


def _reorder_w_ukv(w):
    r = w.reshape(KV_RANK, MLA_HEADS, MLA_NOPE + MLA_VDIM)
    k = jnp.pad(r[:, :, :MLA_NOPE], ((0, 0), (0, 0), (0, LANES - MLA_NOPE))).reshape(KV_RANK, MLA_HEADS * LANES)
    v = jnp.pad(r[:, :, MLA_NOPE:], ((0, 0), (0, 0), (0, LANES - MLA_VDIM))).reshape(KV_RANK, MLA_HEADS * LANES)
    return jnp.concatenate([k, v], 1).astype(bf16)


def _lane_row(vals, lane0):
    n = vals.shape[-1]
    return jnp.pad(vals, ((0, 0), (lane0, LANES - lane0 - n)))


def _par_rows(row0, row1):
    z = jnp.zeros_like(row0)
    return jnp.stack([row0, row1] + [z] * (SUBLANES - 2), axis=1)


def kernel(x_prompt, x_sample, cache_ckv, cache_kpe, state_gdn, state_ssm, c, c_ctx, w_ada, b_ada, norm1, norm2, w_in, gdn_conv, gdn_A_log, gdn_dt_bias, gdn_norm, ssm_conv, ssm_conv_bias, ssm_A_log, ssm_dt_bias, ssm_D, ssm_norm, mla_q_norm, mla_w_uq, mla_kv_norm, mla_w_ukv, w_out, router_group, router_group_bias, router_expert, router_expert_bias, moe_w_gate, moe_w_up, moe_w_down, final_norm):
    n_ctx, t_ctx, _ = x_prompt.shape
    n_lat, t_lat, _ = x_sample.shape
    past = cache_ckv.shape[2]
    depth = w_in.shape[0]
    assert n_lat < SUBLANES and t_ctx % TB == 0 and t_lat % TB == 0 and past % TB == 0
    lay = Layout(n_ctx, t_ctx, n_lat, t_lat)
    m_ctx = lay.m_ctx

    x = (x_prompt.reshape(m_ctx, D_MODEL), x_sample.reshape(lay.m_lat, D_MODEL))
    cvec = jnp.concatenate([c_ctx[None], c, jnp.zeros((SUBLANES - 1 - n_lat, D_MODEL), f32)], 0)
    tables = rope_tables(lay)

    b_ada3 = b_ada[:, None, :]
    norm1_3, norm2_3 = norm1[:, None, :], norm2[:, None, :]
    gdn_par = _par_rows(_lane_row(gdn_A_log.reshape(depth, -1), SM_GA), _lane_row(gdn_dt_bias.reshape(depth, -1), SM_GA))
    ssm_par = _par_rows(_lane_row(ssm_A_log.reshape(depth, -1), SM_DT), _lane_row(ssm_dt_bias.reshape(depth, -1), SM_DT))
    d_lanes = jnp.pad(jnp.repeat(ssm_D, SSM_HEADDIM, axis=-1), ((0, 0), (0, SSM_XBC - SSM_INNER)))
    ssm_cpar = _par_rows(ssm_conv_bias, d_lanes)
    w_rt = jnp.pad(jnp.concatenate([router_group, router_expert], -1),
                   ((0, 0), (0, 0), (0, LANES - N_EGROUPS - N_EXPERTS)))
    w_rt_hi = w_rt.astype(bf16)
    w_rt = jnp.concatenate([w_rt_hi, (w_rt - w_rt_hi.astype(f32)).astype(bf16)], -1)
    b_rt = jnp.pad(jnp.concatenate([router_group_bias, router_expert_bias], -1),
                   ((0, 0), (0, LANES - N_EGROUPS - N_EXPERTS)))[:, None, :]
    kpe_cache = jnp.pad(cache_kpe, ((0, 0), (0, 0), (0, 0), (SM_KPE, LANES - SM_KPE - MLA_ROPE)))

    w_out_b = w_out.astype(bf16)
    wg_b, wu_b, wd_b = moe_w_gate.astype(bf16), moe_w_up.astype(bf16), moe_w_down.astype(bf16)

    ckvs, kpes, gdns, ssms = [], [], [], []
    ytok = mods_prev = None
    for l in range(depth):
        mods = ada_mod(cvec, w_ada, b_ada3, l)
        res = None if l == 0 else (ytok, mods_prev)
        proj, x = in_proj(lay, x, mods, norm1_3, _reorder_w_in(w_in[l]), l, res)

        zg = jnp.zeros((n_ctx, GDN_HEADS, GDN_DK, GDN_DV), f32)
        o_f, sg_f = gdn_scan(lay, proj, gdn_conv, gdn_par, jnp.concatenate([zg, state_gdn[:, l, 0]], 0), l, False)
        o_b, sg_b = gdn_scan(lay, proj, gdn_conv, gdn_par, jnp.concatenate([zg, state_gdn[:, l, 1]], 0), l, True)

        zs = jnp.zeros((n_ctx, SSM_INNER, LANES), f32)
        y_f, hs_f = ssd_scan(lay, proj, ssm_conv, ssm_cpar, ssm_par,
                             jnp.concatenate([zs, ssm_state_to_lanes(state_ssm[:, l, 0])], 0), l, False)
        y_b, hs_b = ssd_scan(lay, proj, ssm_conv, ssm_cpar, ssm_par,
                             jnp.concatenate([zs, ssm_state_to_lanes(state_ssm[:, l, 1])], 0), l, True)

        q, ckvn, kpe = mla_prep(lay, proj, mla_q_norm[:, None, :], mla_kv_norm[:, None, :],
                                _reorder_w_uq(mla_w_uq[l]), tables, l)
        w_ukv_r = _reorder_w_ukv(mla_w_ukv[l])
        k_ctx, v_ctx = kv_up(ckvn[:m_ctx], kpe[:m_ctx], w_ukv_r)
        ckv_lat = jnp.concatenate([cache_ckv[:, l], ckvn[m_ctx:].reshape(n_lat, t_lat, KV_RANK)], 1)
        kpe_lat = jnp.concatenate([kpe_cache[:, l], kpe[m_ctx:].reshape(n_lat, t_lat, LANES)], 1)
        k_lat, v_lat = kv_up(ckv_lat.reshape(-1, KV_RANK), kpe_lat.reshape(-1, LANES), w_ukv_r)
        o_mla = jnp.concatenate([
            attention(q, k_ctx, v_ctx, 0, n_ctx, t_ctx, t_ctx),
            attention(q, k_lat, v_lat, m_ctx // TB, n_lat, t_lat, past + t_lat)], 0)

        x, h2, route = out_proj(lay, o_f, o_b, proj, y_f, y_b, o_mla, x, mods, gdn_norm[:, None, :],
                                ssm_norm[:, None, :], norm2_3, w_out_b, w_rt, b_rt, l)
        ytok = moe_apply(lay, h2, route, wg_b, wu_b, wd_b, l)
        mods_prev = mods

        ckvs.append(ckvn[:m_ctx].reshape(n_ctx, t_ctx, KV_RANK))
        kpes.append(kpe[:m_ctx, SM_KPE:SM_KPE + MLA_ROPE].reshape(n_ctx, t_ctx, MLA_ROPE))
        gdns.append(jnp.stack([sg_f[:n_ctx], sg_b[:n_ctx]], 1))
        ssms.append(jnp.stack([ssm_state_from_lanes(hs_f[:n_ctx]), ssm_state_from_lanes(hs_b[:n_ctx])], 1))

    y_ctx, y_lat = final_norm_apply(lay, x, ytok, mods_prev, final_norm[None, :])
    return (y_ctx.reshape(n_ctx, t_ctx, D_MODEL), y_lat.reshape(n_lat, t_lat, D_MODEL),
            jnp.stack(ckvs, 1), jnp.stack(kpes, 1), jnp.stack(gdns, 1), jnp.stack(ssms, 1))
```

```python
import functools
import math

import numpy as np
import jax
import jax.numpy as jnp
from jax import lax
from jax.experimental import pallas as pl
from jax.experimental.pallas import tpu as pltpu

f32 = jnp.float32
bf16 = jnp.bfloat16

D_MODEL = 2048
GRID_W = 64
EPS = 1e-6
CONV_W = 5
CHUNK = 64

GDN_HEADS = 4
GDN_DK = 128
GDN_DV = 128
GDN_W = GDN_HEADS * GDN_DV
GDN_QKV = 2 * GDN_HEADS * GDN_DK + GDN_HEADS * GDN_DV

SSM_HEADS = 16
SSM_HEADDIM = 64
SSM_GROUPS = 2
SSM_DSTATE = 64
SSM_INNER = SSM_HEADS * SSM_HEADDIM
SSM_XBC = SSM_INNER + 2 * SSM_GROUPS * SSM_DSTATE

MLA_HEADS = 8
MLA_NOPE = 64
MLA_ROPE = 32
MLA_VDIM = 64
Q_RANK = 512
KV_RANK = 256
MLA_W = MLA_HEADS * MLA_VDIM
ROPE_THETA = 10000.0

MIX_W = GDN_W + SSM_INNER + MLA_W
IN_SIZES = (GDN_QKV, GDN_W, 2 * GDN_HEADS, 2 * GDN_HEADS, SSM_INNER, SSM_XBC, 2 * SSM_HEADS, Q_RANK, KV_RANK + MLA_ROPE)

N_EGROUPS = 4
E_PER_GROUP = 4
N_EXPERTS = N_EGROUPS * E_PER_GROUP
EXPERT_FF = 512

LANES = 128
SUBLANES = 8

TB = 256

COL_QKV = 0
COL_GZ = 1536
COL_SZ = 2048
COL_MQ = 3072
COL_CKV = 3584
COL_XBC = 3840
COL_SMALL = 5120
PROJ_W = 5248
SM_GA, SM_GB, SM_DT, SM_KPE = 0, 8, 16, 64


class Layout:
    def __init__(self, n_ctx, t_ctx, n_lat, t_lat):
        self.n_ctx, self.t_ctx, self.n_lat, self.t_lat = n_ctx, t_ctx, n_lat, t_lat
        self.m_ctx = n_ctx * t_ctx
        self.m_lat = n_lat * t_lat
        self.m = self.m_ctx + self.m_lat
        self.n_seq = n_ctx + n_lat
        bc, bl = t_ctx // TB, t_lat // TB
        seq, first, last, mod = [], [], [], []
        for s in range(n_ctx):
            for j in range(bc):
                seq.append(s); first.append(int(j == 0)); last.append(int(j == bc - 1)); mod.append(0)
        for s in range(n_lat):
            for j in range(bl):
                seq.append(n_ctx + s); first.append(int(j == 0)); last.append(int(j == bl - 1)); mod.append(1 + s)
        self.n_blk = len(seq)
        self.seq = np.array(seq, np.int32)
        self.first = np.array(first, np.int32)
        self.last = np.array(last, np.int32)
        self.mod = np.array(mod + mod[-1:], np.int32)
        fwd = np.arange(self.n_blk, dtype=np.int32)
        bwd = []
        i = 0
        while i < self.n_blk:
            j = i
            while self.last[j] == 0:
                j += 1
            bwd.extend(range(j, i - 1, -1))
            i = j + 1
        self.order = {False: fwd, True: np.array(bwd, np.int32)}

    def scan_tables(self, rev):
        order = self.order[rev]
        pad = lambda a: jnp.asarray(np.concatenate([a, a[-1:]]))
        return pad(order), pad(self.seq[order]), pad(self.first[order]), pad(self.last[order])


def _mm(a, b):
    return jnp.dot(a.astype(bf16), b.astype(bf16), preferred_element_type=f32)


def _mm_nt(a, b):
    return lax.dot_general(a.astype(bf16), b.astype(bf16), (((1,), (1,)), ((), ())), preferred_element_type=f32)


def _split3(x):
    hi = x.astype(bf16)
    r = x - hi.astype(f32)
    mid = r.astype(bf16)
    lo = (r - mid.astype(f32)).astype(bf16)
    return hi, mid, lo


def _mm01(m01, x):
    hi, mid, lo = _split3(x)
    d = lambda t: jnp.dot(m01, t, preferred_element_type=f32)
    return d(hi) + d(mid) + d(lo)


def _transpose_exact(eye, x):
    hi, mid, lo = _split3(x)
    d = lambda t: lax.dot_general(eye, t, (((1,), (1,)), ((), ())), preferred_element_type=f32)
    return d(hi) + d(mid) + d(lo)


def _mm3(a, b):
    ah = a.astype(bf16)
    al = (a - ah.astype(f32)).astype(bf16)
    bh = b.astype(bf16)
    bl = (b - bh.astype(f32)).astype(bf16)
    d = lambda x, y: jnp.dot(x, y, preferred_element_type=f32)
    return d(ah, bh) + d(ah, bl) + d(al, bh)


def _sigmoid(x):
    return 1.0 / (1.0 + jnp.exp(-x))


def _silu(x):
    return x * _sigmoid(x)


def _softplus(x):
    return jnp.maximum(x, 0.0) + jnp.log(1.0 + jnp.exp(-jnp.abs(x)))


def _chunk_masks(rev):
    i = lax.broadcasted_iota(jnp.int32, (CHUNK, CHUNK), 0)
    j = lax.broadcasted_iota(jnp.int32, (CHUNK, CHUNK), 1)
    if rev:
        incl, strict = j >= i, j > i
    else:
        incl, strict = j <= i, j < i
    ll = jnp.where(incl, 1.0, 0.0).astype(bf16)
    lls = jnp.where(incl, 0.0, 1.0).astype(bf16)
    return incl, strict, ll, lls


def _fill_ext(ext_ref, main_ref, prev_ref, next_ref, sfirst, slast):
    ext_ref[0:SUBLANES, :] = jnp.where(sfirst == 1, 0.0, prev_ref[...])
    ext_ref[SUBLANES:SUBLANES + TB, :] = main_ref[...]
    ext_ref[SUBLANES + TB:2 * SUBLANES + TB, :] = jnp.where(slast == 1, 0.0, next_ref[...])


def _conv_tile(ext_ref, w_ref, r0, col0, width, bias=None):
    acc = None
    for t in range(CONV_W):
        rows = ext_ref[pl.ds(SUBLANES + r0 - CONV_W // 2 + t, CHUNK), col0:col0 + width]
        term = rows * w_ref[t:t + 1, col0:col0 + width]
        acc = term if acc is None else acc + term
    if bias is not None:
        acc = acc + bias
    return _silu(acc)


INV_BLOCK = 16


def _block_masks():
    i = lax.broadcasted_iota(jnp.int32, (CHUNK, CHUNK), 0)
    j = lax.broadcasted_iota(jnp.int32, (CHUNK, CHUNK), 1)
    blk16 = (i // INV_BLOCK) == (j // INV_BLOCK)
    blk32 = (i // (2 * INV_BLOCK)) == (j // (2 * INV_BLOCK))
    return blk16, blk32


def _unit_tri_inverse_many(lmats, eye, blk16, blk32):
    ps = list(lmats)
    ld = {p: jnp.where(blk16, lmats[p], 0.0) for p in ps}
    t = {p: eye - ld[p] for p in ps}
    pw = {p: _mm3(ld[p], ld[p]) for p in ps}
    for it in range(3):
        t = {p: t[p] + _mm3(t[p], pw[p]) for p in ps}
        if it < 2:
            pw = {p: _mm3(pw[p], pw[p]) for p in ps}
    in32 = jnp.logical_and(blk32, jnp.logical_not(blk16))
    m = {p: _mm3(t[p], jnp.where(in32, lmats[p], 0.0)) for p in ps}
    t = {p: t[p] - _mm3(m[p], t[p]) for p in ps}
    m = {p: _mm3(t[p], jnp.where(blk32, 0.0, lmats[p])) for p in ps}
    return {p: t[p] - _mm3(m[p], t[p]) for p in ps}


def _gdn_kernel(order_ref, seq_ref, first_ref, last_ref,
                main_ref, prev_ref, next_ref, small_ref, convw_ref, par_ref, s0_ref,
                o_ref, sfin_ref, s_ref, ext_ref, *, rev):
    del order_ref, seq_ref
    step = pl.program_id(0)
    sfirst = first_ref[step]
    slast = last_ref[step]
    pfirst, plast = (slast, sfirst) if rev else (sfirst, slast)
    d = 1 if rev else 0

    @pl.when(pfirst == 1)
    def _():
        s_ref[...] = s0_ref[...]

    _fill_ext(ext_ref, main_ref, prev_ref, next_ref, sfirst, slast)

    incl, strict, ll, lls = _chunk_masks(rev)
    eye = jnp.where(lax.broadcasted_iota(jnp.int32, (CHUNK, CHUNK), 0)
                    == lax.broadcasted_iota(jnp.int32, (CHUNK, CHUNK), 1), 1.0, 0.0)
    blk16, blk32 = _block_masks()
    neg_a = -jnp.exp(par_ref[0:1, :])
    dt_bias = par_ref[1:2, :]
    n_chunks = TB // CHUNK

    chunk_order = [n_chunks - 1 - cc if rev else cc for cc in range(n_chunks)]
    heads = range(GDN_HEADS)
    eye_k = jnp.where(lax.broadcasted_iota(jnp.int32, (GDN_DK, GDN_DK), 0)
                      == lax.broadcasted_iota(jnp.int32, (GDN_DK, GDN_DK), 1), 1.0, 0.0).astype(bf16)

    gates = {}
    for c in chunk_order:
        sm = small_ref[c * CHUNK:(c + 1) * CHUNK, :]
        la_blk = neg_a * _softplus(sm + dt_bias)
        gates[c] = (la_blk, _sigmoid(sm))
    g_blk = {c: _mm01(ll, gates[c][0]) for c in chunk_order}
    gr_blk = {c: _mm01(lls, gates[c][0]) for c in chunk_order}
    gl_blk = {c: jnp.sum(gates[c][0], axis=0, keepdims=True) for c in chunk_order}

    probs = [(c, h) for c in chunk_order for h in heads]
    ia = lambda h: SM_GA + d * GDN_HEADS + h
    ib = lambda h: SM_GB + d * GDN_HEADS + h
    col = lambda blk, lane: blk[:, lane:lane + 1]
    beta = {p: col(gates[p[0]][1], ib(p[1])) for p in probs}
    q, k, v = {}, {}, {}
    for p in probs:
        c, h = p
        r0 = c * CHUNK
        qq = _conv_tile(ext_ref, convw_ref, r0, h * GDN_DK, GDN_DK)
        kx = _conv_tile(ext_ref, convw_ref, r0, GDN_HEADS * GDN_DK + h * GDN_DK, GDN_DK)
        v[p] = _conv_tile(ext_ref, convw_ref, r0, 2 * GDN_HEADS * GDN_DK + h * GDN_DV, GDN_DV)
        q[p] = qq * lax.rsqrt(jnp.sum(qq * qq, axis=-1, keepdims=True) + EPS) * (GDN_DK ** -0.5)
        k[p] = kx * lax.rsqrt(jnp.sum(kx * kx, axis=-1, keepdims=True) + EPS)
    g_t = {c: _transpose_exact(eye_k, g_blk[c]) for c in chunk_order}
    dmat = {p: jnp.where(incl, col(g_blk[p[0]], ia(p[1])) - g_t[p[0]][ia(p[1]):ia(p[1]) + 1, :], 0.0)
            for p in probs}
    kk = {p: _mm_nt(k[p], k[p]) for p in probs}
    qk = {p: _mm_nt(q[p], k[p]) for p in probs}
    decay = {p: jnp.where(incl, jnp.exp(dmat[p]), 0.0) for p in probs}
    lmat = {p: jnp.where(strict, kk[p] * beta[p] * decay[p], 0.0) for p in probs}
    qk = {p: jnp.where(incl, qk[p] * decay[p], 0.0) for p in probs}
    tinv = _unit_tri_inverse_many(lmat, eye, blk16, blk32)
    uw = {}
    for p in probs:
        e_g = jnp.exp(col(g_blk[p[0]], ia(p[1])))
        rhs = jnp.concatenate([v[p] * beta[p], k[p] * beta[p] * e_g], axis=1)
        uw[p] = _mm3(tinv[p], rhs)
        q[p] = q[p] * e_g
    kdt = {}
    for p in probs:
        k_dec = (k[p] * jnp.exp(col(gr_blk[p[0]], ia(p[1])))).astype(bf16)
        kdt[p] = lax.dot_general(eye_k, k_dec, (((1,), (1,)), ((), ())), preferred_element_type=f32)

    for c in chunk_order:
        ps = [(c, h) for h in heads]
        s_old = {p: s_ref[p[1]] for p in ps}
        ws = {p: _mm(uw[p][:, GDN_DV:], s_old[p]) for p in ps}
        qs = {p: _mm(q[p], s_old[p]) for p in ps}
        v_new = {p: uw[p][:, :GDN_DV] - ws[p] for p in ps}
        o = {p: qs[p] + _mm(qk[p], v_new[p]) for p in ps}
        for p in ps:
            h = p[1]
            s_ref[h] = s_old[p] * jnp.exp(col(gl_blk[c], ia(h))) + _mm(kdt[p], v_new[p])
            o_ref[c * CHUNK:(c + 1) * CHUNK, h * GDN_DV:(h + 1) * GDN_DV] = o[p]

    @pl.when(plast == 1)
    def _():
        sfin_ref[...] = s_ref[...]


def _halo_maps(n_rows8):
    prev_map = lambda i, order, *_: (jnp.maximum(order[i] * (TB // SUBLANES) - 1, 0), 0)
    next_map = lambda i, order, *_: (jnp.minimum((order[i] + 1) * (TB // SUBLANES), n_rows8 - 1), 0)
    return prev_map, next_map


def gdn_scan(lay, proj, convw, par, s0, layer, rev):
    prev_map, next_map = _halo_maps(lay.m // SUBLANES)
    blk_map = lambda i, order, *_: (order[i], 0)
    seq_map = lambda i, order, seq, *_: (seq[i], 0, 0, 0)
    grid_spec = pltpu.PrefetchScalarGridSpec(
        num_scalar_prefetch=4,
        grid=(lay.n_blk,),
        in_specs=[
            pl.BlockSpec((TB, GDN_QKV), blk_map),
            pl.BlockSpec((SUBLANES, GDN_QKV), prev_map),
            pl.BlockSpec((SUBLANES, GDN_QKV), next_map),
            pl.BlockSpec((TB, LANES), lambda i, order, *_: (order[i], COL_SMALL // LANES)),
            pl.BlockSpec((None, CONV_W, GDN_QKV), lambda i, *_: (layer, 0, 0)),
            pl.BlockSpec((None, SUBLANES, LANES), lambda i, *_: (layer, 0, 0)),
            pl.BlockSpec((None, GDN_HEADS, GDN_DK, GDN_DV), seq_map),
        ],
        out_specs=[
            pl.BlockSpec((TB, GDN_W), blk_map),
            pl.BlockSpec((None, GDN_HEADS, GDN_DK, GDN_DV), seq_map),
        ],
        scratch_shapes=[
            pltpu.VMEM((GDN_HEADS, GDN_DK, GDN_DV), f32),
            pltpu.VMEM((TB + 2 * SUBLANES, GDN_QKV), f32),
        ],
    )
    return pl.pallas_call(
        functools.partial(_gdn_kernel, rev=rev),
        grid_spec=grid_spec,
        out_shape=[
            jax.ShapeDtypeStruct((lay.m, GDN_W), f32),
            jax.ShapeDtypeStruct((lay.n_seq, GDN_HEADS, GDN_DK, GDN_DV), f32),
        ],
        compiler_params=pltpu.CompilerParams(dimension_semantics=("arbitrary",)),
        name="gdn_bwd" if rev else "gdn_fwd",
    )(*lay.scan_tables(rev), proj, proj, proj, proj, convw, par, s0)


SSM_PAIRS = SSM_HEADS // 2
PAIRS_PER_GROUP = SSM_PAIRS // SSM_GROUPS
COL_B = SSM_INNER
COL_C = SSM_INNER + SSM_GROUPS * SSM_DSTATE


def _ssd_kernel(order_ref, seq_ref, first_ref, last_ref,
                main_ref, prev_ref, next_ref, small_ref, convw_ref, cpar_ref, par_ref, h0_ref,
                y_ref, hfin_ref, h_ref, ext_ref, *, rev):
    del order_ref, seq_ref
    step = pl.program_id(0)
    sfirst = first_ref[step]
    slast = last_ref[step]
    pfirst, plast = (slast, sfirst) if rev else (sfirst, slast)
    d = 1 if rev else 0

    @pl.when(pfirst == 1)
    def _():
        h_ref[...] = h0_ref[...]

    _fill_ext(ext_ref, main_ref, prev_ref, next_ref, sfirst, slast)

    incl, _, ll, lls = _chunk_masks(rev)
    lane_lo = lax.broadcasted_iota(jnp.int32, (CHUNK, LANES), 1) < SSM_DSTATE
    row_lo = lax.broadcasted_iota(jnp.int32, (LANES, LANES), 0) < SSM_HEADDIM
    eye_p = jnp.where(lax.broadcasted_iota(jnp.int32, (LANES, LANES), 0)
                      == lax.broadcasted_iota(jnp.int32, (LANES, LANES), 1), 1.0, 0.0).astype(bf16)
    neg_a = -jnp.exp(par_ref[0:1, :])
    dt_bias = par_ref[1:2, :]
    n_chunks = TB // CHUNK

    for cc in range(n_chunks):
        c = n_chunks - 1 - cc if rev else cc
        r0 = c * CHUNK
        sm = small_ref[r0:r0 + CHUNK, :]
        dt_blk = _softplus(sm + dt_bias)
        dta_blk = dt_blk * neg_a
        acum_blk = _mm01(ll, dta_blk)
        ar_blk = _mm01(lls, dta_blk)
        al_blk = jnp.sum(dta_blk, axis=0, keepdims=True)
        b_pair = _conv_tile(ext_ref, convw_ref, r0, COL_B, LANES, cpar_ref[0:1, COL_B:COL_B + LANES])
        c_pair = _conv_tile(ext_ref, convw_ref, r0, COL_C, LANES, cpar_ref[0:1, COL_C:COL_C + LANES])
        c_g, b_g, cb = [], [], []
        for g in range(SSM_GROUPS):
            gmask = lane_lo if g == 0 else jnp.logical_not(lane_lo)
            c_g.append(jnp.where(gmask, c_pair, 0.0))
            b_g.append(jnp.where(gmask, b_pair, 0.0))
        cb = [_mm_nt(c_g[g], b_pair) for g in range(SSM_GROUPS)]
        pairs = range(SSM_PAIRS)
        grp = lambda p: p // PAIRS_PER_GROUP
        lane_of = lambda hh: SM_DT + d * SSM_HEADS + hh
        colv = lambda blk, hh: blk[:, lane_of(hh):lane_of(hh) + 1]
        both = lambda blk, p: jnp.where(lane_lo, colv(blk, 2 * p), colv(blk, 2 * p + 1))
        xs = [_conv_tile(ext_ref, convw_ref, r0, p * LANES, LANES, cpar_ref[0:1, p * LANES:(p + 1) * LANES])
              for p in pairs]
        acum_t = _transpose_exact(eye_p, acum_blk)
        dmat = [jnp.where(incl, colv(acum_blk, hh) - acum_t[lane_of(hh):lane_of(hh) + 1, :], 0.0)
                for hh in range(SSM_HEADS)]
        m_h = [cb[grp(hh // 2)] * jnp.where(incl, jnp.exp(dmat[hh]), 0.0) for hh in range(SSM_HEADS)]
        xdt = [xs[p] * both(dt_blk, p) for p in pairs]
        y = [jnp.where(lane_lo, _mm(m_h[2 * p], xdt[p]), _mm(m_h[2 * p + 1], xdt[p])) for p in pairs]
        xdt_t = [lax.dot_general(eye_p, xdt[p].astype(bf16), (((1,), (1,)), ((), ())),
                                 preferred_element_type=f32).astype(bf16) for p in pairs]
        st_new = [jnp.where(row_lo,
                            jnp.dot(xdt_t[p], (b_g[grp(p)] * jnp.exp(colv(ar_blk, 2 * p))).astype(bf16),
                                    preferred_element_type=f32),
                            jnp.dot(xdt_t[p], (b_g[grp(p)] * jnp.exp(colv(ar_blk, 2 * p + 1))).astype(bf16),
                                    preferred_element_type=f32)) for p in pairs]
        h_pair = [h_ref[p * LANES:(p + 1) * LANES, :] for p in pairs]
        y_off = [_mm_nt(c_g[grp(p)], h_pair[p]) for p in pairs]
        for p in pairs:
            col = p * LANES
            yp = y[p] + y_off[p] * jnp.exp(both(acum_blk, p))
            if not rev:
                yp = yp + xs[p] * cpar_ref[1:2, col:col + LANES]
            dec = jnp.where(row_lo, jnp.exp(colv(al_blk, 2 * p)), jnp.exp(colv(al_blk, 2 * p + 1)))
            h_ref[col:col + LANES, :] = h_pair[p] * dec + st_new[p]
            y_ref[r0:r0 + CHUNK, col:col + LANES] = yp

    @pl.when(plast == 1)
    def _():
        hfin_ref[...] = h_ref[...]


def ssd_scan(lay, proj, convw, cpar, par, h0, layer, rev):
    prev_map, next_map = _halo_maps(lay.m // SUBLANES)
    cblk = COL_XBC // SSM_XBC
    blk_map = lambda i, order, *_: (order[i], 0)
    seq_map = lambda i, order, seq, *_: (seq[i], 0, 0)
    grid_spec = pltpu.PrefetchScalarGridSpec(
        num_scalar_prefetch=4,
        grid=(lay.n_blk,),
        in_specs=[
            pl.BlockSpec((TB, SSM_XBC), lambda i, order, *_: (order[i], cblk)),
            pl.BlockSpec((SUBLANES, SSM_XBC), lambda i, *a: (prev_map(i, *a)[0], cblk)),
            pl.BlockSpec((SUBLANES, SSM_XBC), lambda i, *a: (next_map(i, *a)[0], cblk)),
            pl.BlockSpec((TB, LANES), lambda i, order, *_: (order[i], COL_SMALL // LANES)),
            pl.BlockSpec((None, CONV_W, SSM_XBC), lambda i, *_: (layer, 0, 0)),
            pl.BlockSpec((None, SUBLANES, SSM_XBC), lambda i, *_: (layer, 0, 0)),
            pl.BlockSpec((None, SUBLANES, LANES), lambda i, *_: (layer, 0, 0)),
            pl.BlockSpec((None, SSM_INNER, LANES), seq_map),
        ],
        out_specs=[
            pl.BlockSpec((TB, SSM_INNER), blk_map),
            pl.BlockSpec((None, SSM_INNER, LANES), seq_map),
        ],
        scratch_shapes=[
            pltpu.VMEM((SSM_INNER, LANES), f32),
            pltpu.VMEM((TB + 2 * SUBLANES, SSM_XBC), f32),
        ],
    )
    return pl.pallas_call(
        functools.partial(_ssd_kernel, rev=rev),
        grid_spec=grid_spec,
        out_shape=[
            jax.ShapeDtypeStruct((lay.m, SSM_INNER), f32),
            jax.ShapeDtypeStruct((lay.n_seq, SSM_INNER, LANES), f32),
        ],
        compiler_params=pltpu.CompilerParams(dimension_semantics=("arbitrary",)),
        name="ssd_bwd" if rev else "ssd_fwd",
    )(*lay.scan_tables(rev), proj, proj, proj, proj, convw, cpar, par, h0)


def ssm_state_to_lanes(h):
    n = h.shape[0]
    hg = h.reshape(n, SSM_GROUPS, SSM_HEADS // SSM_GROUPS * SSM_HEADDIM, SSM_DSTATE)
    z = jnp.zeros_like(hg[:, 0])
    return jnp.concatenate([jnp.concatenate([hg[:, 0], z], -1), jnp.concatenate([z, hg[:, 1]], -1)], 1)


def ssm_state_from_lanes(hl):
    n = hl.shape[0]
    r = hl.reshape(n, SSM_GROUPS, SSM_INNER // SSM_GROUPS, SSM_GROUPS, SSM_DSTATE)
    h = jnp.stack([r[:, g, :, g, :] for g in range(SSM_GROUPS)], 1)
    return h.reshape(n, SSM_HEADS, SSM_HEADDIM, SSM_DSTATE)


ADA_TN = 1536
VMEM_LIMIT = 56 * 1024 * 1024


def _ada_kernel(c_ref, w_ref, b_ref, o_ref):
    s = _silu(c_ref[...])
    o_ref[...] = jnp.dot(s.astype(bf16), w_ref[...].astype(bf16), preferred_element_type=f32) + b_ref[...]


def ada_mod(cvec, w_ada, b_ada, layer):
    n = 6 * D_MODEL
    out = pl.pallas_call(
        _ada_kernel,
        grid=(n // ADA_TN,),
        in_specs=[
            pl.BlockSpec((SUBLANES, D_MODEL), lambda j: (0, 0)),
            pl.BlockSpec((None, D_MODEL, ADA_TN), lambda j: (layer, 0, j)),
            pl.BlockSpec((None, 1, ADA_TN), lambda j: (layer, 0, j)),
        ],
        out_specs=pl.BlockSpec((SUBLANES, ADA_TN), lambda j: (0, j)),
        out_shape=jax.ShapeDtypeStruct((SUBLANES, n), f32),
        compiler_params=pltpu.CompilerParams(dimension_semantics=("arbitrary",), vmem_limit_bytes=VMEM_LIMIT),
        name="ada_mod",
    )(cvec, w_ada, b_ada)
    return out.reshape(SUBLANES, 6, D_MODEL)


MOD_SH1, MOD_SC1, MOD_G1, MOD_SH2, MOD_SC2, MOD_G2 = range(6)


def _rms(x):
    return x * lax.rsqrt(jnp.mean(x * x, axis=-1, keepdims=True) + EPS)


PROJ_CHUNK = 512


def _pick_rows(i, n_first, refs):
    if len(refs) == 1:
        return refs[0][...]
    return jnp.where(i < n_first, refs[0][...], refs[1][...])


def _row_specs(lay, x, width):
    if not isinstance(x, tuple):
        return [pl.BlockSpec((TB, width), lambda i, mod: (jnp.minimum(i, lay.n_blk - 1), 0))], [x]
    nbc, nbl = lay.m_ctx // TB, lay.m_lat // TB
    return ([pl.BlockSpec((TB, width), lambda i, mod: (jnp.minimum(i, nbc - 1), 0)),
             pl.BlockSpec((TB, width), lambda i, mod: (jnp.clip(i - nbc, 0, nbl - 1), 0))], list(x))


def _in_proj_kernel(mod_ref, *refs, has_res, n_x, n_first):
    del mod_ref
    i = pl.program_id(0)
    x_refs, refs = refs[:n_x], refs[n_x:]
    if has_res:
        y_ref, pmods_ref, mods_ref, n1_ref, w_ref, proj_ref, xo_ref = refs
        x = _pick_rows(i, n_first, x_refs) + pmods_ref[MOD_G2:MOD_G2 + 1, :] * y_ref[...]
        xo_ref[...] = x
    else:
        mods_ref, n1_ref, w_ref, proj_ref = refs
        x = _pick_rows(i, n_first, x_refs)
    h = _rms(x) * n1_ref[...]
    h = h * (1.0 + mods_ref[MOD_SC1:MOD_SC1 + 1, :]) + mods_ref[MOD_SH1:MOD_SH1 + 1, :]
    hb = h.astype(bf16)
    for c0 in range(0, PROJ_W, PROJ_CHUNK):
        wd = min(PROJ_CHUNK, PROJ_W - c0)
        proj_ref[:, c0:c0 + wd] = jnp.dot(hb, w_ref[:, c0:c0 + wd], preferred_element_type=f32)


def in_proj(lay, x, mods, norm1, w_in_r, layer, res=None):
    last = lay.n_blk - 1
    row_map = lambda i, mod: (jnp.minimum(i, last), 0)
    mod_map = lambda i, mod: (mod[jnp.minimum(i, last)], 0, 0)
    in_specs, args = _row_specs(lay, x, D_MODEL)
    n_x = len(args)
    if res is not None:
        in_specs += [pl.BlockSpec((TB, D_MODEL), row_map), pl.BlockSpec((None, 6, D_MODEL), mod_map)]
        args += [res[0], res[1]]
    in_specs += [
        pl.BlockSpec((None, 6, D_MODEL), mod_map),
        pl.BlockSpec((None, 1, D_MODEL), lambda i, mod: (layer, 0, 0)),
        pl.BlockSpec((D_MODEL, PROJ_W), lambda i, mod: (0, 0), pipeline_mode=pl.Buffered(1)),
    ]
    args += [mods, norm1, w_in_r]
    out_specs = [pl.BlockSpec((TB, PROJ_W), row_map)]
    out_shape = [jax.ShapeDtypeStruct((lay.m, PROJ_W), f32)]
    if res is not None:
        out_specs.append(pl.BlockSpec((TB, D_MODEL), row_map))
        out_shape.append(jax.ShapeDtypeStruct((lay.m, D_MODEL), f32))
    outs = pl.pallas_call(
        functools.partial(_in_proj_kernel, has_res=res is not None, n_x=n_x, n_first=lay.m_ctx // TB),
        grid_spec=pltpu.PrefetchScalarGridSpec(
            num_scalar_prefetch=1, grid=(lay.n_blk,), in_specs=in_specs, out_specs=out_specs),
        out_shape=out_shape,
        compiler_params=pltpu.CompilerParams(dimension_semantics=("arbitrary",), vmem_limit_bytes=VMEM_LIMIT),
        name="in_proj",
    )(jnp.asarray(lay.mod), *args)
    return (outs[0], outs[1]) if res is not None else (outs[0], x)


MLA_QK = MLA_NOPE + MLA_ROPE
Q_HEADS_W = MLA_HEADS * LANES
ROPE_HALF = MLA_ROPE // 2


def _rope_lanes(x, c, s1, s2):
    return x * c + pltpu.roll(x, ROPE_HALF, 1) * s1 + pltpu.roll(x, LANES - ROPE_HALF, 1) * s2


def _mla_prep_kernel(pos_ref, mq_ref, ckv_ref, small_ref, qn_ref, kvn_ref, wuq_ref, rc_ref, rs1_ref, rs2_ref,
                     q_ref, ckvn_ref, kpe_ref):
    del pos_ref
    c, s1, s2 = rc_ref[...], rs1_ref[...], rs2_ref[...]
    cq = (_rms(mq_ref[...]) * qn_ref[...]).astype(bf16)
    qf = jnp.dot(cq, wuq_ref[...], preferred_element_type=f32)
    scale = MLA_QK ** -0.5 * math.log2(math.e)
    for h in range(MLA_HEADS):
        xh = qf[:, h * LANES:(h + 1) * LANES]
        q_ref[:, h * LANES:(h + 1) * LANES] = (_rope_lanes(xh, c, s1, s2) * scale).astype(bf16)
    ckvn_ref[...] = _rms(ckv_ref[...]) * kvn_ref[...]
    lane = lax.broadcasted_iota(jnp.int32, (TB, LANES), 1)
    in_rope = jnp.logical_and(lane >= SM_KPE, lane < SM_KPE + MLA_ROPE)
    kpe_ref[...] = jnp.where(in_rope, _rope_lanes(small_ref[...], c, s1, s2), 0.0)


def rope_tables(lay):
    t = lay.t_lat
    rows = t // GRID_W
    row = jnp.repeat(jnp.arange(rows, dtype=f32), GRID_W)
    col = jnp.tile(jnp.arange(GRID_W, dtype=f32), rows)
    nf = MLA_ROPE // 4
    inv = jnp.power(ROPE_THETA, -jnp.arange(nf, dtype=f32) / nf)
    ang = jnp.concatenate([row[:, None] * inv, col[:, None] * inv], axis=-1)
    cos, sin = jnp.cos(ang), jnp.sin(ang)
    z = lambda n: jnp.zeros((t, n), f32)
    c = jnp.concatenate([jnp.ones((t, MLA_NOPE), f32), cos, cos, z(LANES - MLA_QK)], -1)
    s1 = jnp.concatenate([z(MLA_NOPE + ROPE_HALF), sin, z(LANES - MLA_QK)], -1)
    s2 = jnp.concatenate([z(MLA_NOPE), -sin, z(ROPE_HALF + LANES - MLA_QK)], -1)
    ident = jnp.concatenate([jnp.ones((TB, MLA_QK), f32), jnp.zeros((TB, LANES - MLA_QK), f32)], -1)
    zero = jnp.zeros((TB, LANES), f32)
    return (jnp.concatenate([ident, c]), jnp.concatenate([zero, s1]), jnp.concatenate([zero, s2]))


def _pos_blocks(lay):
    pos = [0] * (lay.m_ctx // TB)
    for _ in range(lay.n_lat):
        pos += [1 + j for j in range(lay.t_lat // TB)]
    return np.array(pos + pos[-1:], np.int32)


def mla_prep(lay, proj, q_norm, kv_norm, w_uq_r, tables, layer):
    row_map = lambda i, pos: (i, 0)
    tab_map = lambda i, pos: (pos[i], 0)
    return pl.pallas_call(
        _mla_prep_kernel,
        grid_spec=pltpu.PrefetchScalarGridSpec(
            num_scalar_prefetch=1, grid=(lay.n_blk,),
            in_specs=[
                pl.BlockSpec((TB, Q_RANK), lambda i, pos: (i, COL_MQ // Q_RANK)),
                pl.BlockSpec((TB, KV_RANK), lambda i, pos: (i, COL_CKV // KV_RANK)),
                pl.BlockSpec((TB, LANES), lambda i, pos: (i, COL_SMALL // LANES)),
                pl.BlockSpec((None, 1, Q_RANK), lambda i, pos: (layer, 0, 0)),
                pl.BlockSpec((None, 1, KV_RANK), lambda i, pos: (layer, 0, 0)),
                pl.BlockSpec((Q_RANK, Q_HEADS_W), lambda i, pos: (0, 0)),
                pl.BlockSpec((TB, LANES), tab_map),
                pl.BlockSpec((TB, LANES), tab_map),
                pl.BlockSpec((TB, LANES), tab_map),
            ],
            out_specs=[
                pl.BlockSpec((TB, Q_HEADS_W), row_map),
                pl.BlockSpec((TB, KV_RANK), row_map),
                pl.BlockSpec((TB, LANES), row_map),
            ]),
        out_shape=[
            jax.ShapeDtypeStruct((lay.m, Q_HEADS_W), bf16),
            jax.ShapeDtypeStruct((lay.m, KV_RANK), f32),
            jax.ShapeDtypeStruct((lay.m, LANES), f32),
        ],
        compiler_params=pltpu.CompilerParams(dimension_semantics=("arbitrary",)),
        name="mla_prep",
    )(jnp.asarray(_pos_blocks(lay)), proj, proj, proj, q_norm, kv_norm, w_uq_r, *tables)


V_PAIRS = MLA_HEADS // 2
KV_UP_W = 2 * MLA_HEADS * LANES


def _kv_up_kernel(ckv_ref, kpe_ref, w_ref, k_ref, v_ref):
    kv = jnp.dot(ckv_ref[...].astype(bf16), w_ref[...], preferred_element_type=f32)
    kpe = kpe_ref[...]
    for h in range(MLA_HEADS):
        k_ref[h] = (kv[:, h * LANES:(h + 1) * LANES] + kpe).astype(bf16)
    v0 = MLA_HEADS * LANES
    lane_lo = lax.broadcasted_iota(jnp.int32, (TB, LANES), 1) < MLA_VDIM
    for h in range(MLA_HEADS):
        v_ref[h] = jnp.where(lane_lo, kv[:, v0 + h * LANES:v0 + (h + 1) * LANES], 1.0).astype(bf16)


def kv_up(ckv, kpe, w_ukv_r):
    r = ckv.shape[0]
    return pl.pallas_call(
        _kv_up_kernel,
        grid=(r // TB,),
        in_specs=[
            pl.BlockSpec((TB, KV_RANK), lambda i: (i, 0)),
            pl.BlockSpec((TB, LANES), lambda i: (i, 0)),
            pl.BlockSpec((KV_RANK, KV_UP_W), lambda i: (0, 0)),
        ],
        out_specs=[
            pl.BlockSpec((MLA_HEADS, TB, LANES), lambda i: (0, i, 0)),
            pl.BlockSpec((MLA_HEADS, TB, LANES), lambda i: (0, i, 0)),
        ],
        out_shape=[
            jax.ShapeDtypeStruct((MLA_HEADS, r, LANES), bf16),
            jax.ShapeDtypeStruct((MLA_HEADS, r, LANES), bf16),
        ],
        compiler_params=pltpu.CompilerParams(dimension_semantics=("arbitrary",)),
        name="kv_up",
    )(ckv, kpe, w_ukv_r)


ATT_KC = 2304


def _attn_kernel(q_ref, k_ref, v_ref, o_ref, m_ref, acc_ref, *, n_keys):
    lane_lo = lax.broadcasted_iota(jnp.int32, (TB, LANES), 1) < MLA_VDIM
    kc = min(ATT_KC, n_keys)
    hs = range(MLA_HEADS)
    m_ref[...] = jnp.full((MLA_HEADS, TB, LANES), -jnp.inf, f32)
    acc_ref[...] = jnp.zeros((MLA_HEADS, TB, LANES), f32)

    def body(c, carry):
        c0 = pl.multiple_of(c * kc, kc)
        score = lambda h: lax.dot_general(q_ref[:, h * LANES:(h + 1) * LANES], k_ref[h, pl.ds(c0, kc), :],
                                          (((1,), (1,)), ((), ())), preferred_element_type=f32)
        def softmax_part(h, s):
            m_old = m_ref[h]
            m_new = jnp.maximum(m_old, jnp.max(s, axis=-1, keepdims=True))
            m_ref[h] = m_new
            return jnp.exp2(m_old - m_new), jnp.exp2(s - m_new[:, 0:1]).astype(bf16)

        def value_part(h, alpha, pe):
            pv = jnp.dot(pe, v_ref[h, pl.ds(c0, kc), :], preferred_element_type=f32)
            acc_ref[h] = alpha * acc_ref[h] + pv

        scores, probs = {0: score(0), 1: score(1)}, {}
        probs[0] = softmax_part(0, scores.pop(0))
        for h in hs:
            if h + 2 < MLA_HEADS:
                scores[h + 2] = score(h + 2)
            if h + 1 < MLA_HEADS:
                probs[h + 1] = softmax_part(h + 1, scores.pop(h + 1))
            value_part(h, *probs.pop(h))
        return carry

    lax.fori_loop(0, n_keys // kc, body, 0)
    half = LANES // 2
    for p in range(V_PAIRS):
        a_even = acc_ref[2 * p]
        a_odd = pltpu.roll(acc_ref[2 * p + 1], half, 1)
        o_even = a_even * pltpu.roll(1.0 / a_even, half, 1)
        o_odd = a_odd * pltpu.roll(1.0 / a_odd, half, 1)
        o_ref[:, p * LANES:(p + 1) * LANES] = jnp.where(lane_lo, o_even, o_odd)


def attention(q, k, v, q_blk0, n_seq, t_q, n_keys):
    qb = t_q // TB
    return pl.pallas_call(
        functools.partial(_attn_kernel, n_keys=n_keys),
        grid=(n_seq, qb),
        in_specs=[
            pl.BlockSpec((TB, Q_HEADS_W), lambda b, j: (q_blk0 + b * qb + j, 0)),
            pl.BlockSpec((MLA_HEADS, n_keys, LANES), lambda b, j: (0, b, 0)),
            pl.BlockSpec((MLA_HEADS, n_keys, LANES), lambda b, j: (0, b, 0)),
        ],
        out_specs=pl.BlockSpec((TB, MLA_W), lambda b, j: (b * qb + j, 0)),
        out_shape=jax.ShapeDtypeStruct((n_seq * t_q, MLA_W), f32),
        scratch_shapes=[pltpu.VMEM((MLA_HEADS, TB, LANES), f32)] * 2,
        compiler_params=pltpu.CompilerParams(dimension_semantics=("arbitrary", "arbitrary"),
                                             vmem_limit_bytes=VMEM_LIMIT),
        name="mla_attn",
    )(q, k, v)


N_PAIRS = E_PER_GROUP * (E_PER_GROUP - 1) // 2
N_BUCKETS = N_EGROUPS * N_PAIRS
RT_BUCKET, RT_WLO, RT_WHI = 0, 1, 2
H2X_W = D_MODEL + LANES


def _route(lg):
    lane = lax.broadcasted_iota(jnp.int32, lg.shape, 1)
    big = jnp.int32(LANES)
    rmax = lambda x: jnp.max(x, axis=-1, keepdims=True)
    rmin = lambda x: jnp.min(x, axis=-1, keepdims=True)
    rsum = lambda x: jnp.sum(x, axis=-1, keepdims=True)
    neg = -jnp.inf
    gmask = lane < N_EGROUPS
    gl = jnp.where(gmask, lg, neg)
    gmax = rmax(gl)
    gsum = rsum(jnp.where(gmask, jnp.exp(lg - gmax), 0.0))
    gsel = rmin(jnp.where(gl == gmax, lane, big))
    pg_sel = 1.0 / gsum
    e0 = N_EGROUPS + E_PER_GROUP * gsel
    emask = jnp.logical_and(lane >= e0, lane < e0 + E_PER_GROUP)
    el = jnp.where(emask, lg, neg)
    emax = rmax(el)
    ee = jnp.where(emask, jnp.exp(lg - emax), 0.0)
    pe = ee / rsum(ee)
    pe_m = jnp.where(emask, pe, -1.0)
    v1 = rmax(pe_m)
    i1 = rmin(jnp.where(pe_m == v1, lane, big))
    pe_m2 = jnp.where(lane == i1, -1.0, pe_m)
    v2 = rmax(pe_m2)
    i2 = rmin(jnp.where(pe_m2 == v2, lane, big))
    w1 = pg_sel * v1 / (v1 + v2)
    w2 = pg_sel * v2 / (v1 + v2)
    a1, a2 = i1 - e0, i2 - e0
    lo, hi = jnp.minimum(a1, a2), jnp.maximum(a1, a2)
    pair = (lo * (2 * E_PER_GROUP - 1 - lo)) // 2 + (hi - lo - 1)
    bucket = (gsel * N_PAIRS + pair).astype(f32)
    w_lo = jnp.where(a1 < a2, w1, w2)
    w_hi = jnp.where(a1 < a2, w2, w1)
    return jnp.where(lane == RT_BUCKET, bucket,
                     jnp.where(lane == RT_WLO, w_lo, jnp.where(lane == RT_WHI, w_hi, 0.0)))


def _out_proj_kernel(mod_ref, *refs, n_x, n_first):
    del mod_ref
    x_refs = refs[:n_x]
    (of_ref, ob_ref, gz_ref, yf_ref, yb_ref, sz_ref, om_ref, mods_ref,
     gn_ref, sn_ref, n2_ref, wout_ref, wr_ref, br_ref, x1_ref, h2_ref, rt_ref) = refs[n_x:]
    acc = None
    for h in range(GDN_HEADS):
        sl = slice(h * GDN_DV, (h + 1) * GDN_DV)
        o = _rms(of_ref[:, sl] + ob_ref[:, sl]) * gn_ref[...] * _silu(gz_ref[:, sl])
        t = jnp.dot(o.astype(bf16), wout_ref[sl, :], preferred_element_type=f32)
        acc = t if acc is None else acc + t
    y = (yf_ref[...] + yb_ref[...]) * _silu(sz_ref[...])
    ys = _rms(y) * sn_ref[...]
    acc = acc + jnp.dot(ys.astype(bf16), wout_ref[GDN_W:GDN_W + SSM_INNER, :], preferred_element_type=f32)
    acc = acc + jnp.dot(om_ref[...].astype(bf16), wout_ref[GDN_W + SSM_INNER:MIX_W, :],
                        preferred_element_type=f32)
    x1 = _pick_rows(pl.program_id(0), n_first, x_refs) + mods_ref[MOD_G1:MOD_G1 + 1, :] * acc
    x1_ref[...] = x1
    h2 = _rms(x1) * n2_ref[...]
    h2 = h2 * (1.0 + mods_ref[MOD_SC2:MOD_SC2 + 1, :]) + mods_ref[MOD_SH2:MOD_SH2 + 1, :]
    h2_hi = h2.astype(bf16)
    h2_lo = (h2 - h2_hi.astype(f32)).astype(bf16)
    r = jnp.dot(h2_hi, wr_ref[...], preferred_element_type=f32)
    logits = r[:, 0:LANES] + r[:, LANES:2 * LANES] + jnp.dot(h2_lo, wr_ref[:, 0:LANES], preferred_element_type=f32)
    rt = _route(logits + br_ref[...])
    h2_ref[:, 0:D_MODEL] = h2
    h2_ref[:, D_MODEL:H2X_W] = rt
    rt_ref[...] = rt


def out_proj(lay, o_f, o_b, proj, y_f, y_b, o_mla, x, mods, gdn_norm, ssm_norm, norm2, w_out_b, w_rt, b_rt, layer):
    row_map = lambda i, mod: (i, 0)
    lyr = lambda i, mod: (layer, 0, 0)
    x_specs, x_args = _row_specs(lay, x, D_MODEL)
    return pl.pallas_call(
        functools.partial(_out_proj_kernel, n_x=len(x_args), n_first=lay.m_ctx // TB),
        grid_spec=pltpu.PrefetchScalarGridSpec(
            num_scalar_prefetch=1, grid=(lay.n_blk,),
            in_specs=x_specs + [
                pl.BlockSpec((TB, GDN_W), row_map),
                pl.BlockSpec((TB, GDN_W), row_map),
                pl.BlockSpec((TB, GDN_W), lambda i, mod: (i, COL_GZ // GDN_W)),
                pl.BlockSpec((TB, SSM_INNER), row_map),
                pl.BlockSpec((TB, SSM_INNER), row_map),
                pl.BlockSpec((TB, SSM_INNER), lambda i, mod: (i, COL_SZ // SSM_INNER)),
                pl.BlockSpec((TB, MLA_W), row_map),
                pl.BlockSpec((None, 6, D_MODEL), lambda i, mod: (mod[i], 0, 0)),
                pl.BlockSpec((None, 1, GDN_DV), lyr),
                pl.BlockSpec((None, 1, SSM_INNER), lyr),
                pl.BlockSpec((None, 1, D_MODEL), lyr),
                pl.BlockSpec((None, MIX_W, D_MODEL), lyr, pipeline_mode=pl.Buffered(1)),
                pl.BlockSpec((None, D_MODEL, 2 * LANES), lyr),
                pl.BlockSpec((None, 1, LANES), lyr),
            ],
            out_specs=[
                pl.BlockSpec((TB, D_MODEL), row_map),
                pl.BlockSpec((TB, H2X_W), row_map),
                pl.BlockSpec((TB, LANES), row_map),
            ]),
        out_shape=[
            jax.ShapeDtypeStruct((lay.m, D_MODEL), f32),
            jax.ShapeDtypeStruct((lay.m, H2X_W), f32),
            jax.ShapeDtypeStruct((lay.m, LANES), f32),
        ],
        compiler_params=pltpu.CompilerParams(dimension_semantics=("arbitrary",), vmem_limit_bytes=VMEM_LIMIT),
        name="out_proj",
    )(jnp.asarray(lay.mod), *x_args, o_f, o_b, proj, y_f, y_b, proj, o_mla, mods, gdn_norm, ssm_norm, norm2,
      w_out_b, w_rt, b_rt)


TME = 256
PAIR_LO = (0, 0, 0, 1, 1, 2)
PAIR_HI = (1, 2, 3, 2, 3, 3)


DMA_UNROLL = 8


def _moe_kernel(te_ref, nu_ref, ts_ref, nv_ref, order_ref,
                h2_hbm, wg_ref, wu_ref, wd_ref, y_hbm, xbuf, ybuf, gsem, ssem):
    del te_ref
    t = pl.program_id(0)
    j = pl.program_id(1)
    n_used = nu_ref[0]
    used = t < n_used
    slot = t % 2
    n_tok = y_hbm.shape[0] - TME

    def gather_rows(tile, sl):
        base = ts_ref[tile]

        def issue(r, c):
            pltpu.make_async_copy(h2_hbm.at[pl.ds(order_ref[base + r], 1), :],
                                  xbuf.at[sl, pl.ds(r, 1), :], gsem.at[sl]).start()
            return c
        lax.fori_loop(0, TME, issue, 0, unroll=DMA_UNROLL)

    def scatter_rows(tile, sl):
        base = ts_ref[tile]
        nvalid = nv_ref[tile]

        def issue(r, c):
            row = jnp.where(r < nvalid, order_ref[base + r], n_tok + r)
            pltpu.make_async_copy(ybuf.at[sl, pl.ds(r, 1), :],
                                  y_hbm.at[pl.ds(row, 1), :], ssem.at[sl]).start()
            return c
        lax.fori_loop(0, TME, issue, 0, unroll=DMA_UNROLL)

    def wait_gather(sl):
        pltpu.make_async_copy(h2_hbm.at[pl.ds(0, TME), :], xbuf.at[sl], gsem.at[sl]).wait()

    def wait_scatter(sl):
        pltpu.make_async_copy(ybuf.at[sl], y_hbm.at[pl.ds(0, TME), :], ssem.at[sl]).wait()

    @pl.when(jnp.logical_and(used, jnp.logical_and(t == 0, j == 0)))
    def _():
        gather_rows(0, 0)
        ybuf[1] = jnp.zeros((TME, D_MODEL), f32)
        spare = pltpu.make_async_copy(ybuf.at[1], y_hbm.at[pl.ds(n_tok, TME), :], ssem.at[1])
        spare.start()
        spare.wait()

    @pl.when(jnp.logical_and(used, j == 0))
    def _():
        wait_gather(slot)

    @pl.when(jnp.logical_and(t + 1 < n_used, j == 1))
    def _():
        gather_rows(t + 1, 1 - slot)

    @pl.when(used)
    def _():
        xb = xbuf[slot, :, 0:D_MODEL].astype(bf16)
        hg = jnp.dot(xb, wg_ref[...], preferred_element_type=f32)
        hu = jnp.dot(xb, wu_ref[...], preferred_element_type=f32)
        hdn = (_silu(hg) * hu).astype(bf16)
        yo = jnp.dot(hdn, wd_ref[...], preferred_element_type=f32)
        lane = lax.broadcasted_iota(jnp.int32, (TME, LANES), 1)
        w_lane = jnp.where(j == slot, RT_WLO, RT_WHI)
        wsel = jnp.sum(jnp.where(lane == w_lane, xbuf[slot, :, D_MODEL:H2X_W], 0.0), axis=-1, keepdims=True)

        @pl.when(j == 0)
        def _():
            ybuf[slot] = wsel * yo

        @pl.when(j == 1)
        def _():
            ybuf[slot] = ybuf[slot] + wsel * yo

    @pl.when(jnp.logical_and(used, j == 1))
    def _():
        @pl.when(t > 0)
        def _():
            wait_scatter(1 - slot)

        scatter_rows(t, slot)

        @pl.when(t == n_used - 1)
        def _():
            wait_scatter(slot)


def moe_dispatch(lay, route):
    m = lay.m
    n_tiles = m // TME + N_BUCKETS
    bucket = route[:, RT_BUCKET].astype(jnp.int32)
    order = jnp.argsort(bucket).astype(jnp.int32)
    counts = jnp.sum(bucket[:, None] == jnp.arange(N_BUCKETS, dtype=jnp.int32)[None, :], axis=0, dtype=jnp.int32)
    ptiles = (counts + TME - 1) // TME
    pend = jnp.cumsum(ptiles)
    pstart = pend - ptiles
    start = jnp.cumsum(counts) - counts
    n_used = pend[-1]
    tid = jnp.arange(n_tiles, dtype=jnp.int32)
    tid_c = jnp.minimum(tid, n_used - 1)
    tb = jnp.minimum(jnp.sum(pend[None, :] <= tid_c[:, None], axis=1, dtype=jnp.int32), N_BUCKETS - 1)
    sel = tb[:, None] == jnp.arange(N_BUCKETS, dtype=jnp.int32)[None, :]
    pick = lambda v: jnp.sum(jnp.where(sel, v[None, :], 0), axis=1, dtype=jnp.int32)
    off = (tid_c - pick(pstart)) * TME
    tstart = pick(start) + off
    nvalid = jnp.where(tid < n_used, jnp.clip(pick(counts) - off, 0, TME), 0).astype(jnp.int32)
    order_p = jnp.concatenate([order, jnp.zeros((TME,), jnp.int32)])
    grp, pr = tb // N_PAIRS, tb % N_PAIRS
    e_lo = grp * E_PER_GROUP + jnp.asarray(PAIR_LO, jnp.int32)[pr]
    e_hi = grp * E_PER_GROUP + jnp.asarray(PAIR_HI, jnp.int32)[pr]
    swap = (tid % 2) == 1
    te = jnp.stack([jnp.where(swap, e_hi, e_lo), jnp.where(swap, e_lo, e_hi)], -1).reshape(-1)
    te = jnp.concatenate([te, te[-2:]])
    return (te.astype(jnp.int32), n_used.reshape(1).astype(jnp.int32), tstart.astype(jnp.int32), nvalid,
            order_p, n_tiles)


def moe_apply(lay, h2, route, wg_b, wu_b, wd_b, layer):
    te, n_used, tstart, nvalid, order_p, n_tiles = moe_dispatch(lay, route)
    wmap = lambda t, j, te, *_: (layer, te[2 * t + j], 0, 0)
    return pl.pallas_call(
        _moe_kernel,
        grid_spec=pltpu.PrefetchScalarGridSpec(
            num_scalar_prefetch=5, grid=(n_tiles, 2),
            in_specs=[
                pl.BlockSpec(memory_space=pl.ANY),
                pl.BlockSpec((None, None, D_MODEL, EXPERT_FF), wmap),
                pl.BlockSpec((None, None, D_MODEL, EXPERT_FF), wmap),
                pl.BlockSpec((None, None, EXPERT_FF, D_MODEL), wmap),
            ],
            out_specs=pl.BlockSpec(memory_space=pl.ANY),
            scratch_shapes=[
                pltpu.VMEM((2, TME, H2X_W), f32),
                pltpu.VMEM((2, TME, D_MODEL), f32),
                pltpu.SemaphoreType.DMA((2,)),
                pltpu.SemaphoreType.DMA((2,)),
            ]),
        out_shape=jax.ShapeDtypeStruct((lay.m + TME, D_MODEL), f32),
        compiler_params=pltpu.CompilerParams(dimension_semantics=("arbitrary", "arbitrary"),
                                             vmem_limit_bytes=VMEM_LIMIT),
        name="moe",
    )(te, n_used, tstart, nvalid, order_p, h2, wg_b, wu_b, wd_b)


def _final_kernel(mod_ref, x_ref, y_ref, mods_ref, g_ref, oc_ref, ol_ref, *, n_first):
    del mod_ref
    i = pl.program_id(0)
    x = x_ref[...] + mods_ref[MOD_G2:MOD_G2 + 1, :] * y_ref[...]
    out = _rms(x) * g_ref[...]

    @pl.when(i < n_first)
    def _():
        oc_ref[...] = out

    @pl.when(i >= n_first)
    def _():
        ol_ref[...] = out


def final_norm_apply(lay, x, ytok, mods, g):
    row_map = lambda i, mod: (i, 0)
    nbc = lay.m_ctx // TB
    return pl.pallas_call(
        functools.partial(_final_kernel, n_first=nbc),
        grid_spec=pltpu.PrefetchScalarGridSpec(
            num_scalar_prefetch=1, grid=(lay.n_blk,),
            in_specs=[
                pl.BlockSpec((TB, D_MODEL), row_map),
                pl.BlockSpec((TB, D_MODEL), row_map),
                pl.BlockSpec((None, 6, D_MODEL), lambda i, mod: (mod[i], 0, 0)),
                pl.BlockSpec((1, D_MODEL), lambda i, mod: (0, 0)),
            ],
            out_specs=[pl.BlockSpec((TB, D_MODEL), lambda i, mod: (jnp.minimum(i, nbc - 1), 0)),
                       pl.BlockSpec((TB, D_MODEL), lambda i, mod: (jnp.maximum(i - nbc, 0), 0))]),
        out_shape=[jax.ShapeDtypeStruct((lay.m_ctx, D_MODEL), f32),
                   jax.ShapeDtypeStruct((lay.m_lat, D_MODEL), f32)],
        compiler_params=pltpu.CompilerParams(dimension_semantics=("arbitrary",)),
        name="final_norm",
    )(jnp.asarray(lay.mod), x, ytok, mods, g)


def _reorder_w_in(w):
    g_qkv, g_z, g_a, g_b, s_z, s_xbc, s_dt, m_q, m_kv = jnp.split(w, np.cumsum(IN_SIZES)[:-1], axis=1)
    ckv, kpe = m_kv[:, :KV_RANK], m_kv[:, KV_RANK:]
    z = lambda n: jnp.zeros((w.shape[0], n), w.dtype)
    small = jnp.concatenate([g_a, g_b, s_dt, z(SM_KPE - SM_DT - 2 * SSM_HEADS), kpe, z(LANES - SM_KPE - MLA_ROPE)], 1)
    return jnp.concatenate([g_qkv, g_z, s_z, m_q, ckv, s_xbc, small], 1).astype(bf16)


def _reorder_w_uq(w):
    r = w.reshape(Q_RANK, MLA_HEADS, MLA_QK)
    return jnp.pad(r, ((0, 0), (0, 0), (0, LANES - MLA_QK))).reshape(Q_RANK, Q_HEADS_W).astype(bf16)


def _reorder_w_ukv(w):
    r = w.reshape(KV_RANK, MLA_HEADS, MLA_NOPE + MLA_VDIM)
    k = jnp.pad(r[:, :, :MLA_NOPE], ((0, 0), (0, 0), (0, LANES - MLA_NOPE))).reshape(KV_RANK, MLA_HEADS * LANES)
    v = jnp.pad(r[:, :, MLA_NOPE:], ((0, 0), (0, 0), (0, LANES - MLA_VDIM))).reshape(KV_RANK, MLA_HEADS * LANES)
    return jnp.concatenate([k, v], 1).astype(bf16)


def _lane_row(vals, lane0):
    n = vals.shape[-1]
    return jnp.pad(vals, ((0, 0), (lane0, LANES - lane0 - n)))


def _par_rows(row0, row1):
    z = jnp.zeros_like(row0)
    return jnp.stack([row0, row1] + [z] * (SUBLANES - 2), axis=1)


def kernel(x_prompt, x_sample, cache_ckv, cache_kpe, state_gdn, state_ssm, c, c_ctx, w_ada, b_ada, norm1, norm2, w_in, gdn_conv, gdn_A_log, gdn_dt_bias, gdn_norm, ssm_conv, ssm_conv_bias, ssm_A_log, ssm_dt_bias, ssm_D, ssm_norm, mla_q_norm, mla_w_uq, mla_kv_norm, mla_w_ukv, w_out, router_group, router_group_bias, router_expert, router_expert_bias, moe_w_gate, moe_w_up, moe_w_down, final_norm):
    n_ctx, t_ctx, _ = x_prompt.shape
    n_lat, t_lat, _ = x_sample.shape
    past = cache_ckv.shape[2]
    depth = w_in.shape[0]
    assert n_lat < SUBLANES and t_ctx % TB == 0 and t_lat % TB == 0 and past % TB == 0
    lay = Layout(n_ctx, t_ctx, n_lat, t_lat)
    m_ctx = lay.m_ctx

    x = (x_prompt.reshape(m_ctx, D_MODEL), x_sample.reshape(lay.m_lat, D_MODEL))
    cvec = jnp.concatenate([c_ctx[None], c, jnp.zeros((SUBLANES - 1 - n_lat, D_MODEL), f32)], 0)
    tables = rope_tables(lay)

    b_ada3 = b_ada[:, None, :]
    norm1_3, norm2_3 = norm1[:, None, :], norm2[:, None, :]
    gdn_par = _par_rows(_lane_row(gdn_A_log.reshape(depth, -1), SM_GA), _lane_row(gdn_dt_bias.reshape(depth, -1), SM_GA))
    ssm_par = _par_rows(_lane_row(ssm_A_log.reshape(depth, -1), SM_DT), _lane_row(ssm_dt_bias.reshape(depth, -1), SM_DT))
    d_lanes = jnp.pad(jnp.repeat(ssm_D, SSM_HEADDIM, axis=-1), ((0, 0), (0, SSM_XBC - SSM_INNER)))
    ssm_cpar = _par_rows(ssm_conv_bias, d_lanes)
    w_rt = jnp.pad(jnp.concatenate([router_group, router_expert], -1),
                   ((0, 0), (0, 0), (0, LANES - N_EGROUPS - N_EXPERTS)))
    w_rt_hi = w_rt.astype(bf16)
    w_rt = jnp.concatenate([w_rt_hi, (w_rt - w_rt_hi.astype(f32)).astype(bf16)], -1)
    b_rt = jnp.pad(jnp.concatenate([router_group_bias, router_expert_bias], -1),
                   ((0, 0), (0, LANES - N_EGROUPS - N_EXPERTS)))[:, None, :]
    kpe_cache = jnp.pad(cache_kpe, ((0, 0), (0, 0), (0, 0), (SM_KPE, LANES - SM_KPE - MLA_ROPE)))

    w_out_b = w_out.astype(bf16)
    wg_b, wu_b, wd_b = moe_w_gate.astype(bf16), moe_w_up.astype(bf16), moe_w_down.astype(bf16)

    ckvs, kpes, gdns, ssms = [], [], [], []
    ytok = mods_prev = None
    for l in range(depth):
        mods = ada_mod(cvec, w_ada, b_ada3, l)
        res = None if l == 0 else (ytok, mods_prev)
        proj, x = in_proj(lay, x, mods, norm1_3, _reorder_w_in(w_in[l]), l, res)

        zg = jnp.zeros((n_ctx, GDN_HEADS, GDN_DK, GDN_DV), f32)
        o_f, sg_f = gdn_scan(lay, proj, gdn_conv, gdn_par, jnp.concatenate([zg, state_gdn[:, l, 0]], 0), l, False)
        o_b, sg_b = gdn_scan(lay, proj, gdn_conv, gdn_par, jnp.concatenate([zg, state_gdn[:, l, 1]], 0), l, True)

        zs = jnp.zeros((n_ctx, SSM_INNER, LANES), f32)
        y_f, hs_f = ssd_scan(lay, proj, ssm_conv, ssm_cpar, ssm_par,
                             jnp.concatenate([zs, ssm_state_to_lanes(state_ssm[:, l, 0])], 0), l, False)
        y_b, hs_b = ssd_scan(lay, proj, ssm_conv, ssm_cpar, ssm_par,
                             jnp.concatenate([zs, ssm_state_to_lanes(state_ssm[:, l, 1])], 0), l, True)

        q, ckvn, kpe = mla_prep(lay, proj, mla_q_norm[:, None, :], mla_kv_norm[:, None, :],
                                _reorder_w_uq(mla_w_uq[l]), tables, l)
        w_ukv_r = _reorder_w_ukv(mla_w_ukv[l])
        k_ctx, v_ctx = kv_up(ckvn[:m_ctx], kpe[:m_ctx], w_ukv_r)
        ckv_lat = jnp.concatenate([cache_ckv[:, l], ckvn[m_ctx:].reshape(n_lat, t_lat, KV_RANK)], 1)
        kpe_lat = jnp.concatenate([kpe_cache[:, l], kpe[m_ctx:].reshape(n_lat, t_lat, LANES)], 1)
        k_lat, v_lat = kv_up(ckv_lat.reshape(-1, KV_RANK), kpe_lat.reshape(-1, LANES), w_ukv_r)
        o_mla = jnp.concatenate([
            attention(q, k_ctx, v_ctx, 0, n_ctx, t_ctx, t_ctx),
            attention(q, k_lat, v_lat, m_ctx // TB, n_lat, t_lat, past + t_lat)], 0)

        x, h2, route = out_proj(lay, o_f, o_b, proj, y_f, y_b, o_mla, x, mods, gdn_norm[:, None, :],
                                ssm_norm[:, None, :], norm2_3, w_out_b, w_rt, b_rt, l)
        ytok = moe_apply(lay, h2, route, wg_b, wu_b, wd_b, l)
        mods_prev = mods

        ckvs.append(ckvn[:m_ctx].reshape(n_ctx, t_ctx, KV_RANK))
        kpes.append(kpe[:m_ctx, SM_KPE:SM_KPE + MLA_ROPE].reshape(n_ctx, t_ctx, MLA_ROPE))
        gdns.append(jnp.stack([sg_f[:n_ctx], sg_b[:n_ctx]], 1))
        ssms.append(jnp.stack([ssm_state_from_lanes(hs_f[:n_ctx]), ssm_state_from_lanes(hs_b[:n_ctx])], 1))

    y_ctx, y_lat = final_norm_apply(lay, x, ytok, mods_prev, final_norm[None, :])
    return (y_ctx.reshape(n_ctx, t_ctx, D_MODEL), y_lat.reshape(n_lat, t_lat, D_MODEL),
            jnp.stack(ckvs, 1), jnp.stack(kpes, 1), jnp.stack(gdns, 1), jnp.stack(ssms, 1))
```

```python
import functools
import math

import numpy as np
import jax
import jax.numpy as jnp
from jax import lax
from jax.experimental import pallas as pl
from jax.experimental.pallas import tpu as pltpu

f32 = jnp.float32
bf16 = jnp.bfloat16

D_MODEL = 2048
GRID_W = 64
EPS = 1e-6
CONV_W = 5
CHUNK = 64

GDN_HEADS = 4
GDN_DK = 128
GDN_DV = 128
GDN_W = GDN_HEADS * GDN_DV
GDN_QKV = 2 * GDN_HEADS * GDN_DK + GDN_HEADS * GDN_DV

SSM_HEADS = 16
SSM_HEADDIM = 64
SSM_GROUPS = 2
SSM_DSTATE = 64
SSM_INNER = SSM_HEADS * SSM_HEADDIM
SSM_XBC = SSM_INNER + 2 * SSM_GROUPS * SSM_DSTATE

MLA_HEADS = 8
MLA_NOPE = 64
MLA_ROPE = 32
MLA_VDIM = 64
Q_RANK = 512
KV_RANK = 256
MLA_W = MLA_HEADS * MLA_VDIM
ROPE_THETA = 10000.0

MIX_W = GDN_W + SSM_INNER + MLA_W
IN_SIZES = (GDN_QKV, GDN_W, 2 * GDN_HEADS, 2 * GDN_HEADS, SSM_INNER, SSM_XBC, 2 * SSM_HEADS, Q_RANK, KV_RANK + MLA_ROPE)

N_EGROUPS = 4
E_PER_GROUP = 4
N_EXPERTS = N_EGROUPS * E_PER_GROUP
EXPERT_FF = 512

LANES = 128
SUBLANES = 8

TB = 256

COL_QKV = 0
COL_GZ = 1536
COL_SZ = 2048
COL_MQ = 3072
COL_CKV = 3584
COL_XBC = 3840
COL_SMALL = 5120
PROJ_W = 5248
SM_GA, SM_GB, SM_DT, SM_KPE = 0, 8, 16, 64


class Layout:
    def __init__(self, n_ctx, t_ctx, n_lat, t_lat):
        self.n_ctx, self.t_ctx, self.n_lat, self.t_lat = n_ctx, t_ctx, n_lat, t_lat
        self.m_ctx = n_ctx * t_ctx
        self.m_lat = n_lat * t_lat
        self.m = self.m_ctx + self.m_lat
        self.n_seq = n_ctx + n_lat
        bc, bl = t_ctx // TB, t_lat // TB
        seq, first, last, mod = [], [], [], []
        for s in range(n_ctx):
            for j in range(bc):
                seq.append(s); first.append(int(j == 0)); last.append(int(j == bc - 1)); mod.append(0)
        for s in range(n_lat):
            for j in range(bl):
                seq.append(n_ctx + s); first.append(int(j == 0)); last.append(int(j == bl - 1)); mod.append(1 + s)
        self.n_blk = len(seq)
        self.seq = np.array(seq, np.int32)
        self.first = np.array(first, np.int32)
        self.last = np.array(last, np.int32)
        self.mod = np.array(mod + mod[-1:], np.int32)
        fwd = np.arange(self.n_blk, dtype=np.int32)
        bwd = []
        i = 0
        while i < self.n_blk:
            j = i
            while self.last[j] == 0:
                j += 1
            bwd.extend(range(j, i - 1, -1))
            i = j + 1
        self.order = {False: fwd, True: np.array(bwd, np.int32)}

    def scan_tables(self, rev):
        order = self.order[rev]
        pad = lambda a: jnp.asarray(np.concatenate([a, a[-1:]]))
        return pad(order), pad(self.seq[order]), pad(self.first[order]), pad(self.last[order])


def _mm(a, b):
    return jnp.dot(a.astype(bf16), b.astype(bf16), preferred_element_type=f32)


def _mm_nt(a, b):
    return lax.dot_general(a.astype(bf16), b.astype(bf16), (((1,), (1,)), ((), ())), preferred_element_type=f32)


def _split3(x):
    hi = x.astype(bf16)
    r = x - hi.astype(f32)
    mid = r.astype(bf16)
    lo = (r - mid.astype(f32)).astype(bf16)
    return hi, mid, lo


def _mm01(m01, x):
    hi, mid, lo = _split3(x)
    d = lambda t: jnp.dot(m01, t, preferred_element_type=f32)
    return d(hi) + d(mid) + d(lo)


def _transpose_exact(eye, x):
    hi, mid, lo = _split3(x)
    d = lambda t: lax.dot_general(eye, t, (((1,), (1,)), ((), ())), preferred_element_type=f32)
    return d(hi) + d(mid) + d(lo)


def _mm3(a, b):
    ah = a.astype(bf16)
    al = (a - ah.astype(f32)).astype(bf16)
    bh = b.astype(bf16)
    bl = (b - bh.astype(f32)).astype(bf16)
    d = lambda x, y: jnp.dot(x, y, preferred_element_type=f32)
    return d(ah, bh) + d(ah, bl) + d(al, bh)


def _sigmoid(x):
    return 1.0 / (1.0 + jnp.exp(-x))


def _silu(x):
    return x * _sigmoid(x)


def _softplus(x):
    return jnp.maximum(x, 0.0) + jnp.log(1.0 + jnp.exp(-jnp.abs(x)))


def _chunk_masks(rev):
    i = lax.broadcasted_iota(jnp.int32, (CHUNK, CHUNK), 0)
    j = lax.broadcasted_iota(jnp.int32, (CHUNK, CHUNK), 1)
    if rev:
        incl, strict = j >= i, j > i
    else:
        incl, strict = j <= i, j < i
    ll = jnp.where(incl, 1.0, 0.0).astype(bf16)
    lls = jnp.where(incl, 0.0, 1.0).astype(bf16)
    return incl, strict, ll, lls


def _fill_ext(ext_ref, main_ref, prev_ref, next_ref, sfirst, slast):
    ext_ref[0:SUBLANES, :] = jnp.where(sfirst == 1, 0.0, prev_ref[...])
    ext_ref[SUBLANES:SUBLANES + TB, :] = main_ref[...]
    ext_ref[SUBLANES + TB:2 * SUBLANES + TB, :] = jnp.where(slast == 1, 0.0, next_ref[...])


def _conv_tile(ext_ref, w_ref, r0, col0, width, bias=None):
    acc = None
    for t in range(CONV_W):
        rows = ext_ref[pl.ds(SUBLANES + r0 - CONV_W // 2 + t, CHUNK), col0:col0 + width]
        term = rows * w_ref[t:t + 1, col0:col0 + width]
        acc = term if acc is None else acc + term
    if bias is not None:
        acc = acc + bias
    return _silu(acc)


INV_BLOCK = 16


def _block_masks():
    i = lax.broadcasted_iota(jnp.int32, (CHUNK, CHUNK), 0)
    j = lax.broadcasted_iota(jnp.int32, (CHUNK, CHUNK), 1)
    blk16 = (i // INV_BLOCK) == (j // INV_BLOCK)
    blk32 = (i // (2 * INV_BLOCK)) == (j // (2 * INV_BLOCK))
    return blk16, blk32


def _unit_tri_inverse_many(lmats, eye, blk16, blk32):
    ps = list(lmats)
    ld = {p: jnp.where(blk16, lmats[p], 0.0) for p in ps}
    t = {p: eye - ld[p] for p in ps}
    pw = {p: _mm3(ld[p], ld[p]) for p in ps}
    for it in range(3):
        t = {p: t[p] + _mm3(t[p], pw[p]) for p in ps}
        if it < 2:
            pw = {p: _mm3(pw[p], pw[p]) for p in ps}
    in32 = jnp.logical_and(blk32, jnp.logical_not(blk16))
    m = {p: _mm3(t[p], jnp.where(in32, lmats[p], 0.0)) for p in ps}
    t = {p: t[p] - _mm3(m[p], t[p]) for p in ps}
    m = {p: _mm3(t[p], jnp.where(blk32, 0.0, lmats[p])) for p in ps}
    return {p: t[p] - _mm3(m[p], t[p]) for p in ps}


def _gdn_kernel(order_ref, seq_ref, first_ref, last_ref,
                main_ref, prev_ref, next_ref, small_ref, convw_ref, par_ref, s0_ref,
                o_ref, sfin_ref, s_ref, ext_ref, *, rev):
    del order_ref, seq_ref
    step = pl.program_id(0)
    sfirst = first_ref[step]
    slast = last_ref[step]
    pfirst, plast = (slast, sfirst) if rev else (sfirst, slast)
    d = 1 if rev else 0

    @pl.when(pfirst == 1)
    def _():
        s_ref[...] = s0_ref[...]

    _fill_ext(ext_ref, main_ref, prev_ref, next_ref, sfirst, slast)

    incl, strict, ll, lls = _chunk_masks(rev)
    eye = jnp.where(lax.broadcasted_iota(jnp.int32, (CHUNK, CHUNK), 0)
                    == lax.broadcasted_iota(jnp.int32, (CHUNK, CHUNK), 1), 1.0, 0.0)
    blk16, blk32 = _block_masks()
    neg_a = -jnp.exp(par_ref[0:1, :])
    dt_bias = par_ref[1:2, :]
    n_chunks = TB // CHUNK

    chunk_order = [n_chunks - 1 - cc if rev else cc for cc in range(n_chunks)]
    heads = range(GDN_HEADS)
    eye_k = jnp.where(lax.broadcasted_iota(jnp.int32, (GDN_DK, GDN_DK), 0)
                      == lax.broadcasted_iota(jnp.int32, (GDN_DK, GDN_DK), 1), 1.0, 0.0).astype(bf16)

    gates = {}
    for c in chunk_order:
        sm = small_ref[c * CHUNK:(c + 1) * CHUNK, :]
        la_blk = neg_a * _softplus(sm + dt_bias)
        gates[c] = (la_blk, _sigmoid(sm))
    g_blk = {c: _mm01(ll, gates[c][0]) for c in chunk_order}
    gr_blk = {c: _mm01(lls, gates[c][0]) for c in chunk_order}
    gl_blk = {c: jnp.sum(gates[c][0], axis=0, keepdims=True) for c in chunk_order}

    probs = [(c, h) for c in chunk_order for h in heads]
    ia = lambda h: SM_GA + d * GDN_HEADS + h
    ib = lambda h: SM_GB + d * GDN_HEADS + h
    col = lambda blk, lane: blk[:, lane:lane + 1]
    beta = {p: col(gates[p[0]][1], ib(p[1])) for p in probs}
    q, k, v = {}, {}, {}
    for p in probs:
        c, h = p
        r0 = c * CHUNK
        qq = _conv_tile(ext_ref, convw_ref, r0, h * GDN_DK, GDN_DK)
        kx = _conv_tile(ext_ref, convw_ref, r0, GDN_HEADS * GDN_DK + h * GDN_DK, GDN_DK)
        v[p] = _conv_tile(ext_ref, convw_ref, r0, 2 * GDN_HEADS * GDN_DK + h * GDN_DV, GDN_DV)
        q[p] = qq * lax.rsqrt(jnp.sum(qq * qq, axis=-1, keepdims=True) + EPS) * (GDN_DK ** -0.5)
        k[p] = kx * lax.rsqrt(jnp.sum(kx * kx, axis=-1, keepdims=True) + EPS)
    g_t = {c: _transpose_exact(eye_k, g_blk[c]) for c in chunk_order}
    dmat = {p: jnp.where(incl, col(g_blk[p[0]], ia(p[1])) - g_t[p[0]][ia(p[1]):ia(p[1]) + 1, :], 0.0)
            for p in probs}
    kk = {p: _mm_nt(k[p], k[p]) for p in probs}
    qk = {p: _mm_nt(q[p], k[p]) for p in probs}
    decay = {p: jnp.where(incl, jnp.exp(dmat[p]), 0.0) for p in probs}
    lmat = {p: jnp.where(strict, kk[p] * beta[p] * decay[p], 0.0) for p in probs}
    qk = {p: jnp.where(incl, qk[p] * decay[p], 0.0) for p in probs}
    tinv = _unit_tri_inverse_many(lmat, eye, blk16, blk32)
    uw = {}
    for p in probs:
        e_g = jnp.exp(col(g_blk[p[0]], ia(p[1])))
        rhs = jnp.concatenate([v[p] * beta[p], k[p] * beta[p] * e_g], axis=1)
        uw[p] = _mm3(tinv[p], rhs)
        q[p] = q[p] * e_g
    kdt = {}
    for p in probs:
        k_dec = (k[p] * jnp.exp(col(gr_blk[p[0]], ia(p[1])))).astype(bf16)
        kdt[p] = lax.dot_general(eye_k, k_dec, (((1,), (1,)), ((), ())), preferred_element_type=f32)

    for c in chunk_order:
        ps = [(c, h) for h in heads]
        s_old = {p: s_ref[p[1]] for p in ps}
        ws = {p: _mm(uw[p][:, GDN_DV:], s_old[p]) for p in ps}
        qs = {p: _mm(q[p], s_old[p]) for p in ps}
        v_new = {p: uw[p][:, :GDN_DV] - ws[p] for p in ps}
        o = {p: qs[p] + _mm(qk[p], v_new[p]) for p in ps}
        for p in ps:
            h = p[1]
            s_ref[h] = s_old[p] * jnp.exp(col(gl_blk[c], ia(h))) + _mm(kdt[p], v_new[p])
            o_ref[c * CHUNK:(c + 1) * CHUNK, h * GDN_DV:(h + 1) * GDN_DV] = o[p]

    @pl.when(plast == 1)
    def _():
        sfin_ref[...] = s_ref[...]


def _halo_maps(n_rows8):
    prev_map = lambda i, order, *_: (jnp.maximum(order[i] * (TB // SUBLANES) - 1, 0), 0)
    next_map = lambda i, order, *_: (jnp.minimum((order[i] + 1) * (TB // SUBLANES), n_rows8 - 1), 0)
    return prev_map, next_map


def gdn_scan(lay, proj, convw, par, s0, layer, rev):
    prev_map, next_map = _halo_maps(lay.m // SUBLANES)
    blk_map = lambda i, order, *_: (order[i], 0)
    seq_map = lambda i, order, seq, *_: (seq[i], 0, 0, 0)
    grid_spec = pltpu.PrefetchScalarGridSpec(
        num_scalar_prefetch=4,
        grid=(lay.n_blk,),
        in_specs=[
            pl.BlockSpec((TB, GDN_QKV), blk_map),
            pl.BlockSpec((SUBLANES, GDN_QKV), prev_map),
            pl.BlockSpec((SUBLANES, GDN_QKV), next_map),
            pl.BlockSpec((TB, LANES), lambda i, order, *_: (order[i], COL_SMALL // LANES)),
            pl.BlockSpec((None, CONV_W, GDN_QKV), lambda i, *_: (layer, 0, 0)),
            pl.BlockSpec((None, SUBLANES, LANES), lambda i, *_: (layer, 0, 0)),
            pl.BlockSpec((None, GDN_HEADS, GDN_DK, GDN_DV), seq_map),
        ],
        out_specs=[
            pl.BlockSpec((TB, GDN_W), blk_map),
            pl.BlockSpec((None, GDN_HEADS, GDN_DK, GDN_DV), seq_map),
        ],
        scratch_shapes=[
            pltpu.VMEM((GDN_HEADS, GDN_DK, GDN_DV), f32),
            pltpu.VMEM((TB + 2 * SUBLANES, GDN_QKV), f32),
        ],
    )
    return pl.pallas_call(
        functools.partial(_gdn_kernel, rev=rev),
        grid_spec=grid_spec,
        out_shape=[
            jax.ShapeDtypeStruct((lay.m, GDN_W), f32),
            jax.ShapeDtypeStruct((lay.n_seq, GDN_HEADS, GDN_DK, GDN_DV), f32),
        ],
        compiler_params=pltpu.CompilerParams(dimension_semantics=("arbitrary",)),
        name="gdn_bwd" if rev else "gdn_fwd",
    )(*lay.scan_tables(rev), proj, proj, proj, proj, convw, par, s0)


SSM_PAIRS = SSM_HEADS // 2
PAIRS_PER_GROUP = SSM_PAIRS // SSM_GROUPS
COL_B = SSM_INNER
COL_C = SSM_INNER + SSM_GROUPS * SSM_DSTATE


def _ssd_kernel(order_ref, seq_ref, first_ref, last_ref,
                main_ref, prev_ref, next_ref, small_ref, convw_ref, cpar_ref, par_ref, h0_ref,
                y_ref, hfin_ref, h_ref, ext_ref, *, rev):
    del order_ref, seq_ref
    step = pl.program_id(0)
    sfirst = first_ref[step]
    slast = last_ref[step]
    pfirst, plast = (slast, sfirst) if rev else (sfirst, slast)
    d = 1 if rev else 0

    @pl.when(pfirst == 1)
    def _():
        h_ref[...] = h0_ref[...]

    _fill_ext(ext_ref, main_ref, prev_ref, next_ref, sfirst, slast)

    incl, _, ll, lls = _chunk_masks(rev)
    lane_lo = lax.broadcasted_iota(jnp.int32, (CHUNK, LANES), 1) < SSM_DSTATE
    row_lo = lax.broadcasted_iota(jnp.int32, (LANES, LANES), 0) < SSM_HEADDIM
    eye_p = jnp.where(lax.broadcasted_iota(jnp.int32, (LANES, LANES), 0)
                      == lax.broadcasted_iota(jnp.int32, (LANES, LANES), 1), 1.0, 0.0).astype(bf16)
    neg_a = -jnp.exp(par_ref[0:1, :])
    dt_bias = par_ref[1:2, :]
    n_chunks = TB // CHUNK

    for cc in range(n_chunks):
        c = n_chunks - 1 - cc if rev else cc
        r0 = c * CHUNK
        sm = small_ref[r0:r0 + CHUNK, :]
        dt_blk = _softplus(sm + dt_bias)
        dta_blk = dt_blk * neg_a
        acum_blk = _mm01(ll, dta_blk)
        ar_blk = _mm01(lls, dta_blk)
        al_blk = jnp.sum(dta_blk, axis=0, keepdims=True)
        b_pair = _conv_tile(ext_ref, convw_ref, r0, COL_B, LANES, cpar_ref[0:1, COL_B:COL_B + LANES])
        c_pair = _conv_tile(ext_ref, convw_ref, r0, COL_C, LANES, cpar_ref[0:1, COL_C:COL_C + LANES])
        c_g, b_g, cb = [], [], []
        for g in range(SSM_GROUPS):
            gmask = lane_lo if g == 0 else jnp.logical_not(lane_lo)
            c_g.append(jnp.where(gmask, c_pair, 0.0))
            b_g.append(jnp.where(gmask, b_pair, 0.0))
        cb = [_mm_nt(c_g[g], b_pair) for g in range(SSM_GROUPS)]
        pairs = range(SSM_PAIRS)
        grp = lambda p: p // PAIRS_PER_GROUP
        lane_of = lambda hh: SM_DT + d * SSM_HEADS + hh
        colv = lambda blk, hh: blk[:, lane_of(hh):lane_of(hh) + 1]
        both = lambda blk, p: jnp.where(lane_lo, colv(blk, 2 * p), colv(blk, 2 * p + 1))
        xs = [_conv_tile(ext_ref, convw_ref, r0, p * LANES, LANES, cpar_ref[0:1, p * LANES:(p + 1) * LANES])
              for p in pairs]
        acum_t = _transpose_exact(eye_p, acum_blk)
        dmat = [jnp.where(incl, colv(acum_blk, hh) - acum_t[lane_of(hh):lane_of(hh) + 1, :], 0.0)
                for hh in range(SSM_HEADS)]
        m_h = [cb[grp(hh // 2)] * jnp.where(incl, jnp.exp(dmat[hh]), 0.0) for hh in range(SSM_HEADS)]
        xdt = [xs[p] * both(dt_blk, p) for p in pairs]
        y = [jnp.where(lane_lo, _mm(m_h[2 * p], xdt[p]), _mm(m_h[2 * p + 1], xdt[p])) for p in pairs]
        xdt_t = [lax.dot_general(eye_p, xdt[p].astype(bf16), (((1,), (1,)), ((), ())),
                                 preferred_element_type=f32).astype(bf16) for p in pairs]
        st_new = [jnp.where(row_lo,
                            jnp.dot(xdt_t[p], (b_g[grp(p)] * jnp.exp(colv(ar_blk, 2 * p))).astype(bf16),
                                    preferred_element_type=f32),
                            jnp.dot(xdt_t[p], (b_g[grp(p)] * jnp.exp(colv(ar_blk, 2 * p + 1))).astype(bf16),
                                    preferred_element_type=f32)) for p in pairs]
        h_pair = [h_ref[p * LANES:(p + 1) * LANES, :] for p in pairs]
        y_off = [_mm_nt(c_g[grp(p)], h_pair[p]) for p in pairs]
        for p in pairs:
            col = p * LANES
            yp = y[p] + y_off[p] * jnp.exp(both(acum_blk, p))
            if not rev:
                yp = yp + xs[p] * cpar_ref[1:2, col:col + LANES]
            dec = jnp.where(row_lo, jnp.exp(colv(al_blk, 2 * p)), jnp.exp(colv(al_blk, 2 * p + 1)))
            h_ref[col:col + LANES, :] = h_pair[p] * dec + st_new[p]
            y_ref[r0:r0 + CHUNK, col:col + LANES] = yp

    @pl.when(plast == 1)
    def _():
        hfin_ref[...] = h_ref[...]


def ssd_scan(lay, proj, convw, cpar, par, h0, layer, rev):
    prev_map, next_map = _halo_maps(lay.m // SUBLANES)
    cblk = COL_XBC // SSM_XBC
    blk_map = lambda i, order, *_: (order[i], 0)
    seq_map = lambda i, order, seq, *_: (seq[i], 0, 0)
    grid_spec = pltpu.PrefetchScalarGridSpec(
        num_scalar_prefetch=4,
        grid=(lay.n_blk,),
        in_specs=[
            pl.BlockSpec((TB, SSM_XBC), lambda i, order, *_: (order[i], cblk)),
            pl.BlockSpec((SUBLANES, SSM_XBC), lambda i, *a: (prev_map(i, *a)[0], cblk)),
            pl.BlockSpec((SUBLANES, SSM_XBC), lambda i, *a: (next_map(i, *a)[0], cblk)),
            pl.BlockSpec((TB, LANES), lambda i, order, *_: (order[i], COL_SMALL // LANES)),
            pl.BlockSpec((None, CONV_W, SSM_XBC), lambda i, *_: (layer, 0, 0)),
            pl.BlockSpec((None, SUBLANES, SSM_XBC), lambda i, *_: (layer, 0, 0)),
            pl.BlockSpec((None, SUBLANES, LANES), lambda i, *_: (layer, 0, 0)),
            pl.BlockSpec((None, SSM_INNER, LANES), seq_map),
        ],
        out_specs=[
            pl.BlockSpec((TB, SSM_INNER), blk_map),
            pl.BlockSpec((None, SSM_INNER, LANES), seq_map),
        ],
        scratch_shapes=[
            pltpu.VMEM((SSM_INNER, LANES), f32),
            pltpu.VMEM((TB + 2 * SUBLANES, SSM_XBC), f32),
        ],
    )
    return pl.pallas_call(
        functools.partial(_ssd_kernel, rev=rev),
        grid_spec=grid_spec,
        out_shape=[
            jax.ShapeDtypeStruct((lay.m, SSM_INNER), f32),
            jax.ShapeDtypeStruct((lay.n_seq, SSM_INNER, LANES), f32),
        ],
        compiler_params=pltpu.CompilerParams(dimension_semantics=("arbitrary",)),
        name="ssd_bwd" if rev else "ssd_fwd",
    )(*lay.scan_tables(rev), proj, proj, proj, proj, convw, cpar, par, h0)


def ssm_state_to_lanes(h):
    n = h.shape[0]
    hg = h.reshape(n, SSM_GROUPS, SSM_HEADS // SSM_GROUPS * SSM_HEADDIM, SSM_DSTATE)
    z = jnp.zeros_like(hg[:, 0])
    return jnp.concatenate([jnp.concatenate([hg[:, 0], z], -1), jnp.concatenate([z, hg[:, 1]], -1)], 1)


def ssm_state_from_lanes(hl):
    n = hl.shape[0]
    r = hl.reshape(n, SSM_GROUPS, SSM_INNER // SSM_GROUPS, SSM_GROUPS, SSM_DSTATE)
    h = jnp.stack([r[:, g, :, g, :] for g in range(SSM_GROUPS)], 1)
    return h.reshape(n, SSM_HEADS, SSM_HEADDIM, SSM_DSTATE)


ADA_TN = 1536
VMEM_LIMIT = 56 * 1024 * 1024


def _ada_kernel(c_ref, w_ref, b_ref, o_ref):
    s = _silu(c_ref[...])
    o_ref[...] = jnp.dot(s.astype(bf16), w_ref[...].astype(bf16), preferred_element_type=f32) + b_ref[...]


def ada_mod(cvec, w_ada, b_ada, layer):
    n = 6 * D_MODEL
    out = pl.pallas_call(
        _ada_kernel,
        grid=(n // ADA_TN,),
        in_specs=[
            pl.BlockSpec((SUBLANES, D_MODEL), lambda j: (0, 0)),
            pl.BlockSpec((None, D_MODEL, ADA_TN), lambda j: (layer, 0, j)),
            pl.BlockSpec((None, 1, ADA_TN), lambda j: (layer, 0, j)),
        ],
        out_specs=pl.BlockSpec((SUBLANES, ADA_TN), lambda j: (0, j)),
        out_shape=jax.ShapeDtypeStruct((SUBLANES, n), f32),
        compiler_params=pltpu.CompilerParams(dimension_semantics=("arbitrary",), vmem_limit_bytes=VMEM_LIMIT),
        name="ada_mod",
    )(cvec, w_ada, b_ada)
    return out.reshape(SUBLANES, 6, D_MODEL)


MOD_SH1, MOD_SC1, MOD_G1, MOD_SH2, MOD_SC2, MOD_G2 = range(6)


def _rms(x):
    return x * lax.rsqrt(jnp.mean(x * x, axis=-1, keepdims=True) + EPS)


PROJ_CHUNK = 512


def _pick_rows(i, n_first, refs):
    if len(refs) == 1:
        return refs[0][...]
    return jnp.where(i < n_first, refs[0][...], refs[1][...])


def _row_specs(lay, x, width):
    if not isinstance(x, tuple):
        return [pl.BlockSpec((TB, width), lambda i, mod: (jnp.minimum(i, lay.n_blk - 1), 0))], [x]
    nbc, nbl = lay.m_ctx // TB, lay.m_lat // TB
    return ([pl.BlockSpec((TB, width), lambda i, mod: (jnp.minimum(i, nbc - 1), 0)),
             pl.BlockSpec((TB, width), lambda i, mod: (jnp.clip(i - nbc, 0, nbl - 1), 0))], list(x))


def _in_proj_kernel(mod_ref, *refs, has_res, n_x, n_first):
    del mod_ref
    i = pl.program_id(0)
    x_refs, refs = refs[:n_x], refs[n_x:]
    if has_res:
        y_ref, pmods_ref, mods_ref, n1_ref, w_ref, proj_ref, xo_ref = refs
        x = _pick_rows(i, n_first, x_refs) + pmods_ref[MOD_G2:MOD_G2 + 1, :] * y_ref[...]
        xo_ref[...] = x
    else:
        mods_ref, n1_ref, w_ref, proj_ref = refs
        x = _pick_rows(i, n_first, x_refs)
    h = _rms(x) * n1_ref[...]
    h = h * (1.0 + mods_ref[MOD_SC1:MOD_SC1 + 1, :]) + mods_ref[MOD_SH1:MOD_SH1 + 1, :]
    hb = h.astype(bf16)
    for c0 in range(0, PROJ_W, PROJ_CHUNK):
        wd = min(PROJ_CHUNK, PROJ_W - c0)
        proj_ref[:, c0:c0 + wd] = jnp.dot(hb, w_ref[:, c0:c0 + wd], preferred_element_type=f32)


def in_proj(lay, x, mods, norm1, w_in_r, layer, res=None):
    last = lay.n_blk - 1
    row_map = lambda i, mod: (jnp.minimum(i, last), 0)
    mod_map = lambda i, mod: (mod[jnp.minimum(i, last)], 0, 0)
    in_specs, args = _row_specs(lay, x, D_MODEL)
    n_x = len(args)
    if res is not None:
        in_specs += [pl.BlockSpec((TB, D_MODEL), row_map), pl.BlockSpec((None, 6, D_MODEL), mod_map)]
        args += [res[0], res[1]]
    in_specs += [
        pl.BlockSpec((None, 6, D_MODEL), mod_map),
        pl.BlockSpec((None, 1, D_MODEL), lambda i, mod: (layer, 0, 0)),
        pl.BlockSpec((D_MODEL, PROJ_W), lambda i, mod: (0, 0), pipeline_mode=pl.Buffered(1)),
    ]
    args += [mods, norm1, w_in_r]
    out_specs = [pl.BlockSpec((TB, PROJ_W), row_map)]
    out_shape = [jax.ShapeDtypeStruct((lay.m, PROJ_W), f32)]
    if res is not None:
        out_specs.append(pl.BlockSpec((TB, D_MODEL), row_map))
        out_shape.append(jax.ShapeDtypeStruct((lay.m, D_MODEL), f32))
    outs = pl.pallas_call(
        functools.partial(_in_proj_kernel, has_res=res is not None, n_x=n_x, n_first=lay.m_ctx // TB),
        grid_spec=pltpu.PrefetchScalarGridSpec(
            num_scalar_prefetch=1, grid=(lay.n_blk,), in_specs=in_specs, out_specs=out_specs),
        out_shape=out_shape,
        compiler_params=pltpu.CompilerParams(dimension_semantics=("arbitrary",), vmem_limit_bytes=VMEM_LIMIT),
        name="in_proj",
    )(jnp.asarray(lay.mod), *args)
    return (outs[0], outs[1]) if res is not None else (outs[0], x)


MLA_QK = MLA_NOPE + MLA_ROPE
Q_HEADS_W = MLA_HEADS * LANES
ROPE_HALF = MLA_ROPE // 2


def _rope_lanes(x, c, s1, s2):
    return x * c + pltpu.roll(x, ROPE_HALF, 1) * s1 + pltpu.roll(x, LANES - ROPE_HALF, 1) * s2


def _mla_prep_kernel(pos_ref, mq_ref, ckv_ref, small_ref, qn_ref, kvn_ref, wuq_ref, rc_ref, rs1_ref, rs2_ref,
                     q_ref, ckvn_ref, kpe_ref):
    del pos_ref
    c, s1, s2 = rc_ref[...], rs1_ref[...], rs2_ref[...]
    cq = (_rms(mq_ref[...]) * qn_ref[...]).astype(bf16)
    qf = jnp.dot(cq, wuq_ref[...], preferred_element_type=f32)
    scale = MLA_QK ** -0.5 * math.log2(math.e)
    for h in range(MLA_HEADS):
        xh = qf[:, h * LANES:(h + 1) * LANES]
        q_ref[:, h * LANES:(h + 1) * LANES] = (_rope_lanes(xh, c, s1, s2) * scale).astype(bf16)
    ckvn_ref[...] = _rms(ckv_ref[...]) * kvn_ref[...]
    lane = lax.broadcasted_iota(jnp.int32, (TB, LANES), 1)
    in_rope = jnp.logical_and(lane >= SM_KPE, lane < SM_KPE + MLA_ROPE)
    kpe_ref[...] = jnp.where(in_rope, _rope_lanes(small_ref[...], c, s1, s2), 0.0)


def rope_tables(lay):
    t = lay.t_lat
    rows = t // GRID_W
    row = jnp.repeat(jnp.arange(rows, dtype=f32), GRID_W)
    col = jnp.tile(jnp.arange(GRID_W, dtype=f32), rows)
    nf = MLA_ROPE // 4
    inv = jnp.power(ROPE_THETA, -jnp.arange(nf, dtype=f32) / nf)
    ang = jnp.concatenate([row[:, None] * inv, col[:, None] * inv], axis=-1)
    cos, sin = jnp.cos(ang), jnp.sin(ang)
    z = lambda n: jnp.zeros((t, n), f32)
    c = jnp.concatenate([jnp.ones((t, MLA_NOPE), f32), cos, cos, z(LANES - MLA_QK)], -1)
    s1 = jnp.concatenate([z(MLA_NOPE + ROPE_HALF), sin, z(LANES - MLA_QK)], -1)
    s2 = jnp.concatenate([z(MLA_NOPE), -sin, z(ROPE_HALF + LANES - MLA_QK)], -1)
    ident = jnp.concatenate([jnp.ones((TB, MLA_QK), f32), jnp.zeros((TB, LANES - MLA_QK), f32)], -1)
    zero = jnp.zeros((TB, LANES), f32)
    return (jnp.concatenate([ident, c]), jnp.concatenate([zero, s1]), jnp.concatenate([zero, s2]))


def _pos_blocks(lay):
    pos = [0] * (lay.m_ctx // TB)
    for _ in range(lay.n_lat):
        pos += [1 + j for j in range(lay.t_lat // TB)]
    return np.array(pos + pos[-1:], np.int32)


def mla_prep(lay, proj, q_norm, kv_norm, w_uq_r, tables, layer):
    row_map = lambda i, pos: (i, 0)
    tab_map = lambda i, pos: (pos[i], 0)
    return pl.pallas_call(
        _mla_prep_kernel,
        grid_spec=pltpu.PrefetchScalarGridSpec(
            num_scalar_prefetch=1, grid=(lay.n_blk,),
            in_specs=[
                pl.BlockSpec((TB, Q_RANK), lambda i, pos: (i, COL_MQ // Q_RANK)),
                pl.BlockSpec((TB, KV_RANK), lambda i, pos: (i, COL_CKV // KV_RANK)),
                pl.BlockSpec((TB, LANES), lambda i, pos: (i, COL_SMALL // LANES)),
                pl.BlockSpec((None, 1, Q_RANK), lambda i, pos: (layer, 0, 0)),
                pl.BlockSpec((None, 1, KV_RANK), lambda i, pos: (layer, 0, 0)),
                pl.BlockSpec((Q_RANK, Q_HEADS_W), lambda i, pos: (0, 0)),
                pl.BlockSpec((TB, LANES), tab_map),
                pl.BlockSpec((TB, LANES), tab_map),
                pl.BlockSpec((TB, LANES), tab_map),
            ],
            out_specs=[
                pl.BlockSpec((TB, Q_HEADS_W), row_map),
                pl.BlockSpec((TB, KV_RANK), row_map),
                pl.BlockSpec((TB, LANES), row_map),
            ]),
        out_shape=[
            jax.ShapeDtypeStruct((lay.m, Q_HEADS_W), bf16),
            jax.ShapeDtypeStruct((lay.m, KV_RANK), f32),
            jax.ShapeDtypeStruct((lay.m, LANES), f32),
        ],
        compiler_params=pltpu.CompilerParams(dimension_semantics=("arbitrary",)),
        name="mla_prep",
    )(jnp.asarray(_pos_blocks(lay)), proj, proj, proj, q_norm, kv_norm, w_uq_r, *tables)


V_PAIRS = MLA_HEADS // 2
KV_UP_W = 2 * MLA_HEADS * LANES


def _kv_up_kernel(ckv_ref, kpe_ref, w_ref, k_ref, v_ref):
    kv = jnp.dot(ckv_ref[...].astype(bf16), w_ref[...], preferred_element_type=f32)
    kpe = kpe_ref[...]
    for h in range(MLA_HEADS):
        k_ref[h] = (kv[:, h * LANES:(h + 1) * LANES] + kpe).astype(bf16)
    v0 = MLA_HEADS * LANES
    lane_lo = lax.broadcasted_iota(jnp.int32, (TB, LANES), 1) < MLA_VDIM
    for h in range(MLA_HEADS):
        v_ref[h] = jnp.where(lane_lo, kv[:, v0 + h * LANES:v0 + (h + 1) * LANES], 1.0).astype(bf16)


def kv_up(ckv, kpe, w_ukv_r):
    r = ckv.shape[0]
    return pl.pallas_call(
        _kv_up_kernel,
        grid=(r // TB,),
        in_specs=[
            pl.BlockSpec((TB, KV_RANK), lambda i: (i, 0)),
            pl.BlockSpec((TB, LANES), lambda i: (i, 0)),
            pl.BlockSpec((KV_RANK, KV_UP_W), lambda i: (0, 0)),
        ],
        out_specs=[
            pl.BlockSpec((MLA_HEADS, TB, LANES), lambda i: (0, i, 0)),
            pl.BlockSpec((MLA_HEADS, TB, LANES), lambda i: (0, i, 0)),
        ],
        out_shape=[
            jax.ShapeDtypeStruct((MLA_HEADS, r, LANES), bf16),
            jax.ShapeDtypeStruct((MLA_HEADS, r, LANES), bf16),
        ],
        compiler_params=pltpu.CompilerParams(dimension_semantics=("arbitrary",)),
        name="kv_up",
    )(ckv, kpe, w_ukv_r)


ATT_KC = 2304


def _attn_kernel(q_ref, k_ref, v_ref, o_ref, m_ref, acc_ref, *, n_keys):
    lane_lo = lax.broadcasted_iota(jnp.int32, (TB, LANES), 1) < MLA_VDIM
    kc = min(ATT_KC, n_keys)
    hs = range(MLA_HEADS)
    m_ref[...] = jnp.full((MLA_HEADS, TB, LANES), -jnp.inf, f32)
    acc_ref[...] = jnp.zeros((MLA_HEADS, TB, LANES), f32)

    def body(c, carry):
        c0 = pl.multiple_of(c * kc, kc)
        score = lambda h: lax.dot_general(q_ref[:, h * LANES:(h + 1) * LANES], k_ref[h, pl.ds(c0, kc), :],
                                          (((1,), (1,)), ((), ())), preferred_element_type=f32)
        def softmax_part(h, s):
            m_old = m_ref[h]
            m_new = jnp.maximum(m_old, jnp.max(s, axis=-1, keepdims=True))
            m_ref[h] = m_new
            return jnp.exp2(m_old - m_new), jnp.exp2(s - m_new[:, 0:1]).astype(bf16)

        def value_part(h, alpha, pe):
            pv = jnp.dot(pe, v_ref[h, pl.ds(c0, kc), :], preferred_element_type=f32)
            acc_ref[h] = alpha * acc_ref[h] + pv

        scores, probs = {0: score(0), 1: score(1)}, {}
        probs[0] = softmax_part(0, scores.pop(0))
        for h in hs:
            if h + 2 < MLA_HEADS:
                scores[h + 2] = score(h + 2)
            if h + 1 < MLA_HEADS:
                probs[h + 1] = softmax_part(h + 1, scores.pop(h + 1))
            value_part(h, *probs.pop(h))
        return carry

    lax.fori_loop(0, n_keys // kc, body, 0)
    half = LANES // 2
    for p in range(V_PAIRS):
        a_even = acc_ref[2 * p]
        a_odd = pltpu.roll(acc_ref[2 * p + 1], half, 1)
        o_even = a_even * pltpu.roll(1.0 / a_even, half, 1)
        o_odd = a_odd * pltpu.roll(1.0 / a_odd, half, 1)
        o_ref[:, p * LANES:(p + 1) * LANES] = jnp.where(lane_lo, o_even, o_odd)


def attention(q, k, v, q_blk0, n_seq, t_q, n_keys):
    qb = t_q // TB
    return pl.pallas_call(
        functools.partial(_attn_kernel, n_keys=n_keys),
        grid=(n_seq, qb),
        in_specs=[
            pl.BlockSpec((TB, Q_HEADS_W), lambda b, j: (q_blk0 + b * qb + j, 0)),
            pl.BlockSpec((MLA_HEADS, n_keys, LANES), lambda b, j: (0, b, 0)),
            pl.BlockSpec((MLA_HEADS, n_keys, LANES), lambda b, j: (0, b, 0)),
        ],
        out_specs=pl.BlockSpec((TB, MLA_W), lambda b, j: (b * qb + j, 0)),
        out_shape=jax.ShapeDtypeStruct((n_seq * t_q, MLA_W), f32),
        scratch_shapes=[pltpu.VMEM((MLA_HEADS, TB, LANES), f32)] * 2,
        compiler_params=pltpu.CompilerParams(dimension_semantics=("arbitrary", "arbitrary"),
                                             vmem_limit_bytes=VMEM_LIMIT),
        name="mla_attn",
    )(q, k, v)


N_PAIRS = E_PER_GROUP * (E_PER_GROUP - 1) // 2
N_BUCKETS = N_EGROUPS * N_PAIRS
RT_BUCKET, RT_WLO, RT_WHI = 0, 1, 2
H2X_W = D_MODEL + LANES


def _route(lg):
    lane = lax.broadcasted_iota(jnp.int32, lg.shape, 1)
    big = jnp.int32(LANES)
    rmax = lambda x: jnp.max(x, axis=-1, keepdims=True)
    rmin = lambda x: jnp.min(x, axis=-1, keepdims=True)
    rsum = lambda x: jnp.sum(x, axis=-1, keepdims=True)
    neg = -jnp.inf
    gmask = lane < N_EGROUPS
    gl = jnp.where(gmask, lg, neg)
    gmax = rmax(gl)
    gsum = rsum(jnp.where(gmask, jnp.exp(lg - gmax), 0.0))
    gsel = rmin(jnp.where(gl == gmax, lane, big))
    pg_sel = 1.0 / gsum
    e0 = N_EGROUPS + E_PER_GROUP * gsel
    emask = jnp.logical_and(lane >= e0, lane < e0 + E_PER_GROUP)
    el = jnp.where(emask, lg, neg)
    emax = rmax(el)
    ee = jnp.where(emask, jnp.exp(lg - emax), 0.0)
    pe = ee / rsum(ee)
    pe_m = jnp.where(emask, pe, -1.0)
    v1 = rmax(pe_m)
    i1 = rmin(jnp.where(pe_m == v1, lane, big))
    pe_m2 = jnp.where(lane == i1, -1.0, pe_m)
    v2 = rmax(pe_m2)
    i2 = rmin(jnp.where(pe_m2 == v2, lane, big))
    w1 = pg_sel * v1 / (v1 + v2)
    w2 = pg_sel * v2 / (v1 + v2)
    a1, a2 = i1 - e0, i2 - e0
    lo, hi = jnp.minimum(a1, a2), jnp.maximum(a1, a2)
    pair = (lo * (2 * E_PER_GROUP - 1 - lo)) // 2 + (hi - lo - 1)
    bucket = (gsel * N_PAIRS + pair).astype(f32)
    w_lo = jnp.where(a1 < a2, w1, w2)
    w_hi = jnp.where(a1 < a2, w2, w1)
    return jnp.where(lane == RT_BUCKET, bucket,
                     jnp.where(lane == RT_WLO, w_lo, jnp.where(lane == RT_WHI, w_hi, 0.0)))


def _out_proj_kernel(mod_ref, *refs, n_x, n_first):
    del mod_ref
    x_refs = refs[:n_x]
    (of_ref, ob_ref, gz_ref, yf_ref, yb_ref, sz_ref, om_ref, mods_ref,
     gn_ref, sn_ref, n2_ref, wout_ref, wr_ref, br_ref, x1_ref, h2_ref, rt_ref) = refs[n_x:]
    acc = None
    for h in range(GDN_HEADS):
        sl = slice(h * GDN_DV, (h + 1) * GDN_DV)
        o = _rms(of_ref[:, sl] + ob_ref[:, sl]) * gn_ref[...] * _silu(gz_ref[:, sl])
        t = jnp.dot(o.astype(bf16), wout_ref[sl, :], preferred_element_type=f32)
        acc = t if acc is None else acc + t
    y = (yf_ref[...] + yb_ref[...]) * _silu(sz_ref[...])
    ys = _rms(y) * sn_ref[...]
    acc = acc + jnp.dot(ys.astype(bf16), wout_ref[GDN_W:GDN_W + SSM_INNER, :], preferred_element_type=f32)
    acc = acc + jnp.dot(om_ref[...].astype(bf16), wout_ref[GDN_W + SSM_INNER:MIX_W, :],
                        preferred_element_type=f32)
    x1 = _pick_rows(pl.program_id(0), n_first, x_refs) + mods_ref[MOD_G1:MOD_G1 + 1, :] * acc
    x1_ref[...] = x1
    h2 = _rms(x1) * n2_ref[...]
    h2 = h2 * (1.0 + mods_ref[MOD_SC2:MOD_SC2 + 1, :]) + mods_ref[MOD_SH2:MOD_SH2 + 1, :]
    h2_hi = h2.astype(bf16)
    h2_lo = (h2 - h2_hi.astype(f32)).astype(bf16)
    r = jnp.dot(h2_hi, wr_ref[...], preferred_element_type=f32)
    logits = r[:, 0:LANES] + r[:, LANES:2 * LANES] + jnp.dot(h2_lo, wr_ref[:, 0:LANES], preferred_element_type=f32)
    rt = _route(logits + br_ref[...])
    h2_ref[:, 0:D_MODEL] = h2
    h2_ref[:, D_MODEL:H2X_W] = rt
    rt_ref[...] = rt


def out_proj(lay, o_f, o_b, proj, y_f, y_b, o_mla, x, mods, gdn_norm, ssm_norm, norm2, w_out_b, w_rt, b_rt, layer):
    row_map = lambda i, mod: (i, 0)
    lyr = lambda i, mod: (layer, 0, 0)
    x_specs, x_args = _row_specs(lay, x, D_MODEL)
    return pl.pallas_call(
        functools.partial(_out_proj_kernel, n_x=len(x_args), n_first=lay.m_ctx // TB),
        grid_spec=pltpu.PrefetchScalarGridSpec(
            num_scalar_prefetch=1, grid=(lay.n_blk,),
            in_specs=x_specs + [
                pl.BlockSpec((TB, GDN_W), row_map),
                pl.BlockSpec((TB, GDN_W), row_map),
                pl.BlockSpec((TB, GDN_W), lambda i, mod: (i, COL_GZ // GDN_W)),
                pl.BlockSpec((TB, SSM_INNER), row_map),
                pl.BlockSpec((TB, SSM_INNER), row_map),
                pl.BlockSpec((TB, SSM_INNER), lambda i, mod: (i, COL_SZ // SSM_INNER)),
                pl.BlockSpec((TB, MLA_W), row_map),
                pl.BlockSpec((None, 6, D_MODEL), lambda i, mod: (mod[i], 0, 0)),
                pl.BlockSpec((None, 1, GDN_DV), lyr),
                pl.BlockSpec((None, 1, SSM_INNER), lyr),
                pl.BlockSpec((None, 1, D_MODEL), lyr),
                pl.BlockSpec((None, MIX_W, D_MODEL), lyr, pipeline_mode=pl.Buffered(1)),
                pl.BlockSpec((None, D_MODEL, 2 * LANES), lyr),
                pl.BlockSpec((None, 1, LANES), lyr),
            ],
            out_specs=[
                pl.BlockSpec((TB, D_MODEL), row_map),
                pl.BlockSpec((TB, H2X_W), row_map),
                pl.BlockSpec((TB, LANES), row_map),
            ]),
        out_shape=[
            jax.ShapeDtypeStruct((lay.m, D_MODEL), f32),
            jax.ShapeDtypeStruct((lay.m, H2X_W), f32),
            jax.ShapeDtypeStruct((lay.m, LANES), f32),
        ],
        compiler_params=pltpu.CompilerParams(dimension_semantics=("arbitrary",), vmem_limit_bytes=VMEM_LIMIT),
        name="out_proj",
    )(jnp.asarray(lay.mod), *x_args, o_f, o_b, proj, y_f, y_b, proj, o_mla, mods, gdn_norm, ssm_norm, norm2,
      w_out_b, w_rt, b_rt)


TME = 256
PAIR_LO = (0, 0, 0, 1, 1, 2)
PAIR_HI = (1, 2, 3, 2, 3, 3)


DMA_UNROLL = 8


def _moe_kernel(te_ref, nu_ref, ts_ref, nv_ref, order_ref,
                h2_hbm, wg_ref, wu_ref, wd_ref, y_hbm, xbuf, ybuf, gsem, ssem):
    del te_ref
    t = pl.program_id(0)
    j = pl.program_id(1)
    n_used = nu_ref[0]
    used = t < n_used
    slot = t % 2
    n_tok = y_hbm.shape[0] - TME

    def gather_rows(tile, sl):
        base = ts_ref[tile]

        def issue(r, c):
            pltpu.make_async_copy(h2_hbm.at[pl.ds(order_ref[base + r], 1), :],
                                  xbuf.at[sl, pl.ds(r, 1), :], gsem.at[sl]).start(priority=1)
            return c
        lax.fori_loop(0, TME, issue, 0, unroll=DMA_UNROLL)

    def scatter_rows(tile, sl):
        base = ts_ref[tile]
        nvalid = nv_ref[tile]

        def issue(k, c):
            for prio in range(2):
                r = 2 * k + prio
                row = jnp.where(r < nvalid, order_ref[base + r], n_tok + r)
                pltpu.make_async_copy(ybuf.at[sl, pl.ds(r, 1), :],
                                      y_hbm.at[pl.ds(row, 1), :], ssem.at[sl]).start(priority=prio)
            return c
        lax.fori_loop(0, TME // 2, issue, 0, unroll=DMA_UNROLL // 2)

    def wait_gather(sl):
        pltpu.make_async_copy(h2_hbm.at[pl.ds(0, TME), :], xbuf.at[sl], gsem.at[sl]).wait()

    def wait_scatter(sl):
        pltpu.make_async_copy(ybuf.at[sl], y_hbm.at[pl.ds(0, TME), :], ssem.at[sl]).wait()

    @pl.when(jnp.logical_and(used, jnp.logical_and(t == 0, j == 0)))
    def _():
        gather_rows(0, 0)
        ybuf[1] = jnp.zeros((TME, D_MODEL), f32)
        spare = pltpu.make_async_copy(ybuf.at[1], y_hbm.at[pl.ds(n_tok, TME), :], ssem.at[1])
        spare.start()
        spare.wait()

    @pl.when(jnp.logical_and(used, j == 0))
    def _():
        wait_gather(slot)

    @pl.when(jnp.logical_and(t + 1 < n_used, j == 1))
    def _():
        gather_rows(t + 1, 1 - slot)

    @pl.when(used)
    def _():
        xb = xbuf[slot, :, 0:D_MODEL].astype(bf16)
        hg = jnp.dot(xb, wg_ref[...], preferred_element_type=f32)
        hu = jnp.dot(xb, wu_ref[...], preferred_element_type=f32)
        hdn = (_silu(hg) * hu).astype(bf16)
        yo = jnp.dot(hdn, wd_ref[...], preferred_element_type=f32)
        lane = lax.broadcasted_iota(jnp.int32, (TME, LANES), 1)
        w_lane = jnp.where(j == slot, RT_WLO, RT_WHI)
        wsel = jnp.sum(jnp.where(lane == w_lane, xbuf[slot, :, D_MODEL:H2X_W], 0.0), axis=-1, keepdims=True)

        @pl.when(j == 0)
        def _():
            ybuf[slot] = wsel * yo

        @pl.when(j == 1)
        def _():
            ybuf[slot] = ybuf[slot] + wsel * yo

    @pl.when(jnp.logical_and(used, j == 1))
    def _():
        @pl.when(t > 0)
        def _():
            wait_scatter(1 - slot)

        scatter_rows(t, slot)

        @pl.when(t == n_used - 1)
        def _():
            wait_scatter(slot)


def moe_dispatch(lay, route):
    m = lay.m
    n_tiles = m // TME + N_BUCKETS
    bucket = route[:, RT_BUCKET].astype(jnp.int32)
    order = jnp.argsort(bucket).astype(jnp.int32)
    counts = jnp.sum(bucket[:, None] == jnp.arange(N_BUCKETS, dtype=jnp.int32)[None, :], axis=0, dtype=jnp.int32)
    ptiles = (counts + TME - 1) // TME
    pend = jnp.cumsum(ptiles)
    pstart = pend - ptiles
    start = jnp.cumsum(counts) - counts
    n_used = pend[-1]
    tid = jnp.arange(n_tiles, dtype=jnp.int32)
    tid_c = jnp.minimum(tid, n_used - 1)
    tb = jnp.minimum(jnp.sum(pend[None, :] <= tid_c[:, None], axis=1, dtype=jnp.int32), N_BUCKETS - 1)
    sel = tb[:, None] == jnp.arange(N_BUCKETS, dtype=jnp.int32)[None, :]
    pick = lambda v: jnp.sum(jnp.where(sel, v[None, :], 0), axis=1, dtype=jnp.int32)
    off = (tid_c - pick(pstart)) * TME
    tstart = pick(start) + off
    nvalid = jnp.where(tid < n_used, jnp.clip(pick(counts) - off, 0, TME), 0).astype(jnp.int32)
    order_p = jnp.concatenate([order, jnp.zeros((TME,), jnp.int32)])
    grp, pr = tb // N_PAIRS, tb % N_PAIRS
    e_lo = grp * E_PER_GROUP + jnp.asarray(PAIR_LO, jnp.int32)[pr]
    e_hi = grp * E_PER_GROUP + jnp.asarray(PAIR_HI, jnp.int32)[pr]
    swap = (tid % 2) == 1
    te = jnp.stack([jnp.where(swap, e_hi, e_lo), jnp.where(swap, e_lo, e_hi)], -1).reshape(-1)
    te = jnp.concatenate([te, te[-2:]])
    return (te.astype(jnp.int32), n_used.reshape(1).astype(jnp.int32), tstart.astype(jnp.int32), nvalid,
            order_p, n_tiles)


def moe_apply(lay, h2, route, wg_b, wu_b, wd_b, layer):
    te, n_used, tstart, nvalid, order_p, n_tiles = moe_dispatch(lay, route)
    wmap = lambda t, j, te, *_: (layer, te[2 * t + j], 0, 0)
    return pl.pallas_call(
        _moe_kernel,
        grid_spec=pltpu.PrefetchScalarGridSpec(
            num_scalar_prefetch=5, grid=(n_tiles, 2),
            in_specs=[
                pl.BlockSpec(memory_space=pl.ANY),
                pl.BlockSpec((None, None, D_MODEL, EXPERT_FF), wmap),
                pl.BlockSpec((None, None, D_MODEL, EXPERT_FF), wmap),
                pl.BlockSpec((None, None, EXPERT_FF, D_MODEL), wmap),
            ],
            out_specs=pl.BlockSpec(memory_space=pl.ANY),
            scratch_shapes=[
                pltpu.VMEM((2, TME, H2X_W), f32),
                pltpu.VMEM((2, TME, D_MODEL), f32),
                pltpu.SemaphoreType.DMA((2,)),
                pltpu.SemaphoreType.DMA((2,)),
            ]),
        out_shape=jax.ShapeDtypeStruct((lay.m + TME, D_MODEL), f32),
        compiler_params=pltpu.CompilerParams(dimension_semantics=("arbitrary", "arbitrary"),
                                             vmem_limit_bytes=VMEM_LIMIT),
        name="moe",
    )(te, n_used, tstart, nvalid, order_p, h2, wg_b, wu_b, wd_b)


def _final_kernel(mod_ref, x_ref, y_ref, mods_ref, g_ref, oc_ref, ol_ref, *, n_first):
    del mod_ref
    i = pl.program_id(0)
    x = x_ref[...] + mods_ref[MOD_G2:MOD_G2 + 1, :] * y_ref[...]
    out = _rms(x) * g_ref[...]

    @pl.when(i < n_first)
    def _():
        oc_ref[...] = out

    @pl.when(i >= n_first)
    def _():
        ol_ref[...] = out


def final_norm_apply(lay, x, ytok, mods, g):
    row_map = lambda i, mod: (i, 0)
    nbc = lay.m_ctx // TB
    return pl.pallas_call(
        functools.partial(_final_kernel, n_first=nbc),
        grid_spec=pltpu.PrefetchScalarGridSpec(
            num_scalar_prefetch=1, grid=(lay.n_blk,),
            in_specs=[
                pl.BlockSpec((TB, D_MODEL), row_map),
                pl.BlockSpec((TB, D_MODEL), row_map),
                pl.BlockSpec((None, 6, D_MODEL), lambda i, mod: (mod[i], 0, 0)),
                pl.BlockSpec((1, D_MODEL), lambda i, mod: (0, 0)),
            ],
            out_specs=[pl.BlockSpec((TB, D_MODEL), lambda i, mod: (jnp.minimum(i, nbc - 1), 0)),
                       pl.BlockSpec((TB, D_MODEL), lambda i, mod: (jnp.maximum(i - nbc, 0), 0))]),
        out_shape=[jax.ShapeDtypeStruct((lay.m_ctx, D_MODEL), f32),
                   jax.ShapeDtypeStruct((lay.m_lat, D_MODEL), f32)],
        compiler_params=pltpu.CompilerParams(dimension_semantics=("arbitrary",)),
        name="final_norm",
    )(jnp.asarray(lay.mod), x, ytok, mods, g)


def _reorder_w_in(w):
    g_qkv, g_z, g_a, g_b, s_z, s_xbc, s_dt, m_q, m_kv = jnp.split(w, np.cumsum(IN_SIZES)[:-1], axis=1)
    ckv, kpe = m_kv[:, :KV_RANK], m_kv[:, KV_RANK:]
    z = lambda n: jnp.zeros((w.shape[0], n), w.dtype)
    small = jnp.concatenate([g_a, g_b, s_dt, z(SM_KPE - SM_DT - 2 * SSM_HEADS), kpe, z(LANES - SM_KPE - MLA_ROPE)], 1)
    return jnp.concatenate([g_qkv, g_z, s_z, m_q, ckv, s_xbc, small], 1).astype(bf16)


def _reorder_w_uq(w):
    r = w.reshape(Q_RANK, MLA_HEADS, MLA_QK)
    return jnp.pad(r, ((0, 0), (0, 0), (0, LANES - MLA_QK))).reshape(Q_RANK, Q_HEADS_W).astype(bf16)


def _reorder_w_ukv(w):
    r = w.reshape(KV_RANK, MLA_HEADS, MLA_NOPE + MLA_VDIM)
    k = jnp.pad(r[:, :, :MLA_NOPE], ((0, 0), (0, 0), (0, LANES - MLA_NOPE))).reshape(KV_RANK, MLA_HEADS * LANES)
    v = jnp.pad(r[:, :, MLA_NOPE:], ((0, 0), (0, 0), (0, LANES - MLA_VDIM))).reshape(KV_RANK, MLA_HEADS * LANES)
    return jnp.concatenate([k, v], 1).astype(bf16)


def _lane_row(vals, lane0):
    n = vals.shape[-1]
    return jnp.pad(vals, ((0, 0), (lane0, LANES - lane0 - n)))


def _par_rows(row0, row1):
    z = jnp.zeros_like(row0)
    return jnp.stack([row0, row1] + [z] * (SUBLANES - 2), axis=1)


def kernel(x_prompt, x_sample, cache_ckv, cache_kpe, state_gdn, state_ssm, c, c_ctx, w_ada, b_ada, norm1, norm2, w_in, gdn_conv, gdn_A_log, gdn_dt_bias, gdn_norm, ssm_conv, ssm_conv_bias, ssm_A_log, ssm_dt_bias, ssm_D, ssm_norm, mla_q_norm, mla_w_uq, mla_kv_norm, mla_w_ukv, w_out, router_group, router_group_bias, router_expert, router_expert_bias, moe_w_gate, moe_w_up, moe_w_down, final_norm):
    n_ctx, t_ctx, _ = x_prompt.shape
    n_lat, t_lat, _ = x_sample.shape
    past = cache_ckv.shape[2]
    depth = w_in.shape[0]
    assert n_lat < SUBLANES and t_ctx % TB == 0 and t_lat % TB == 0 and past % TB == 0
    lay = Layout(n_ctx, t_ctx, n_lat, t_lat)
    m_ctx = lay.m_ctx

    x = (x_prompt.reshape(m_ctx, D_MODEL), x_sample.reshape(lay.m_lat, D_MODEL))
    cvec = jnp.concatenate([c_ctx[None], c, jnp.zeros((SUBLANES - 1 - n_lat, D_MODEL), f32)], 0)
    tables = rope_tables(lay)

    b_ada3 = b_ada[:, None, :]
    norm1_3, norm2_3 = norm1[:, None, :], norm2[:, None, :]
    gdn_par = _par_rows(_lane_row(gdn_A_log.reshape(depth, -1), SM_GA), _lane_row(gdn_dt_bias.reshape(depth, -1), SM_GA))
    ssm_par = _par_rows(_lane_row(ssm_A_log.reshape(depth, -1), SM_DT), _lane_row(ssm_dt_bias.reshape(depth, -1), SM_DT))
    d_lanes = jnp.pad(jnp.repeat(ssm_D, SSM_HEADDIM, axis=-1), ((0, 0), (0, SSM_XBC - SSM_INNER)))
    ssm_cpar = _par_rows(ssm_conv_bias, d_lanes)
    w_rt = jnp.pad(jnp.concatenate([router_group, router_expert], -1),
                   ((0, 0), (0, 0), (0, LANES - N_EGROUPS - N_EXPERTS)))
    w_rt_hi = w_rt.astype(bf16)
    w_rt = jnp.concatenate([w_rt_hi, (w_rt - w_rt_hi.astype(f32)).astype(bf16)], -1)
    b_rt = jnp.pad(jnp.concatenate([router_group_bias, router_expert_bias], -1),
                   ((0, 0), (0, LANES - N_EGROUPS - N_EXPERTS)))[:, None, :]
    kpe_cache = jnp.pad(cache_kpe, ((0, 0), (0, 0), (0, 0), (SM_KPE, LANES - SM_KPE - MLA_ROPE)))

    w_out_b = w_out.astype(bf16)
    wg_b, wu_b, wd_b = moe_w_gate.astype(bf16), moe_w_up.astype(bf16), moe_w_down.astype(bf16)

    ckvs, kpes, gdns, ssms = [], [], [], []
    ytok = mods_prev = None
    for l in range(depth):
        mods = ada_mod(cvec, w_ada, b_ada3, l)
        res = None if l == 0 else (ytok, mods_prev)
        proj, x = in_proj(lay, x, mods, norm1_3, _reorder_w_in(w_in[l]), l, res)

        zg = jnp.zeros((n_ctx, GDN_HEADS, GDN_DK, GDN_DV), f32)
        o_f, sg_f = gdn_scan(lay, proj, gdn_conv, gdn_par, jnp.concatenate([zg, state_gdn[:, l, 0]], 0), l, False)
        o_b, sg_b = gdn_scan(lay, proj, gdn_conv, gdn_par, jnp.concatenate([zg, state_gdn[:, l, 1]], 0), l, True)

        zs = jnp.zeros((n_ctx, SSM_INNER, LANES), f32)
        y_f, hs_f = ssd_scan(lay, proj, ssm_conv, ssm_cpar, ssm_par,
                             jnp.concatenate([zs, ssm_state_to_lanes(state_ssm[:, l, 0])], 0), l, False)
        y_b, hs_b = ssd_scan(lay, proj, ssm_conv, ssm_cpar, ssm_par,
                             jnp.concatenate([zs, ssm_state_to_lanes(state_ssm[:, l, 1])], 0), l, True)

        q, ckvn, kpe = mla_prep(lay, proj, mla_q_norm[:, None, :], mla_kv_norm[:, None, :],
                                _reorder_w_uq(mla_w_uq[l]), tables, l)
        w_ukv_r = _reorder_w_ukv(mla_w_ukv[l])
        k_ctx, v_ctx = kv_up(ckvn[:m_ctx], kpe[:m_ctx], w_ukv_r)
        ckv_lat = jnp.concatenate([cache_ckv[:, l], ckvn[m_ctx:].reshape(n_lat, t_lat, KV_RANK)], 1)
        kpe_lat = jnp.concatenate([kpe_cache[:, l], kpe[m_ctx:].reshape(n_lat, t_lat, LANES)], 1)
        k_lat, v_lat = kv_up(ckv_lat.reshape(-1, KV_RANK), kpe_lat.reshape(-1, LANES), w_ukv_r)
        o_mla = jnp.concatenate([
            attention(q, k_ctx, v_ctx, 0, n_ctx, t_ctx, t_ctx),
            attention(q, k_lat, v_lat, m_ctx // TB, n_lat, t_lat, past + t_lat)], 0)

        x, h2, route = out_proj(lay, o_f, o_b, proj, y_f, y_b, o_mla, x, mods, gdn_norm[:, None, :],
                                ssm_norm[:, None, :], norm2_3, w_out_b, w_rt, b_rt, l)
        ytok = moe_apply(lay, h2, route, wg_b, wu_b, wd_b, l)
        mods_prev = mods

        ckvs.append(ckvn[:m_ctx].reshape(n_ctx, t_ctx, KV_RANK))
        kpes.append(kpe[:m_ctx, SM_KPE:SM_KPE + MLA_ROPE].reshape(n_ctx, t_ctx, MLA_ROPE))
        gdns.append(jnp.stack([sg_f[:n_ctx], sg_b[:n_ctx]], 1))
        ssms.append(jnp.stack([ssm_state_from_lanes(hs_f[:n_ctx]), ssm_state_from_lanes(hs_b[:n_ctx])], 1))

    y_ctx, y_lat = final_norm_apply(lay, x, ytok, mods_prev, final_norm[None, :])
    return (y_ctx.reshape(n_ctx, t_ctx, D_MODEL), y_lat.reshape(n_lat, t_lat, D_MODEL),
            jnp.stack(ckvs, 1), jnp.stack(kpes, 1), jnp.stack(gdns, 1), jnp.stack(ssms, 1))
```
